```python
import math
import jax, jax.numpy as jnp
from jax import lax
import numpy as np

D_MODEL = 2048
BATCH = 2
SEQ = 4096
DEPTH = 1

HEAD_DIM = 128
D_MIX = D_MODEL
N_HEADS_A = 8
N_KV_A = 2
N_HEADS_B = 8
N_KV_B = 2
D_FF = 4 * D_MODEL
D_PLE = 256
GRID_W = 64
BLOCK_Q = 128
WINDOW = 128
N_BUCKETS = 32
MAX_DISTANCE = 128
ROPE_THETA = 10000.0
EPS = 1e-6
NEG_INF = -1e30

Q_A = N_HEADS_A * HEAD_DIM
KV_A = N_KV_A * HEAD_DIM
Q_B = N_HEADS_B * HEAD_DIM
KV_B = N_KV_B * HEAD_DIM
D_IN_PROJ = Q_A + 2 * KV_A + Q_B + 2 * KV_B

kernel_name = "hybrid_axial_window_sink_encoder_layer"


def rmsnorm(x, g):
    xf = x.astype(jnp.float32)
    y = xf * lax.rsqrt(jnp.mean(xf * xf, axis=-1, keepdims=True) + EPS)
    return (y * g.astype(jnp.float32)).astype(x.dtype)


def axial_rope_tables(seq):
    rows = seq // GRID_W
    row = jnp.repeat(jnp.arange(rows, dtype=jnp.int32), GRID_W)
    col = jnp.tile(jnp.arange(GRID_W, dtype=jnp.int32), rows)
    half = HEAD_DIM // 2
    inv_freq = ROPE_THETA ** (-jnp.arange(0, half, 2, dtype=jnp.float32) / half)
    ang_r = row.astype(jnp.float32)[:, None] * inv_freq
    ang_c = col.astype(jnp.float32)[:, None] * inv_freq
    return jnp.cos(ang_r), jnp.sin(ang_r), jnp.cos(ang_c), jnp.sin(ang_c)


def _rotate(x, cos, sin):
    x1, x2 = x[..., : x.shape[-1] // 2], x[..., x.shape[-1] // 2:]
    return jnp.concatenate([x1 * cos - x2 * sin, x2 * cos + x1 * sin], axis=-1)


def apply_axial_rope(x, tabs):
    cr, sr, cc, sc = tabs
    xf = x.astype(jnp.float32)
    half = HEAD_DIM // 2
    out = jnp.concatenate([_rotate(xf[..., :half], cr, sr), _rotate(xf[..., half:], cc, sc)], axis=-1)
    return out.astype(x.dtype)


def global_axial_attention(q, k, v, g_q, g_k, tabs):
    B, S, H, D = q.shape
    KV = k.shape[2]
    G = H // KV
    nb = S // BLOCK_Q
    q = apply_axial_rope(rmsnorm(q, g_q).transpose(0, 2, 1, 3), tabs)
    k = apply_axial_rope(rmsnorm(k, g_k).transpose(0, 2, 1, 3), tabs)
    v = v.transpose(0, 2, 1, 3)
    qb = q.reshape(B, KV, G, nb, BLOCK_Q, D).transpose(3, 0, 1, 2, 4, 5)
    scale = D ** -0.5

    def attend(q_blk):
        s = jnp.einsum('bkgqd,bksd->bkgqs', q_blk, k, preferred_element_type=jnp.float32) * scale
        pr = jax.nn.softmax(s, axis=-1)
        return jnp.einsum('bkgqs,bksd->bkgqd', pr.astype(v.dtype), v)

    o = lax.map(attend, qb)
    return o.transpose(1, 0, 4, 2, 3, 5).reshape(B, S, H * D)


def t5_bucket(rel):
    nb = N_BUCKETS // 2
    ret = jnp.where(rel > 0, nb, 0)
    n = jnp.abs(rel)
    max_exact = nb // 2
    nf = jnp.maximum(n, 1).astype(jnp.float32)
    large = max_exact + (jnp.log(nf / max_exact) / math.log(MAX_DISTANCE / max_exact)
                         * (nb - max_exact)).astype(jnp.int32)
    large = jnp.minimum(large, nb - 1)
    return ret + jnp.where(n < max_exact, n, large)


def window_sink_attention(q, k, v, rel_bias_table, sink):
    B, S, H, D = q.shape
    KV = k.shape[2]
    G = H // KV
    nb = S // BLOCK_Q
    Q = BLOCK_Q
    q = q.transpose(0, 2, 1, 3).reshape(B, KV, G, nb, Q, D)
    pad = ((0, 0), (0, 0), (Q, Q), (0, 0))
    kp = jnp.pad(k.transpose(0, 2, 1, 3), pad).reshape(B, KV, nb + 2, Q, D)
    vp = jnp.pad(v.transpose(0, 2, 1, 3), pad).reshape(B, KV, nb + 2, Q, D)
    kband = jnp.concatenate([kp[:, :, :-2], kp[:, :, 1:-1], kp[:, :, 2:]], axis=3)
    vband = jnp.concatenate([vp[:, :, :-2], vp[:, :, 1:-1], vp[:, :, 2:]], axis=3)
    s = jnp.einsum('bkgnqd,bknjd->bkgnqj', q, kband,
                   preferred_element_type=jnp.float32) * (D ** -0.5)
    r = jnp.arange(Q, dtype=jnp.int32)
    j = jnp.arange(3 * Q, dtype=jnp.int32)
    rel = (j[None, :] - Q) - r[:, None]
    bias = rel_bias_table[t5_bucket(rel)].astype(jnp.float32)
    bias = bias.transpose(2, 0, 1).reshape(KV, G, 1, Q, 3 * Q)
    kabs = jnp.arange(nb, dtype=jnp.int32)[:, None] * Q + j[None, :] - Q
    in_range = (kabs >= 0) & (kabs < S)
    mask = (jnp.abs(rel) <= WINDOW)[None, :, :] & in_range[:, None, :]
    s = jnp.where(mask, s + bias, NEG_INF)
    sink_col = jnp.broadcast_to(sink.astype(jnp.float32).reshape(1, KV, G, 1, 1, 1),
                                s.shape[:-1] + (1,))
    pr = jax.nn.softmax(jnp.concatenate([s, sink_col], axis=-1), axis=-1)[..., :-1]
    o = jnp.einsum('bkgnqj,bknjd->bkgnqd', pr.astype(vband.dtype), vband)
    return o.transpose(0, 3, 4, 1, 2, 5).reshape(B, S, H * D)


def setup_inputs(seed: int = 0) -> dict:
    key = jax.random.key(seed)
    ks = jax.random.split(key, 20)
    f32 = jnp.float32

    def nrm(k, shape, scale):
        return jax.random.normal(k, shape, f32) * scale

    def gain(k, shape):
        return 1.0 + 0.02 * jax.random.normal(k, shape, f32)

    return {
        "x": nrm(ks[0], (BATCH, SEQ, D_MODEL), 1.0),
        "p": nrm(ks[1], (DEPTH, BATCH, SEQ, D_PLE), 1.0),
        "attn_norm_g": gain(ks[2], (DEPTH, D_MODEL)),
        "w_in": nrm(ks[3], (DEPTH, D_MODEL, D_IN_PROJ), D_MODEL ** -0.5),
        "q_norm_g": gain(ks[4], (DEPTH, HEAD_DIM)),
        "k_norm_g": gain(ks[5], (DEPTH, HEAD_DIM)),
        "sink_logits": nrm(ks[6], (DEPTH, N_HEADS_B), 1.0),
        "w_out": nrm(ks[7], (DEPTH, D_MIX, D_MODEL), D_MIX ** -0.5),
        "mlp_norm_g": gain(ks[8], (DEPTH, D_MODEL)),
        "w_up": nrm(ks[9], (DEPTH, D_MODEL, D_FF), D_MODEL ** -0.5),
        "w_down": nrm(ks[10], (DEPTH, D_FF, D_MODEL), D_FF ** -0.5),
        "ple_w": nrm(ks[11], (DEPTH, D_PLE, D_MODEL), D_PLE ** -0.5),
        "ple_norm_g": gain(ks[12], (DEPTH, D_MODEL)),
        "gate_norm_g": gain(ks[13], (DEPTH, D_MODEL)),
        "w_gate": nrm(ks[14], (DEPTH, D_MODEL, D_MODEL), D_MODEL ** -0.5),
        "rel_bias_table": nrm(ks[15], (N_BUCKETS, N_HEADS_B), 0.5),
        "final_norm_g": gain(ks[16], (D_MODEL,)),
    }


def reference(x, p, attn_norm_g, w_in, q_norm_g, k_norm_g, sink_logits, w_out,
              mlp_norm_g, w_up, w_down, ple_w, ple_norm_g, gate_norm_g, w_gate,
              rel_bias_table, final_norm_g):
    B, S, _ = x.shape
    tabs = axial_rope_tables(S)
    splits = [Q_A, Q_A + KV_A, Q_A + 2 * KV_A, Q_A + 2 * KV_A + Q_B, Q_A + 2 * KV_A + Q_B + KV_B]
    h = x
    for i in range(DEPTH):
        u = rmsnorm(h, attn_norm_g[i])
        proj = u @ w_in[i]
        qa, ka, va, qb, kb, vb = jnp.split(proj, splits, axis=-1)
        oa = global_axial_attention(
            qa.reshape(B, S, N_HEADS_A, HEAD_DIM), ka.reshape(B, S, N_KV_A, HEAD_DIM),
            va.reshape(B, S, N_KV_A, HEAD_DIM), q_norm_g[i], k_norm_g[i], tabs)
        ob = window_sink_attention(
            qb.reshape(B, S, N_HEADS_B, HEAD_DIM), kb.reshape(B, S, N_KV_B, HEAD_DIM),
            vb.reshape(B, S, N_KV_B, HEAD_DIM), rel_bias_table, sink_logits[i])
        h = h + jnp.concatenate([oa, ob], axis=-1) @ w_out[i]
        m = rmsnorm(h, mlp_norm_g[i])
        h = h + jnp.square(jax.nn.relu(m @ w_up[i])) @ w_down[i]
        e = rmsnorm(p[i] @ ple_w[i], ple_norm_g[i])
        gate = jax.nn.sigmoid(rmsnorm(h, gate_norm_g[i]) @ w_gate[i])
        h = h + gate * e
    return rmsnorm(h, final_norm_g)
```

```python
import functools
import math

import jax
import jax.numpy as jnp
from jax import lax
from jax.experimental import pallas as pl
from jax.experimental.pallas import tpu as pltpu

HEAD_DIM = 128
N_HEADS_A = 8
N_KV_A = 2
N_HEADS_B = 8
N_KV_B = 2
GROUP = 4
GRID_W = 64
BLOCK_Q = 128
WINDOW = 128
N_BUCKETS = 32
MAX_DISTANCE = 128
ROPE_THETA = 10000.0
EPS = 1e-6
NEG_INF = -1e30
SM_SCALE = HEAD_DIM ** -0.5

V7X_VMEM_BYTES = 64 * 1024 * 1024
VMEM_LIMIT_BYTES = 56 * 1024 * 1024

BF16 = jnp.bfloat16
F32 = jnp.float32


def _params(*semantics):
    return pltpu.CompilerParams(dimension_semantics=semantics, vmem_limit_bytes=VMEM_LIMIT_BYTES)


def _rms(x, g):
    return x * lax.rsqrt(jnp.mean(x * x, axis=-1, keepdims=True) + EPS) * g


IN_TM = 512
IN_TN = 512


def _rope(y, cos, sin_signed):
    lane = lax.broadcasted_iota(jnp.int32, y.shape, 1)
    partner = jnp.where((lane % 64) < 32, pltpu.roll(y, 96, 1), pltpu.roll(y, 32, 1))
    return y * cos + partner * sin_signed


def _in_proj_kernel(x_ref, g_ref, w_ref, cos_ref, sin_ref, gq_ref, gk_ref, o_ref, u_ref):
    j = pl.program_id(1)

    @pl.when(j == 0)
    def _():
        u_ref[...] = _rms(x_ref[...], g_ref[...]).astype(BF16)

    acc = jnp.dot(u_ref[...], w_ref[...], preferred_element_type=F32)

    def norm_rope(a, gain):
        return _rope(_rms(a, gain), cos_ref[...], sin_ref[...])

    heads = [acc[:, h * HEAD_DIM:(h + 1) * HEAD_DIM] for h in range(IN_TN // HEAD_DIM)]

    def store(vals):
        for h, v in enumerate(vals):
            o_ref[:, h * HEAD_DIM:(h + 1) * HEAD_DIM] = v.astype(BF16)

    @pl.when(j < 2)
    def _():
        store([norm_rope(a, gq_ref[...]) * SM_SCALE for a in heads])

    @pl.when(j == 2)
    def _():
        store([norm_rope(heads[0], gk_ref[...]), norm_rope(heads[1], gk_ref[...]), heads[2], heads[3]])

    @pl.when((j == 3) | (j == 4))
    def _():
        store([a * SM_SCALE for a in heads])

    @pl.when(j == 5)
    def _():
        store(heads)


def _in_proj(x2, g, w, cos, sin_signed, gq, gk, seq):
    m, d = x2.shape
    n = w.shape[1]
    pos_tiles = seq // IN_TM
    return pl.pallas_call(
        _in_proj_kernel,
        grid=(m // IN_TM, n // IN_TN),
        in_specs=[
            pl.BlockSpec((IN_TM, d), lambda i, j: (i, 0)),
            pl.BlockSpec((1, d), lambda i, j: (0, 0)),
            pl.BlockSpec((d, IN_TN), lambda i, j: (0, j)),
            pl.BlockSpec((IN_TM, HEAD_DIM), lambda i, j: (i % pos_tiles, 0)),
            pl.BlockSpec((IN_TM, HEAD_DIM), lambda i, j: (i % pos_tiles, 0)),
            pl.BlockSpec((1, HEAD_DIM), lambda i, j: (0, 0)),
            pl.BlockSpec((1, HEAD_DIM), lambda i, j: (0, 0)),
        ],
        out_specs=pl.BlockSpec((IN_TM, IN_TN), lambda i, j: (i, j)),
        out_shape=jax.ShapeDtypeStruct((m, n), BF16),
        scratch_shapes=[pltpu.VMEM((IN_TM, d), BF16)],
        compiler_params=_params("parallel", "arbitrary"),
        name="in_proj",
    )(x2, g, w, cos, sin_signed, gq, gk)


GA_TQ = 256


def _global_attn_kernel(q_ref, k_ref, v_ref, o_ref):
    k = k_ref[...]
    v = v_ref[...]
    for g in range(GROUP):
        q = q_ref[:, g * HEAD_DIM:(g + 1) * HEAD_DIM]
        s = lax.dot_general(q, k, (((1,), (1,)), ((), ())), preferred_element_type=F32)
        m = jnp.max(s, axis=-1, keepdims=True)
        p = jnp.exp(s - m)
        l = jnp.sum(p, axis=-1, keepdims=True)
        o = jnp.dot(p.astype(BF16), v, preferred_element_type=F32)
        o_ref[:, g * HEAD_DIM:(g + 1) * HEAD_DIM] = (o / l).astype(BF16)


def _global_attn(proj, batch, seq):
    q_tiles = seq // GA_TQ
    k_col = N_HEADS_A
    v_col = N_HEADS_A + N_KV_A
    return pl.pallas_call(
        _global_attn_kernel,
        grid=(batch, N_KV_A, q_tiles),
        in_specs=[
            pl.BlockSpec((GA_TQ, GROUP * HEAD_DIM), lambda b, k, i: (b * q_tiles + i, k)),
            pl.BlockSpec((seq, HEAD_DIM), lambda b, k, i: (b, k_col + k)),
            pl.BlockSpec((seq, HEAD_DIM), lambda b, k, i: (b, v_col + k)),
        ],
        out_specs=pl.BlockSpec((GA_TQ, GROUP * HEAD_DIM), lambda b, k, i: (b * q_tiles + i, k)),
        out_shape=jax.ShapeDtypeStruct((batch * seq, N_HEADS_A * HEAD_DIM), BF16),
        compiler_params=_params("parallel", "parallel", "arbitrary"),
        name="global_attn",
    )(proj, proj, proj)


def _window_attn_kernel(sink_ref, q_ref, kp_ref, kc_ref, kn_ref, vp_ref, vc_ref, vn_ref, bias_ref, o_ref):
    kvh = pl.program_id(1)
    n = pl.program_id(2)
    nb = pl.num_programs(2)
    kband = jnp.concatenate([kp_ref[...], kc_ref[...], kn_ref[...]], axis=0)
    vband = jnp.concatenate([vp_ref[...], vc_ref[...], vn_ref[...]], axis=0)
    col = lax.broadcasted_iota(jnp.int32, (BLOCK_Q, 3 * BLOCK_Q), 1)
    in_range = ((col >= BLOCK_Q) | (n > 0)) & ((col < 2 * BLOCK_Q) | (n < nb - 1))
    for g in range(GROUP):
        q = q_ref[:, g * HEAD_DIM:(g + 1) * HEAD_DIM]
        s = lax.dot_general(q, kband, (((1,), (1,)), ((), ())), preferred_element_type=F32)
        s = jnp.where(in_range, s + bias_ref[g], NEG_INF)
        sink = sink_ref[kvh * GROUP + g]
        m = jnp.maximum(jnp.max(s, axis=-1, keepdims=True), sink)
        p = jnp.exp(s - m)
        l = jnp.sum(p, axis=-1, keepdims=True) + jnp.exp(sink - m)
        o = jnp.dot(p.astype(BF16), vband, preferred_element_type=F32)
        o_ref[:, g * HEAD_DIM:(g + 1) * HEAD_DIM] = (o / l).astype(BF16)


def _window_attn(proj, bias, sink, batch, seq):
    nb = seq // BLOCK_Q
    q_col = (N_HEADS_A + 2 * N_KV_A) // GROUP
    k_col = N_HEADS_A + 2 * N_KV_A + N_HEADS_B
    v_col = k_col + N_KV_B

    def band(col0, shift):
        def index(b, k, n, sink_ref):
            return (b * nb + jnp.clip(n + shift, 0, nb - 1), col0 + k)
        return pl.BlockSpec((BLOCK_Q, HEAD_DIM), index)

    grid_spec = pltpu.PrefetchScalarGridSpec(
        num_scalar_prefetch=1,
        grid=(batch, N_KV_B, nb),
        in_specs=[
            pl.BlockSpec((BLOCK_Q, GROUP * HEAD_DIM), lambda b, k, n, s: (b * nb + n, q_col + k)),
            band(k_col, -1), band(k_col, 0), band(k_col, 1),
            band(v_col, -1), band(v_col, 0), band(v_col, 1),
            pl.BlockSpec((GROUP, BLOCK_Q, 3 * BLOCK_Q), lambda b, k, n, s: (k, 0, 0)),
        ],
        out_specs=pl.BlockSpec((BLOCK_Q, GROUP * HEAD_DIM), lambda b, k, n, s: (b * nb + n, k)),
    )
    return pl.pallas_call(
        _window_attn_kernel,
        grid_spec=grid_spec,
        out_shape=jax.ShapeDtypeStruct((batch * seq, N_HEADS_B * HEAD_DIM), BF16),
        compiler_params=_params("parallel", "parallel", "arbitrary"),
        name="window_attn",
    )(sink, proj, proj, proj, proj, proj, proj, proj, bias)


OUT_TM = 256


def _out_proj_kernel(oa_ref, ob_ref, w_ref, x_ref, g_ref, h_ref, m_ref):
    ka = oa_ref.shape[1]
    h = x_ref[...]
    h = h + jnp.dot(oa_ref[...], w_ref[:ka, :], preferred_element_type=F32)
    h = h + jnp.dot(ob_ref[...], w_ref[ka:, :], preferred_element_type=F32)
    h_ref[...] = h
    m_ref[...] = _rms(h, g_ref[...]).astype(BF16)


def _out_proj(oa, ob, w, x2, g):
    m, d = x2.shape
    ka, kb = oa.shape[1], ob.shape[1]
    return pl.pallas_call(
        _out_proj_kernel,
        grid=(m // OUT_TM,),
        in_specs=[
            pl.BlockSpec((OUT_TM, ka), lambda i: (i, 0)),
            pl.BlockSpec((OUT_TM, kb), lambda i: (i, 0)),
            pl.BlockSpec((ka + kb, d), lambda i: (0, 0)),
            pl.BlockSpec((OUT_TM, d), lambda i: (i, 0)),
            pl.BlockSpec((1, d), lambda i: (0, 0)),
        ],
        out_specs=[pl.BlockSpec((OUT_TM, d), lambda i: (i, 0)), pl.BlockSpec((OUT_TM, d), lambda i: (i, 0))],
        out_shape=[jax.ShapeDtypeStruct((m, d), F32), jax.ShapeDtypeStruct((m, d), BF16)],
        compiler_params=_params("parallel"),
        name="out_proj",
    )(oa, ob, w, x2, g)


MLP_TM = 512
MLP_TF = 512


def _mlp_kernel(m_ref, h_ref, wu_ref, wd_ref, g_ref, h2_ref, gn_ref):
    f = pl.program_id(1)
    a = jnp.dot(m_ref[...], wu_ref[...], preferred_element_type=F32)
    act = jnp.square(jnp.maximum(a, 0.0)).astype(BF16)
    contrib = jnp.dot(act, wd_ref[...], preferred_element_type=F32)

    @pl.when(f == 0)
    def _():
        h2_ref[...] = h_ref[...] + contrib

    @pl.when(f > 0)
    def _():
        h2_ref[...] += contrib

    @pl.when(f == pl.num_programs(1) - 1)
    def _():
        gn_ref[...] = _rms(h2_ref[...], g_ref[...]).astype(BF16)


def _mlp(mn, h, wu, wd, g):
    m, d = h.shape
    dff = wu.shape[1]
    return pl.pallas_call(
        _mlp_kernel,
        grid=(m // MLP_TM, dff // MLP_TF),
        in_specs=[
            pl.BlockSpec((MLP_TM, d), lambda i, f: (i, 0)),
            pl.BlockSpec((MLP_TM, d), lambda i, f: (i, 0)),
            pl.BlockSpec((d, MLP_TF), lambda i, f: (0, f)),
            pl.BlockSpec((MLP_TF, d), lambda i, f: (f, 0)),
            pl.BlockSpec((1, d), lambda i, f: (0, 0)),
        ],
        out_specs=[pl.BlockSpec((MLP_TM, d), lambda i, f: (i, 0)), pl.BlockSpec((MLP_TM, d), lambda i, f: (i, 0))],
        out_shape=[jax.ShapeDtypeStruct((m, d), F32), jax.ShapeDtypeStruct((m, d), BF16)],
        compiler_params=_params("parallel", "arbitrary"),
        name="mlp",
    )(mn, h, wu, wd, g)


GATE_TM = 256


def _gate_final_kernel(gn_ref, wg_ref, p_ref, wp_ref, h_ref, gp_ref, gf_ref, o_ref):
    gate = jax.nn.sigmoid(jnp.dot(gn_ref[...], wg_ref[...], preferred_element_type=F32))
    e = _rms(jnp.dot(p_ref[...].astype(BF16), wp_ref[...], preferred_element_type=F32), gp_ref[...])
    o_ref[...] = _rms(h_ref[...] + gate * e, gf_ref[...])


def _gate_final(gn, wg, p2, wp, h, gp, gf):
    m, d = h.shape
    dp = p2.shape[1]
    return pl.pallas_call(
        _gate_final_kernel,
        grid=(m // GATE_TM,),
        in_specs=[
            pl.BlockSpec((GATE_TM, d), lambda i: (i, 0)),
            pl.BlockSpec((d, d), lambda i: (0, 0)),
            pl.BlockSpec((GATE_TM, dp), lambda i: (i, 0)),
            pl.BlockSpec((dp, d), lambda i: (0, 0)),
            pl.BlockSpec((GATE_TM, d), lambda i: (i, 0)),
            pl.BlockSpec((1, d), lambda i: (0, 0)),
            pl.BlockSpec((1, d), lambda i: (0, 0)),
        ],
        out_specs=pl.BlockSpec((GATE_TM, d), lambda i: (i, 0)),
        out_shape=jax.ShapeDtypeStruct((m, d), F32),
        compiler_params=_params("parallel"),
        name="gate_final",
    )(gn, wg, p2, wp, h, gp, gf)


def _rope_tables(seq):
    rows = seq // GRID_W
    row = jnp.repeat(jnp.arange(rows, dtype=jnp.int32), GRID_W)
    col = jnp.tile(jnp.arange(GRID_W, dtype=jnp.int32), rows)
    half = HEAD_DIM // 2
    inv_freq = ROPE_THETA ** (-jnp.arange(0, half, 2, dtype=F32) / half)
    ang_r = row.astype(F32)[:, None] * inv_freq
    ang_c = col.astype(F32)[:, None] * inv_freq
    cr, sr, cc, sc = jnp.cos(ang_r), jnp.sin(ang_r), jnp.cos(ang_c), jnp.sin(ang_c)
    return jnp.concatenate([cr, cr, cc, cc], axis=-1), jnp.concatenate([-sr, sr, -sc, sc], axis=-1)


def _t5_bucket(rel):
    nb = N_BUCKETS // 2
    ret = jnp.where(rel > 0, nb, 0)
    n = jnp.abs(rel)
    max_exact = nb // 2
    nf = jnp.maximum(n, 1).astype(F32)
    large = max_exact + (jnp.log(nf / max_exact) / math.log(MAX_DISTANCE / max_exact)
                         * (nb - max_exact)).astype(jnp.int32)
    large = jnp.minimum(large, nb - 1)
    return ret + jnp.where(n < max_exact, n, large)


def _window_bias(rel_bias_table):
    r = jnp.arange(BLOCK_Q, dtype=jnp.int32)
    j = jnp.arange(3 * BLOCK_Q, dtype=jnp.int32)
    rel = (j[None, :] - BLOCK_Q) - r[:, None]
    bias = rel_bias_table[_t5_bucket(rel)].astype(F32).transpose(2, 0, 1)
    return jnp.where((jnp.abs(rel) <= WINDOW)[None], bias, NEG_INF)


def kernel(x, p, attn_norm_g, w_in, q_norm_g, k_norm_g, sink_logits, w_out, mlp_norm_g, w_up, w_down, ple_w,
           ple_norm_g, gate_norm_g, w_gate, rel_bias_table, final_norm_g):
    batch, seq, d = x.shape
    depth = w_in.shape[0]
    cos, sin_signed = _rope_tables(seq)
    h = x.reshape(batch * seq, d)
    for i in range(depth):
        row = lambda v: v[i].reshape(1, -1).astype(F32)
        proj = _in_proj(h, row(attn_norm_g), w_in[i].astype(BF16), cos, sin_signed, row(q_norm_g), row(k_norm_g), seq)
        oa = _global_attn(proj, batch, seq)
        ob = _window_attn(proj, _window_bias(rel_bias_table), sink_logits[i].astype(F32), batch, seq)
        h1, mn = _out_proj(oa, ob, w_out[i].astype(BF16), h, row(mlp_norm_g))
        h2, gn = _mlp(mn, h1, w_up[i].astype(BF16), w_down[i].astype(BF16), row(gate_norm_g))
        h = _gate_final(gn, w_gate[i].astype(BF16), p[i].reshape(batch * seq, -1), ple_w[i].astype(BF16), h2,
                        row(ple_norm_g), final_norm_g.reshape(1, -1).astype(F32))
    assert depth == 1
    return h.reshape(batch, seq, d)
```

```python
import functools
import math

import jax
import jax.numpy as jnp
import numpy as np
from jax import lax
from jax.experimental import pallas as pl
from jax.experimental.pallas import tpu as pltpu

HEAD_DIM = 128
N_HEADS_A = 8
N_KV_A = 2
N_HEADS_B = 8
N_KV_B = 2
GROUP = 4
GRID_W = 64
BLOCK_Q = 128
WINDOW = 128
N_BUCKETS = 32
MAX_DISTANCE = 128
ROPE_THETA = 10000.0
EPS = 1e-6
NEG_INF = -1e30
SM_SCALE = HEAD_DIM ** -0.5

V7X_VMEM_BYTES = 64 * 1024 * 1024
VMEM_LIMIT_BYTES = 56 * 1024 * 1024

BF16 = jnp.bfloat16
F32 = jnp.float32


def _params(*semantics):
    return pltpu.CompilerParams(dimension_semantics=semantics, vmem_limit_bytes=VMEM_LIMIT_BYTES)


def _rms(x, g):
    return x * lax.rsqrt(jnp.mean(x * x, axis=-1, keepdims=True) + EPS) * g


IN_TM = 512
IN_TN = 512


def _rope(y, cos, sin_signed):
    lane = lax.broadcasted_iota(jnp.int32, y.shape, 1)
    partner = jnp.where((lane % 64) < 32, pltpu.roll(y, 96, 1), pltpu.roll(y, 32, 1))
    return y * cos + partner * sin_signed


def _in_proj_kernel(x_ref, g_ref, w_ref, cos_ref, sin_ref, gq_ref, gk_ref, o_ref, u_ref):
    j = pl.program_id(1)

    @pl.when(j == 0)
    def _():
        u_ref[...] = _rms(x_ref[...], g_ref[...]).astype(BF16)

    acc = jnp.dot(u_ref[...], w_ref[...], preferred_element_type=F32)

    def norm_rope(a, gain):
        return _rope(_rms(a, gain), cos_ref[...], sin_ref[...])

    heads = [acc[:, h * HEAD_DIM:(h + 1) * HEAD_DIM] for h in range(IN_TN // HEAD_DIM)]

    def store(vals):
        for h, v in enumerate(vals):
            o_ref[:, h * HEAD_DIM:(h + 1) * HEAD_DIM] = v.astype(BF16)

    @pl.when(j < 2)
    def _():
        store([norm_rope(a, gq_ref[...]) * SM_SCALE for a in heads])

    @pl.when(j == 2)
    def _():
        store([norm_rope(heads[0], gk_ref[...]), norm_rope(heads[1], gk_ref[...]), heads[2], heads[3]])

    @pl.when((j == 3) | (j == 4))
    def _():
        store([a * SM_SCALE for a in heads])

    @pl.when(j == 5)
    def _():
        store(heads)


def _in_proj(x2, g, w, cos, sin_signed, gq, gk, seq):
    m, d = x2.shape
    n = w.shape[1]
    pos_tiles = seq // IN_TM
    return pl.pallas_call(
        _in_proj_kernel,
        grid=(m // IN_TM, n // IN_TN),
        in_specs=[
            pl.BlockSpec((IN_TM, d), lambda i, j: (i, 0)),
            pl.BlockSpec((1, d), lambda i, j: (0, 0)),
            pl.BlockSpec((d, IN_TN), lambda i, j: (0, j)),
            pl.BlockSpec((IN_TM, HEAD_DIM), lambda i, j: (i % pos_tiles, 0)),
            pl.BlockSpec((IN_TM, HEAD_DIM), lambda i, j: (i % pos_tiles, 0)),
            pl.BlockSpec((1, HEAD_DIM), lambda i, j: (0, 0)),
            pl.BlockSpec((1, HEAD_DIM), lambda i, j: (0, 0)),
        ],
        out_specs=pl.BlockSpec((IN_TM, IN_TN), lambda i, j: (i, j)),
        out_shape=jax.ShapeDtypeStruct((m, n), BF16),
        scratch_shapes=[pltpu.VMEM((IN_TM, d), BF16)],
        compiler_params=_params("parallel", "arbitrary"),
        name="in_proj",
    )(x2, g, w, cos, sin_signed, gq, gk)


GA_TQ = 256


def _global_attn_kernel(q_ref, k_ref, v_ref, o_ref):
    k = k_ref[...]
    v = v_ref[...]
    for g in range(GROUP):
        q = q_ref[:, g * HEAD_DIM:(g + 1) * HEAD_DIM]
        s = lax.dot_general(q, k, (((1,), (1,)), ((), ())), preferred_element_type=F32)
        m = jnp.max(s, axis=-1, keepdims=True)
        p = jnp.exp(s - m)
        l = jnp.sum(p, axis=-1, keepdims=True)
        o = jnp.dot(p.astype(BF16), v, preferred_element_type=F32)
        o_ref[:, g * HEAD_DIM:(g + 1) * HEAD_DIM] = (o / l).astype(BF16)


def _global_attn(proj, batch, seq):
    q_tiles = seq // GA_TQ
    k_col = N_HEADS_A
    v_col = N_HEADS_A + N_KV_A
    return pl.pallas_call(
        _global_attn_kernel,
        grid=(batch, N_KV_A, q_tiles),
        in_specs=[
            pl.BlockSpec((GA_TQ, GROUP * HEAD_DIM), lambda b, k, i: (b * q_tiles + i, k)),
            pl.BlockSpec((seq, HEAD_DIM), lambda b, k, i: (b, k_col + k)),
            pl.BlockSpec((seq, HEAD_DIM), lambda b, k, i: (b, v_col + k)),
        ],
        out_specs=pl.BlockSpec((GA_TQ, GROUP * HEAD_DIM), lambda b, k, i: (b * q_tiles + i, k)),
        out_shape=jax.ShapeDtypeStruct((batch * seq, N_HEADS_A * HEAD_DIM), BF16),
        compiler_params=_params("parallel", "parallel", "arbitrary"),
        name="global_attn",
    )(proj, proj, proj)


def _window_attn_kernel(sink_ref, table_ref, q_ref, kp_ref, kc_ref, kn_ref, vp_ref, vc_ref, vn_ref, bucket_ref,
                        o_ref, bias_ref):
    kvh = pl.program_id(1)
    n = pl.program_id(2)
    nb = pl.num_programs(2)

    @pl.when(n == 0)
    def _():
        bucket = bucket_ref[...]
        for g in range(GROUP):
            head = kvh * GROUP + g
            tile = jnp.full(bucket.shape, NEG_INF, F32)
            for b in range(N_BUCKETS):
                tile = jnp.where(bucket == b, table_ref[b * N_HEADS_B + head], tile)
            bias_ref[g] = tile

    kband = jnp.concatenate([kp_ref[...], kc_ref[...], kn_ref[...]], axis=0)
    vband = jnp.concatenate([vp_ref[...], vc_ref[...], vn_ref[...]], axis=0)
    col = lax.broadcasted_iota(jnp.int32, (BLOCK_Q, 3 * BLOCK_Q), 1)
    in_range = ((col >= BLOCK_Q) | (n > 0)) & ((col < 2 * BLOCK_Q) | (n < nb - 1))
    for g in range(GROUP):
        q = q_ref[:, g * HEAD_DIM:(g + 1) * HEAD_DIM]
        s = lax.dot_general(q, kband, (((1,), (1,)), ((), ())), preferred_element_type=F32)
        s = jnp.where(in_range, s + bias_ref[g], NEG_INF)
        sink = sink_ref[kvh * GROUP + g]
        m = jnp.maximum(jnp.max(s, axis=-1, keepdims=True), sink)
        p = jnp.exp(s - m)
        l = jnp.sum(p, axis=-1, keepdims=True) + jnp.exp(sink - m)
        o = jnp.dot(p.astype(BF16), vband, preferred_element_type=F32)
        o_ref[:, g * HEAD_DIM:(g + 1) * HEAD_DIM] = (o / l).astype(BF16)


def _window_attn(proj, bucket, table, sink, batch, seq):
    nb = seq // BLOCK_Q
    q_col = (N_HEADS_A + 2 * N_KV_A) // GROUP
    k_col = N_HEADS_A + 2 * N_KV_A + N_HEADS_B
    v_col = k_col + N_KV_B

    def band(col0, shift):
        def index(b, k, n, sink_ref, table_ref):
            return (b * nb + jnp.clip(n + shift, 0, nb - 1), col0 + k)
        return pl.BlockSpec((BLOCK_Q, HEAD_DIM), index)

    grid_spec = pltpu.PrefetchScalarGridSpec(
        num_scalar_prefetch=2,
        grid=(batch, N_KV_B, nb),
        in_specs=[
            pl.BlockSpec((BLOCK_Q, GROUP * HEAD_DIM), lambda b, k, n, s, t: (b * nb + n, q_col + k)),
            band(k_col, -1), band(k_col, 0), band(k_col, 1),
            band(v_col, -1), band(v_col, 0), band(v_col, 1),
            pl.BlockSpec((BLOCK_Q, 3 * BLOCK_Q), lambda b, k, n, s, t: (0, 0)),
        ],
        out_specs=pl.BlockSpec((BLOCK_Q, GROUP * HEAD_DIM), lambda b, k, n, s, t: (b * nb + n, k)),
        scratch_shapes=[pltpu.VMEM((GROUP, BLOCK_Q, 3 * BLOCK_Q), F32)],
    )
    return pl.pallas_call(
        _window_attn_kernel,
        grid_spec=grid_spec,
        out_shape=jax.ShapeDtypeStruct((batch * seq, N_HEADS_B * HEAD_DIM), BF16),
        compiler_params=_params("arbitrary", "arbitrary", "arbitrary"),
        name="window_attn",
    )(sink, table, proj, proj, proj, proj, proj, proj, proj, bucket)


OUT_TM = 256


def _out_proj_kernel(oa_ref, ob_ref, w_ref, x_ref, g_ref, h_ref, m_ref):
    ka = oa_ref.shape[1]
    h = x_ref[...]
    h = h + jnp.dot(oa_ref[...], w_ref[:ka, :], preferred_element_type=F32)
    h = h + jnp.dot(ob_ref[...], w_ref[ka:, :], preferred_element_type=F32)
    h_ref[...] = h
    m_ref[...] = _rms(h, g_ref[...]).astype(BF16)


def _out_proj(oa, ob, w, x2, g):
    m, d = x2.shape
    ka, kb = oa.shape[1], ob.shape[1]
    return pl.pallas_call(
        _out_proj_kernel,
        grid=(m // OUT_TM,),
        in_specs=[
            pl.BlockSpec((OUT_TM, ka), lambda i: (i, 0)),
            pl.BlockSpec((OUT_TM, kb), lambda i: (i, 0)),
            pl.BlockSpec((ka + kb, d), lambda i: (0, 0)),
            pl.BlockSpec((OUT_TM, d), lambda i: (i, 0)),
            pl.BlockSpec((1, d), lambda i: (0, 0)),
        ],
        out_specs=[pl.BlockSpec((OUT_TM, d), lambda i: (i, 0)), pl.BlockSpec((OUT_TM, d), lambda i: (i, 0))],
        out_shape=[jax.ShapeDtypeStruct((m, d), F32), jax.ShapeDtypeStruct((m, d), BF16)],
        compiler_params=_params("parallel"),
        name="out_proj",
    )(oa, ob, w, x2, g)


MLP_TM = 512
MLP_TF = 1024


def _mlp_kernel(m_ref, h_ref, wu_ref, wd_ref, g_ref, h2_ref, gn_ref):
    f = pl.program_id(1)

    @pl.when(f == 0)
    def _():
        h2_ref[...] = h_ref[...]

    a = jnp.dot(m_ref[...], wu_ref[...], preferred_element_type=F32)
    act = jnp.square(jnp.maximum(a, 0.0)).astype(BF16)
    h2_ref[...] += jnp.dot(act, wd_ref[...], preferred_element_type=F32)

    @pl.when(f == pl.num_programs(1) - 1)
    def _():
        gn_ref[...] = _rms(h2_ref[...], g_ref[...]).astype(BF16)


def _mlp(mn, h, wu, wd, g):
    m, d = h.shape
    dff = wu.shape[1]
    return pl.pallas_call(
        _mlp_kernel,
        grid=(m // MLP_TM, dff // MLP_TF),
        in_specs=[
            pl.BlockSpec((MLP_TM, d), lambda i, f: (i, 0)),
            pl.BlockSpec((MLP_TM, d), lambda i, f: (i, 0)),
            pl.BlockSpec((d, MLP_TF), lambda i, f: (0, f)),
            pl.BlockSpec((MLP_TF, d), lambda i, f: (f, 0)),
            pl.BlockSpec((1, d), lambda i, f: (0, 0)),
        ],
        out_specs=[pl.BlockSpec((MLP_TM, d), lambda i, f: (i, 0)), pl.BlockSpec((MLP_TM, d), lambda i, f: (i, 0))],
        out_shape=[jax.ShapeDtypeStruct((m, d), F32), jax.ShapeDtypeStruct((m, d), BF16)],
        compiler_params=_params("parallel", "arbitrary"),
        name="mlp",
    )(mn, h, wu, wd, g)


GATE_TM = 256


def _gate_final_kernel(gn_ref, wg_ref, p_ref, wp_ref, h_ref, gp_ref, gf_ref, o_ref):
    gate = jax.nn.sigmoid(jnp.dot(gn_ref[...], wg_ref[...], preferred_element_type=F32))
    e = _rms(jnp.dot(p_ref[...].astype(BF16), wp_ref[...], preferred_element_type=F32), gp_ref[...])
    o_ref[...] = _rms(h_ref[...] + gate * e, gf_ref[...])


def _gate_final(gn, wg, p2, wp, h, gp, gf):
    m, d = h.shape
    dp = p2.shape[1]
    return pl.pallas_call(
        _gate_final_kernel,
        grid=(m // GATE_TM,),
        in_specs=[
            pl.BlockSpec((GATE_TM, d), lambda i: (i, 0)),
            pl.BlockSpec((d, d), lambda i: (0, 0)),
            pl.BlockSpec((GATE_TM, dp), lambda i: (i, 0)),
            pl.BlockSpec((dp, d), lambda i: (0, 0)),
            pl.BlockSpec((GATE_TM, d), lambda i: (i, 0)),
            pl.BlockSpec((1, d), lambda i: (0, 0)),
            pl.BlockSpec((1, d), lambda i: (0, 0)),
        ],
        out_specs=pl.BlockSpec((GATE_TM, d), lambda i: (i, 0)),
        out_shape=jax.ShapeDtypeStruct((m, d), F32),
        compiler_params=_params("parallel"),
        name="gate_final",
    )(gn, wg, p2, wp, h, gp, gf)


def _rope_tables(seq):
    rows = seq // GRID_W
    row = np.repeat(np.arange(rows, dtype=np.float32), GRID_W)
    col = np.tile(np.arange(GRID_W, dtype=np.float32), rows)
    half = HEAD_DIM // 2
    inv_freq = np.float32(ROPE_THETA) ** (-np.arange(0, half, 2, dtype=np.float32) / np.float32(half))
    ang_r = row[:, None] * inv_freq
    ang_c = col[:, None] * inv_freq
    cr, sr, cc, sc = np.cos(ang_r), np.sin(ang_r), np.cos(ang_c), np.sin(ang_c)
    cos = np.concatenate([cr, cr, cc, cc], axis=-1).astype(np.float32)
    sin_signed = np.concatenate([-sr, sr, -sc, sc], axis=-1).astype(np.float32)
    return jnp.asarray(cos), jnp.asarray(sin_signed)


def _t5_bucket(rel):
    nb = N_BUCKETS // 2
    ret = jnp.where(rel > 0, nb, 0)
    n = jnp.abs(rel)
    max_exact = nb // 2
    nf = jnp.maximum(n, 1).astype(F32)
    large = max_exact + (jnp.log(nf / max_exact) / math.log(MAX_DISTANCE / max_exact)
                         * (nb - max_exact)).astype(jnp.int32)
    large = jnp.minimum(large, nb - 1)
    return ret + jnp.where(n < max_exact, n, large)


def _window_buckets():
    r = jnp.arange(BLOCK_Q, dtype=jnp.int32)
    j = jnp.arange(3 * BLOCK_Q, dtype=jnp.int32)
    rel = (j[None, :] - BLOCK_Q) - r[:, None]
    return jnp.where(jnp.abs(rel) <= WINDOW, _t5_bucket(rel), -1).astype(jnp.int32)


def kernel(x, p, attn_norm_g, w_in, q_norm_g, k_norm_g, sink_logits, w_out, mlp_norm_g, w_up, w_down, ple_w,
           ple_norm_g, gate_norm_g, w_gate, rel_bias_table, final_norm_g):
    batch, seq, d = x.shape
    assert w_in.shape[0] == 1, "gate_final fuses the final RMSNorm, which is only valid for a single layer"
    row = lambda v: v.reshape(1, -1).astype(F32)
    cos, sin_signed = _rope_tables(seq)
    h = x.reshape(batch * seq, d)
    proj = _in_proj(h, row(attn_norm_g), w_in[0].astype(BF16), cos, sin_signed, row(q_norm_g), row(k_norm_g), seq)
    oa = _global_attn(proj, batch, seq)
    ob = _window_attn(proj, _window_buckets(), rel_bias_table.reshape(-1).astype(F32),
                      sink_logits.reshape(-1).astype(F32), batch, seq)
    h1, mn = _out_proj(oa, ob, w_out[0].astype(BF16), h, row(mlp_norm_g))
    h2, gn = _mlp(mn, h1, w_up[0].astype(BF16), w_down[0].astype(BF16), row(gate_norm_g))
    out = _gate_final(gn, w_gate[0].astype(BF16), p.reshape(batch * seq, -1), ple_w[0].astype(BF16), h2,
                      row(ple_norm_g), row(final_norm_g))
    return out.reshape(batch, seq, d)
```

```python
import functools
import math

import jax
import jax.numpy as jnp
import numpy as np
from jax import lax
from jax.experimental import pallas as pl
from jax.experimental.pallas import tpu as pltpu

HEAD_DIM = 128
N_HEADS_A = 8
N_KV_A = 2
N_HEADS_B = 8
N_KV_B = 2
GROUP = 4
GRID_W = 64
BLOCK_Q = 128
WINDOW = 128
N_BUCKETS = 32
MAX_DISTANCE = 128
ROPE_THETA = 10000.0
EPS = 1e-6
NEG_INF = -1e30
SM_SCALE = HEAD_DIM ** -0.5
LOG2_E = math.log2(math.e)

V7X_VMEM_BYTES = 64 * 1024 * 1024
VMEM_LIMIT_BYTES = 56 * 1024 * 1024

SUB_ROWS = 256

BF16 = jnp.bfloat16
F32 = jnp.float32


def _params(*semantics):
    return pltpu.CompilerParams(dimension_semantics=semantics, vmem_limit_bytes=VMEM_LIMIT_BYTES)


def _rms(x, g):
    return x * lax.rsqrt(jnp.mean(x * x, axis=-1, keepdims=True) + EPS) * g


IN_TM = 512
IN_TN = 512

_PLAIN, _Q_A, _K_A, _Q_B = range(4)
_HEAD_KINDS = ([_Q_A] * N_HEADS_A + [_K_A] * N_KV_A + [_PLAIN] * N_KV_A
               + [_Q_B] * N_HEADS_B + [_PLAIN] * (2 * N_KV_B))


def _rope(y, cos, sin_signed):
    lane = lax.broadcasted_iota(jnp.int32, y.shape, 1)
    partner = jnp.where((lane % 64) < 32, pltpu.roll(y, 96, 1), pltpu.roll(y, 32, 1))
    return y * cos + partner * sin_signed


def _in_proj_kernel(x_ref, g_ref, w_ref, cos_ref, sin_ref, gq_ref, gk_ref, o_ref):
    heads_per_dot = IN_TN // HEAD_DIM
    for r in range(IN_TM // SUB_ROWS):
        rows = pl.ds(r * SUB_ROWS, SUB_ROWS)
        u = _rms(x_ref[rows, :], g_ref[...]).astype(BF16)
        cos, sin_signed = cos_ref[rows, :], sin_ref[rows, :]
        for c in range(w_ref.shape[1] // IN_TN):
            acc = jnp.dot(u, w_ref[:, c * IN_TN:(c + 1) * IN_TN], preferred_element_type=F32)
            for hh in range(heads_per_dot):
                head = c * heads_per_dot + hh
                a = acc[:, hh * HEAD_DIM:(hh + 1) * HEAD_DIM]
                kind = _HEAD_KINDS[head]
                if kind == _Q_A:
                    a = _rope(_rms(a, gq_ref[...]), cos, sin_signed) * (SM_SCALE * LOG2_E)
                elif kind == _K_A:
                    a = _rope(_rms(a, gk_ref[...]), cos, sin_signed)
                elif kind == _Q_B:
                    a = a * SM_SCALE
                o_ref[rows, head * HEAD_DIM:(head + 1) * HEAD_DIM] = a.astype(BF16)


def _in_proj(x2, g, w, cos, sin_signed, gq, gk, seq):
    m, d = x2.shape
    n = w.shape[1]
    assert n == len(_HEAD_KINDS) * HEAD_DIM
    pos_tiles = seq // IN_TM
    return pl.pallas_call(
        _in_proj_kernel,
        grid=(m // IN_TM,),
        in_specs=[
            pl.BlockSpec((IN_TM, d), lambda i: (i, 0)),
            pl.BlockSpec((1, d), lambda i: (0, 0)),
            pl.BlockSpec((d, n), lambda i: (0, 0)),
            pl.BlockSpec((IN_TM, HEAD_DIM), lambda i: (i % pos_tiles, 0)),
            pl.BlockSpec((IN_TM, HEAD_DIM), lambda i: (i % pos_tiles, 0)),
            pl.BlockSpec((1, HEAD_DIM), lambda i: (0, 0)),
            pl.BlockSpec((1, HEAD_DIM), lambda i: (0, 0)),
        ],
        out_specs=pl.BlockSpec((IN_TM, n), lambda i: (i, 0)),
        out_shape=jax.ShapeDtypeStruct((m, n), BF16),
        compiler_params=_params("parallel"),
        name="in_proj",
    )(x2, g, w, cos, sin_signed, gq, gk)


GA_ROWS = 256


def _global_attn_kernel(q_ref, k_ref, v_ref, o_ref, v1_ref, s_ref, m_ref):
    seq = q_ref.shape[0]
    n_blocks = seq // GA_ROWS

    v1_ref[:, :HEAD_DIM] = v_ref[...]
    v1_ref[:, HEAD_DIM:] = jnp.ones(v_ref.shape, BF16)

    def rows_of(r):
        return pl.ds(pl.multiple_of(r * GA_ROWS, GA_ROWS), GA_ROWS)

    def scores(r, g, slot):
        q = q_ref[rows_of(r), g * HEAD_DIM:(g + 1) * HEAD_DIM]
        s = lax.dot_general(q, k_ref[...], (((1,), (1,)), ((), ())), preferred_element_type=F32)
        s_ref[slot] = s
        m_ref[slot] = jnp.max(s, axis=-1, keepdims=True)

    def apply(r, g, slot):
        p = jnp.exp2(s_ref[slot] - m_ref[slot]).astype(BF16)
        o = jnp.dot(p, v1_ref[...], preferred_element_type=F32)
        o_ref[rows_of(r), g * HEAD_DIM:(g + 1) * HEAD_DIM] = (o[:, :HEAD_DIM] / o[:, HEAD_DIM:HEAD_DIM + 1]).astype(BF16)

    def row_block(r, last):
        for g in range(GROUP):
            slot = g % 2
            if g + 1 < GROUP:
                scores(r, g + 1, 1 - slot)
            elif not last:
                scores(r + 1, 0, 1 - slot)
            apply(r, g, slot)

    scores(0, 0, 0)

    def body(r, carry):
        row_block(r, last=False)
        return carry

    lax.fori_loop(0, n_blocks - 1, body, 0)
    row_block(n_blocks - 1, last=True)


def _global_attn(proj, batch, seq):
    assert GROUP % 2 == 0, "score slots alternate per head and must line up across row blocks"
    k_col = N_HEADS_A
    v_col = N_HEADS_A + N_KV_A
    return pl.pallas_call(
        _global_attn_kernel,
        grid=(batch, N_KV_A),
        in_specs=[
            pl.BlockSpec((seq, GROUP * HEAD_DIM), lambda b, k: (b, k)),
            pl.BlockSpec((seq, HEAD_DIM), lambda b, k: (b, k_col + k)),
            pl.BlockSpec((seq, HEAD_DIM), lambda b, k: (b, v_col + k)),
        ],
        out_specs=pl.BlockSpec((seq, GROUP * HEAD_DIM), lambda b, k: (b, k)),
        out_shape=jax.ShapeDtypeStruct((batch * seq, N_HEADS_A * HEAD_DIM), BF16),
        scratch_shapes=[
            pltpu.VMEM((seq, 2 * HEAD_DIM), BF16),
            pltpu.VMEM((2, GA_ROWS, seq), F32),
            pltpu.VMEM((2, GA_ROWS, 1), F32),
        ],
        compiler_params=_params("parallel", "parallel"),
        name="global_attn",
    )(proj, proj, proj)


def _window_attn_kernel(sink_ref, table_ref, q_ref, kp_ref, kc_ref, kn_ref, vp_ref, vc_ref, vn_ref, bucket_ref,
                        o_ref, bias_ref):
    kvh = pl.program_id(1)
    n = pl.program_id(2)
    nb = pl.num_programs(2)

    @pl.when(n == 0)
    def _():
        bucket = bucket_ref[...]
        for g in range(GROUP):
            head = kvh * GROUP + g
            tile = jnp.full(bucket.shape, NEG_INF, F32)
            for b in range(N_BUCKETS):
                tile = jnp.where(bucket == b, table_ref[b * N_HEADS_B + head], tile)
            bias_ref[g] = tile

    kband = jnp.concatenate([kp_ref[...], kc_ref[...], kn_ref[...]], axis=0)
    vband = jnp.concatenate([vp_ref[...], vc_ref[...], vn_ref[...]], axis=0)
    col = lax.broadcasted_iota(jnp.int32, (BLOCK_Q, 3 * BLOCK_Q), 1)
    in_range = ((col >= BLOCK_Q) | (n > 0)) & ((col < 2 * BLOCK_Q) | (n < nb - 1))
    for g in range(GROUP):
        q = q_ref[:, g * HEAD_DIM:(g + 1) * HEAD_DIM]
        s = lax.dot_general(q, kband, (((1,), (1,)), ((), ())), preferred_element_type=F32)
        s = jnp.where(in_range, s + bias_ref[g], NEG_INF)
        sink = sink_ref[kvh * GROUP + g]
        m = jnp.maximum(jnp.max(s, axis=-1, keepdims=True), sink)
        p = jnp.exp(s - m)
        l = jnp.sum(p, axis=-1, keepdims=True) + jnp.exp(sink - m)
        o = jnp.dot(p.astype(BF16), vband, preferred_element_type=F32)
        o_ref[:, g * HEAD_DIM:(g + 1) * HEAD_DIM] = (o / l).astype(BF16)


def _window_attn(proj, bucket, table, sink, batch, seq):
    nb = seq // BLOCK_Q
    q_col = (N_HEADS_A + 2 * N_KV_A) // GROUP
    k_col = N_HEADS_A + 2 * N_KV_A + N_HEADS_B
    v_col = k_col + N_KV_B

    def band(col0, shift):
        def index(b, k, n, sink_ref, table_ref):
            return (b * nb + jnp.clip(n + shift, 0, nb - 1), col0 + k)
        return pl.BlockSpec((BLOCK_Q, HEAD_DIM), index)

    grid_spec = pltpu.PrefetchScalarGridSpec(
        num_scalar_prefetch=2,
        grid=(batch, N_KV_B, nb),
        in_specs=[
            pl.BlockSpec((BLOCK_Q, GROUP * HEAD_DIM), lambda b, k, n, s, t: (b * nb + n, q_col + k)),
            band(k_col, -1), band(k_col, 0), band(k_col, 1),
            band(v_col, -1), band(v_col, 0), band(v_col, 1),
            pl.BlockSpec((BLOCK_Q, 3 * BLOCK_Q), lambda b, k, n, s, t: (0, 0)),
        ],
        out_specs=pl.BlockSpec((BLOCK_Q, GROUP * HEAD_DIM), lambda b, k, n, s, t: (b * nb + n, k)),
        scratch_shapes=[pltpu.VMEM((GROUP, BLOCK_Q, 3 * BLOCK_Q), F32)],
    )
    return pl.pallas_call(
        _window_attn_kernel,
        grid_spec=grid_spec,
        out_shape=jax.ShapeDtypeStruct((batch * seq, N_HEADS_B * HEAD_DIM), BF16),
        compiler_params=_params("arbitrary", "arbitrary", "arbitrary"),
        name="window_attn",
    )(sink, table, proj, proj, proj, proj, proj, proj, proj, bucket)


OUT_TM = 512


def _out_proj_kernel(oa_ref, ob_ref, w_ref, x_ref, g_ref, h_ref, m_ref):
    ka = oa_ref.shape[1]
    for r in range(OUT_TM // SUB_ROWS):
        rows = pl.ds(r * SUB_ROWS, SUB_ROWS)
        h = x_ref[rows, :]
        h = h + jnp.dot(oa_ref[rows, :], w_ref[:ka, :], preferred_element_type=F32)
        h = h + jnp.dot(ob_ref[rows, :], w_ref[ka:, :], preferred_element_type=F32)
        h_ref[rows, :] = h
        m_ref[rows, :] = _rms(h, g_ref[...]).astype(BF16)


def _out_proj(oa, ob, w, x2, g):
    m, d = x2.shape
    ka, kb = oa.shape[1], ob.shape[1]
    return pl.pallas_call(
        _out_proj_kernel,
        grid=(m // OUT_TM,),
        in_specs=[
            pl.BlockSpec((OUT_TM, ka), lambda i: (i, 0)),
            pl.BlockSpec((OUT_TM, kb), lambda i: (i, 0)),
            pl.BlockSpec((ka + kb, d), lambda i: (0, 0)),
            pl.BlockSpec((OUT_TM, d), lambda i: (i, 0)),
            pl.BlockSpec((1, d), lambda i: (0, 0)),
        ],
        out_specs=[pl.BlockSpec((OUT_TM, d), lambda i: (i, 0)), pl.BlockSpec((OUT_TM, d), lambda i: (i, 0))],
        out_shape=[jax.ShapeDtypeStruct((m, d), F32), jax.ShapeDtypeStruct((m, d), BF16)],
        compiler_params=_params("parallel"),
        name="out_proj",
    )(oa, ob, w, x2, g)


MLP_TM = 512
MLP_TF = 1024


def _mlp_kernel(m_ref, h_ref, wu_ref, wd_ref, g_ref, h2_ref, gn_ref):
    f = pl.program_id(1)

    @pl.when(f == 0)
    def _():
        h2_ref[...] = h_ref[...]

    a = jnp.dot(m_ref[...], wu_ref[...], preferred_element_type=F32)
    act = jnp.square(jnp.maximum(a, 0.0)).astype(BF16)
    h2_ref[...] += jnp.dot(act, wd_ref[...], preferred_element_type=F32)

    @pl.when(f == pl.num_programs(1) - 1)
    def _():
        gn_ref[...] = _rms(h2_ref[...], g_ref[...]).astype(BF16)


def _mlp(mn, h, wu, wd, g):
    m, d = h.shape
    dff = wu.shape[1]
    return pl.pallas_call(
        _mlp_kernel,
        grid=(m // MLP_TM, dff // MLP_TF),
        in_specs=[
            pl.BlockSpec((MLP_TM, d), lambda i, f: (i, 0)),
            pl.BlockSpec((MLP_TM, d), lambda i, f: (i, 0)),
            pl.BlockSpec((d, MLP_TF), lambda i, f: (0, f)),
            pl.BlockSpec((MLP_TF, d), lambda i, f: (f, 0)),
            pl.BlockSpec((1, d), lambda i, f: (0, 0)),
        ],
        out_specs=[pl.BlockSpec((MLP_TM, d), lambda i, f: (i, 0)), pl.BlockSpec((MLP_TM, d), lambda i, f: (i, 0))],
        out_shape=[jax.ShapeDtypeStruct((m, d), F32), jax.ShapeDtypeStruct((m, d), BF16)],
        compiler_params=_params("parallel", "arbitrary"),
        name="mlp",
    )(mn, h, wu, wd, g)


GATE_TM = 512


def _gate_final_kernel(gn_ref, wg_ref, p_ref, wp_ref, h_ref, gp_ref, gf_ref, o_ref):
    for r in range(GATE_TM // SUB_ROWS):
        rows = pl.ds(r * SUB_ROWS, SUB_ROWS)
        gate = jax.nn.sigmoid(jnp.dot(gn_ref[rows, :], wg_ref[...], preferred_element_type=F32))
        e = _rms(jnp.dot(p_ref[rows, :].astype(BF16), wp_ref[...], preferred_element_type=F32), gp_ref[...])
        o_ref[rows, :] = _rms(h_ref[rows, :] + gate * e, gf_ref[...])


def _gate_final(gn, wg, p2, wp, h, gp, gf):
    m, d = h.shape
    dp = p2.shape[1]
    return pl.pallas_call(
        _gate_final_kernel,
        grid=(m // GATE_TM,),
        in_specs=[
            pl.BlockSpec((GATE_TM, d), lambda i: (i, 0)),
            pl.BlockSpec((d, d), lambda i: (0, 0)),
            pl.BlockSpec((GATE_TM, dp), lambda i: (i, 0)),
            pl.BlockSpec((dp, d), lambda i: (0, 0)),
            pl.BlockSpec((GATE_TM, d), lambda i: (i, 0)),
            pl.BlockSpec((1, d), lambda i: (0, 0)),
            pl.BlockSpec((1, d), lambda i: (0, 0)),
        ],
        out_specs=pl.BlockSpec((GATE_TM, d), lambda i: (i, 0)),
        out_shape=jax.ShapeDtypeStruct((m, d), F32),
        compiler_params=_params("parallel"),
        name="gate_final",
    )(gn, wg, p2, wp, h, gp, gf)


def _rope_tables(seq):
    rows = seq // GRID_W
    row = np.repeat(np.arange(rows, dtype=np.float32), GRID_W)
    col = np.tile(np.arange(GRID_W, dtype=np.float32), rows)
    half = HEAD_DIM // 2
    inv_freq = np.float32(ROPE_THETA) ** (-np.arange(0, half, 2, dtype=np.float32) / np.float32(half))
    ang_r = row[:, None] * inv_freq
    ang_c = col[:, None] * inv_freq
    cr, sr, cc, sc = np.cos(ang_r), np.sin(ang_r), np.cos(ang_c), np.sin(ang_c)
    cos = np.concatenate([cr, cr, cc, cc], axis=-1).astype(np.float32)
    sin_signed = np.concatenate([-sr, sr, -sc, sc], axis=-1).astype(np.float32)
    return jnp.asarray(cos), jnp.asarray(sin_signed)


def _t5_bucket(rel):
    nb = N_BUCKETS // 2
    ret = jnp.where(rel > 0, nb, 0)
    n = jnp.abs(rel)
    max_exact = nb // 2
    nf = jnp.maximum(n, 1).astype(F32)
    large = max_exact + (jnp.log(nf / max_exact) / math.log(MAX_DISTANCE / max_exact)
                         * (nb - max_exact)).astype(jnp.int32)
    large = jnp.minimum(large, nb - 1)
    return ret + jnp.where(n < max_exact, n, large)


def _window_buckets():
    r = jnp.arange(BLOCK_Q, dtype=jnp.int32)
    j = jnp.arange(3 * BLOCK_Q, dtype=jnp.int32)
    rel = (j[None, :] - BLOCK_Q) - r[:, None]
    return jnp.where(jnp.abs(rel) <= WINDOW, _t5_bucket(rel), -1).astype(jnp.int32)


def kernel(x, p, attn_norm_g, w_in, q_norm_g, k_norm_g, sink_logits, w_out, mlp_norm_g, w_up, w_down, ple_w,
           ple_norm_g, gate_norm_g, w_gate, rel_bias_table, final_norm_g):
    batch, seq, d = x.shape
    assert w_in.shape[0] == 1, "gate_final fuses the final RMSNorm, which is only valid for a single layer"
    row = lambda v: v.reshape(1, -1).astype(F32)
    cos, sin_signed = _rope_tables(seq)
    h = x.reshape(batch * seq, d)
    proj = _in_proj(h, row(attn_norm_g), w_in[0].astype(BF16), cos, sin_signed, row(q_norm_g), row(k_norm_g), seq)
    oa = _global_attn(proj, batch, seq)
    ob = _window_attn(proj, _window_buckets(), rel_bias_table.reshape(-1).astype(F32),
                      sink_logits.reshape(-1).astype(F32), batch, seq)
    h1, mn = _out_proj(oa, ob, w_out[0].astype(BF16), h, row(mlp_norm_g))
    h2, gn = _mlp(mn, h1, w_up[0].astype(BF16), w_down[0].astype(BF16), row(gate_norm_g))
    out = _gate_final(gn, w_gate[0].astype(BF16), p.reshape(batch * seq, -1), ple_w[0].astype(BF16), h2,
                      row(ple_norm_g), row(final_norm_g))
    return out.reshape(batch, seq, d)
```

```python
import functools
import math

import jax
import jax.numpy as jnp
import numpy as np
from jax import lax
from jax.experimental import pallas as pl
from jax.experimental.pallas import tpu as pltpu

HEAD_DIM = 128
N_HEADS_A = 8
N_KV_A = 2
N_HEADS_B = 8
N_KV_B = 2
GROUP = 4
GRID_W = 64
BLOCK_Q = 128
WINDOW = 128
N_BUCKETS = 32
MAX_DISTANCE = 128
ROPE_THETA = 10000.0
EPS = 1e-6
NEG_INF = -1e30
LOG2_E = math.log2(math.e)
Q_SCALE = HEAD_DIM ** -0.5 * LOG2_E

V7X_VMEM_BYTES = 64 * 1024 * 1024
VMEM_LIMIT_BYTES = 56 * 1024 * 1024

SUB_ROWS = 256

BF16 = jnp.bfloat16
F32 = jnp.float32


def _params(*semantics):
    return pltpu.CompilerParams(dimension_semantics=semantics, vmem_limit_bytes=VMEM_LIMIT_BYTES)


def _rms(x, g):
    return x * lax.rsqrt(jnp.mean(x * x, axis=-1, keepdims=True) + EPS) * g


IN_TM = 512
IN_TN = 512

_PLAIN, _Q_A, _K_A, _Q_B = range(4)
_HEAD_KINDS = ([_Q_A] * N_HEADS_A + [_K_A] * N_KV_A + [_PLAIN] * N_KV_A
               + [_Q_B] * N_HEADS_B + [_PLAIN] * (2 * N_KV_B))


def _rope(y, cos, sin_signed):
    lane = lax.broadcasted_iota(jnp.int32, y.shape, 1)
    partner = jnp.where((lane % 64) < 32, pltpu.roll(y, 96, 1), pltpu.roll(y, 32, 1))
    return y * cos + partner * sin_signed


def _in_proj_kernel(x_ref, g_ref, w_ref, cos_ref, sin_ref, gq_ref, gk_ref, o_ref):
    heads_per_dot = IN_TN // HEAD_DIM
    for r in range(IN_TM // SUB_ROWS):
        rows = pl.ds(r * SUB_ROWS, SUB_ROWS)
        u = _rms(x_ref[rows, :], g_ref[...]).astype(BF16)
        cos, sin_signed = cos_ref[rows, :], sin_ref[rows, :]
        for c in range(w_ref.shape[1] // IN_TN):
            acc = jnp.dot(u, w_ref[:, c * IN_TN:(c + 1) * IN_TN], preferred_element_type=F32)
            for hh in range(heads_per_dot):
                head = c * heads_per_dot + hh
                a = acc[:, hh * HEAD_DIM:(hh + 1) * HEAD_DIM]
                kind = _HEAD_KINDS[head]
                if kind == _Q_A:
                    a = _rope(_rms(a, gq_ref[...]), cos, sin_signed) * Q_SCALE
                elif kind == _K_A:
                    a = _rope(_rms(a, gk_ref[...]), cos, sin_signed)
                elif kind == _Q_B:
                    a = a * Q_SCALE
                o_ref[rows, head * HEAD_DIM:(head + 1) * HEAD_DIM] = a.astype(BF16)


def _in_proj(x2, g, w, cos, sin_signed, gq, gk, seq):
    m, d = x2.shape
    n = w.shape[1]
    assert n == len(_HEAD_KINDS) * HEAD_DIM
    pos_tiles = seq // IN_TM
    return pl.pallas_call(
        _in_proj_kernel,
        grid=(m // IN_TM,),
        in_specs=[
            pl.BlockSpec((IN_TM, d), lambda i: (i, 0)),
            pl.BlockSpec((1, d), lambda i: (0, 0)),
            pl.BlockSpec((d, n), lambda i: (0, 0)),
            pl.BlockSpec((IN_TM, HEAD_DIM), lambda i: (i % pos_tiles, 0)),
            pl.BlockSpec((IN_TM, HEAD_DIM), lambda i: (i % pos_tiles, 0)),
            pl.BlockSpec((1, HEAD_DIM), lambda i: (0, 0)),
            pl.BlockSpec((1, HEAD_DIM), lambda i: (0, 0)),
        ],
        out_specs=pl.BlockSpec((IN_TM, n), lambda i: (i, 0)),
        out_shape=jax.ShapeDtypeStruct((m, n), BF16),
        compiler_params=_params("parallel"),
        name="in_proj",
    )(x2, g, w, cos, sin_signed, gq, gk)


GA_ROWS = 256


def _global_attn_kernel(q_ref, k_ref, v_ref, o_ref, v1_ref, s_ref, m_ref):
    seq = q_ref.shape[0]
    n_blocks = seq // GA_ROWS

    v1_ref[:, :HEAD_DIM] = v_ref[...]
    v1_ref[:, HEAD_DIM:] = jnp.ones(v_ref.shape, BF16)

    def rows_of(r):
        return pl.ds(pl.multiple_of(r * GA_ROWS, GA_ROWS), GA_ROWS)

    def scores(r, g, slot):
        q = q_ref[rows_of(r), g * HEAD_DIM:(g + 1) * HEAD_DIM]
        s = lax.dot_general(q, k_ref[...], (((1,), (1,)), ((), ())), preferred_element_type=F32)
        s_ref[slot] = s
        m_ref[slot] = jnp.max(s, axis=-1, keepdims=True)

    def apply(r, g, slot):
        p = jnp.exp2(s_ref[slot] - m_ref[slot]).astype(BF16)
        o = jnp.dot(p, v1_ref[...], preferred_element_type=F32)
        o_ref[rows_of(r), g * HEAD_DIM:(g + 1) * HEAD_DIM] = (o[:, :HEAD_DIM] / o[:, HEAD_DIM:]).astype(BF16)

    def row_block(r, last):
        for g in range(GROUP):
            slot = g % 2
            if g + 1 < GROUP:
                scores(r, g + 1, 1 - slot)
            elif not last:
                scores(r + 1, 0, 1 - slot)
            apply(r, g, slot)

    scores(0, 0, 0)

    def body(r, carry):
        row_block(r, last=False)
        return carry

    lax.fori_loop(0, n_blocks - 1, body, 0)
    row_block(n_blocks - 1, last=True)


def _global_attn(proj, batch, seq):
    assert GROUP % 2 == 0, "score slots alternate per head and must line up across row blocks"
    k_col = N_HEADS_A
    v_col = N_HEADS_A + N_KV_A
    return pl.pallas_call(
        _global_attn_kernel,
        grid=(batch, N_KV_A),
        in_specs=[
            pl.BlockSpec((seq, GROUP * HEAD_DIM), lambda b, k: (b, k)),
            pl.BlockSpec((seq, HEAD_DIM), lambda b, k: (b, k_col + k)),
            pl.BlockSpec((seq, HEAD_DIM), lambda b, k: (b, v_col + k)),
        ],
        out_specs=pl.BlockSpec((seq, GROUP * HEAD_DIM), lambda b, k: (b, k)),
        out_shape=jax.ShapeDtypeStruct((batch * seq, N_HEADS_A * HEAD_DIM), BF16),
        scratch_shapes=[
            pltpu.VMEM((seq, 2 * HEAD_DIM), BF16),
            pltpu.VMEM((2, GA_ROWS, seq), F32),
            pltpu.VMEM((2, GA_ROWS, 1), F32),
        ],
        compiler_params=_params("parallel", "parallel"),
        name="global_attn",
    )(proj, proj, proj)


WA_ROWS = 256
WA_BAND = WA_ROWS + 2 * WINDOW


def _window_attn_kernel(sink_ref, table_ref, q_ref, k_ref, v_ref, bucket_ref, o_ref,
                        kpad_ref, v1pad_ref, bias_ref, sink_ref_b, s_ref, m_ref):
    kvh = pl.program_id(1)
    seq = q_ref.shape[0]
    n_stages = seq // WA_ROWS

    zeros = jnp.zeros((WINDOW, 2 * HEAD_DIM), BF16)
    kpad_ref[:WINDOW, :] = zeros[:, :HEAD_DIM]
    kpad_ref[WINDOW + seq:, :] = zeros[:, :HEAD_DIM]
    kpad_ref[WINDOW:WINDOW + seq, :] = k_ref[...]
    v1pad_ref[:WINDOW, :] = zeros
    v1pad_ref[WINDOW + seq:, :] = zeros
    v1pad_ref[WINDOW:WINDOW + seq, :HEAD_DIM] = v_ref[...]
    v1pad_ref[WINDOW:WINDOW + seq, HEAD_DIM:] = jnp.ones(v_ref.shape, BF16)

    bucket = bucket_ref[...]
    for g in range(GROUP):
        head = kvh * GROUP + g
        line = jnp.full(bucket.shape, NEG_INF, F32)
        for b in range(N_BUCKETS):
            line = jnp.where(bucket == b, table_ref[b * N_HEADS_B + head] * LOG2_E, line)
        tile = jnp.broadcast_to(line[:1], (WA_ROWS, WA_BAND))
        bias_ref[g * WA_ROWS:(g + 1) * WA_ROWS, :] = pltpu.roll(tile, 0, 1, stride=1, stride_axis=0)
        sink_ref_b[g * WA_ROWS:(g + 1) * WA_ROWS, :] = jnp.full((WA_ROWS, HEAD_DIM), sink_ref[head] * LOG2_E, F32)

    col = lax.broadcasted_iota(jnp.int32, (GROUP * WA_ROWS, WA_BAND), 1)

    def rows_of(n):
        return pl.ds(pl.multiple_of(n * WA_ROWS, WA_ROWS), WA_ROWS)

    def band_of(n):
        return pl.ds(pl.multiple_of(n * WA_ROWS, WA_ROWS), WA_BAND)

    def scores(n, slot):
        q = jnp.concatenate([q_ref[rows_of(n), g * HEAD_DIM:(g + 1) * HEAD_DIM] for g in range(GROUP)], axis=0)
        s = lax.dot_general(q, kpad_ref[band_of(n), :], (((1,), (1,)), ((), ())), preferred_element_type=F32)
        in_range = ((col >= WINDOW) | (n > 0)) & ((col < WA_BAND - WINDOW) | (n < n_stages - 1))
        s = jnp.where(in_range, s + bias_ref[...], NEG_INF)
        s_ref[slot] = s
        row_max = jnp.broadcast_to(jnp.max(s, axis=-1, keepdims=True), sink_ref_b.shape)
        m_ref[slot] = jnp.maximum(row_max, sink_ref_b[...])

    def apply(n, slot):
        m = m_ref[slot]
        p = jnp.concatenate([jnp.exp2(s_ref[slot, :, c * HEAD_DIM:(c + 1) * HEAD_DIM] - m)
                             for c in range(WA_BAND // HEAD_DIM)], axis=1).astype(BF16)
        o = jnp.dot(p, v1pad_ref[band_of(n), :], preferred_element_type=F32)
        o = o[:, :HEAD_DIM] / (o[:, HEAD_DIM:] + jnp.exp2(sink_ref_b[...] - m))
        for g in range(GROUP):
            o_ref[rows_of(n), g * HEAD_DIM:(g + 1) * HEAD_DIM] = o[g * WA_ROWS:(g + 1) * WA_ROWS].astype(BF16)

    scores(0, 0)

    def body(i, carry):
        for slot in range(2):
            n = 2 * i + slot
            scores(n + 1, 1 - slot)
            apply(n, slot)
        return carry

    lax.fori_loop(0, n_stages // 2 - 1, body, 0)
    scores(n_stages - 1, 1)
    apply(n_stages - 2, 0)
    apply(n_stages - 1, 1)


def _window_attn(proj, bucket, table, sink, batch, seq):
    assert (seq // WA_ROWS) % 2 == 0, "the stage loop is unrolled by two so score slots are static"
    q_col = (N_HEADS_A + 2 * N_KV_A) // GROUP
    k_col = N_HEADS_A + 2 * N_KV_A + N_HEADS_B
    v_col = k_col + N_KV_B
    grid_spec = pltpu.PrefetchScalarGridSpec(
        num_scalar_prefetch=2,
        grid=(batch, N_KV_B),
        in_specs=[
            pl.BlockSpec((seq, GROUP * HEAD_DIM), lambda b, k, s, t: (b, q_col + k)),
            pl.BlockSpec((seq, HEAD_DIM), lambda b, k, s, t: (b, k_col + k)),
            pl.BlockSpec((seq, HEAD_DIM), lambda b, k, s, t: (b, v_col + k)),
            pl.BlockSpec((8, WA_BAND), lambda b, k, s, t: (0, 0)),
        ],
        out_specs=pl.BlockSpec((seq, GROUP * HEAD_DIM), lambda b, k, s, t: (b, k)),
        scratch_shapes=[
            pltpu.VMEM((seq + 2 * WINDOW, HEAD_DIM), BF16),
            pltpu.VMEM((seq + 2 * WINDOW, 2 * HEAD_DIM), BF16),
            pltpu.VMEM((GROUP * WA_ROWS, WA_BAND), F32),
            pltpu.VMEM((GROUP * WA_ROWS, HEAD_DIM), F32),
            pltpu.VMEM((2, GROUP * WA_ROWS, WA_BAND), F32),
            pltpu.VMEM((2, GROUP * WA_ROWS, HEAD_DIM), F32),
        ],
    )
    return pl.pallas_call(
        _window_attn_kernel,
        grid_spec=grid_spec,
        out_shape=jax.ShapeDtypeStruct((batch * seq, N_HEADS_B * HEAD_DIM), BF16),
        compiler_params=_params("parallel", "parallel"),
        name="window_attn",
    )(sink, table, proj, proj, proj, bucket)


OUT_TM = 512


def _out_proj_kernel(oa_ref, ob_ref, w_ref, x_ref, g_ref, h_ref, m_ref):
    ka = oa_ref.shape[1]
    for r in range(OUT_TM // SUB_ROWS):
        rows = pl.ds(r * SUB_ROWS, SUB_ROWS)
        h = x_ref[rows, :]
        h = h + jnp.dot(oa_ref[rows, :], w_ref[:ka, :], preferred_element_type=F32)
        h = h + jnp.dot(ob_ref[rows, :], w_ref[ka:, :], preferred_element_type=F32)
        h_ref[rows, :] = h
        m_ref[rows, :] = _rms(h, g_ref[...]).astype(BF16)


def _out_proj(oa, ob, w, x2, g):
    m, d = x2.shape
    ka, kb = oa.shape[1], ob.shape[1]
    return pl.pallas_call(
        _out_proj_kernel,
        grid=(m // OUT_TM,),
        in_specs=[
            pl.BlockSpec((OUT_TM, ka), lambda i: (i, 0)),
            pl.BlockSpec((OUT_TM, kb), lambda i: (i, 0)),
            pl.BlockSpec((ka + kb, d), lambda i: (0, 0)),
            pl.BlockSpec((OUT_TM, d), lambda i: (i, 0)),
            pl.BlockSpec((1, d), lambda i: (0, 0)),
        ],
        out_specs=[pl.BlockSpec((OUT_TM, d), lambda i: (i, 0)), pl.BlockSpec((OUT_TM, d), lambda i: (i, 0))],
        out_shape=[jax.ShapeDtypeStruct((m, d), F32), jax.ShapeDtypeStruct((m, d), BF16)],
        compiler_params=_params("parallel"),
        name="out_proj",
    )(oa, ob, w, x2, g)


MLP_TM = 512
MLP_TF = 1024


def _mlp_kernel(m_ref, h_ref, wu_ref, wd_ref, g_ref, h2_ref, gn_ref):
    f = pl.program_id(1)

    @pl.when(f == 0)
    def _():
        h2_ref[...] = h_ref[...]

    a = jnp.dot(m_ref[...], wu_ref[...], preferred_element_type=F32)
    act = jnp.square(jnp.maximum(a, 0.0)).astype(BF16)
    h2_ref[...] += jnp.dot(act, wd_ref[...], preferred_element_type=F32)

    @pl.when(f == pl.num_programs(1) - 1)
    def _():
        gn_ref[...] = _rms(h2_ref[...], g_ref[...]).astype(BF16)


def _mlp(mn, h, wu, wd, g):
    m, d = h.shape
    dff = wu.shape[1]
    return pl.pallas_call(
        _mlp_kernel,
        grid=(m // MLP_TM, dff // MLP_TF),
        in_specs=[
            pl.BlockSpec((MLP_TM, d), lambda i, f: (i, 0)),
            pl.BlockSpec((MLP_TM, d), lambda i, f: (i, 0)),
            pl.BlockSpec((d, MLP_TF), lambda i, f: (0, f)),
            pl.BlockSpec((MLP_TF, d), lambda i, f: (f, 0)),
            pl.BlockSpec((1, d), lambda i, f: (0, 0)),
        ],
        out_specs=[pl.BlockSpec((MLP_TM, d), lambda i, f: (i, 0)), pl.BlockSpec((MLP_TM, d), lambda i, f: (i, 0))],
        out_shape=[jax.ShapeDtypeStruct((m, d), F32), jax.ShapeDtypeStruct((m, d), BF16)],
        compiler_params=_params("parallel", "arbitrary"),
        name="mlp",
    )(mn, h, wu, wd, g)


GATE_TM = 512


def _gate_final_kernel(gn_ref, wg_ref, p_ref, wp_ref, h_ref, gp_ref, gf_ref, o_ref):
    for r in range(GATE_TM // SUB_ROWS):
        rows = pl.ds(r * SUB_ROWS, SUB_ROWS)
        gate = jax.nn.sigmoid(jnp.dot(gn_ref[rows, :], wg_ref[...], preferred_element_type=F32))
        e = _rms(jnp.dot(p_ref[rows, :].astype(BF16), wp_ref[...], preferred_element_type=F32), gp_ref[...])
        o_ref[rows, :] = _rms(h_ref[rows, :] + gate * e, gf_ref[...])


def _gate_final(gn, wg, p2, wp, h, gp, gf):
    m, d = h.shape
    dp = p2.shape[1]
    return pl.pallas_call(
        _gate_final_kernel,
        grid=(m // GATE_TM,),
        in_specs=[
            pl.BlockSpec((GATE_TM, d), lambda i: (i, 0)),
            pl.BlockSpec((d, d), lambda i: (0, 0)),
            pl.BlockSpec((GATE_TM, dp), lambda i: (i, 0)),
            pl.BlockSpec((dp, d), lambda i: (0, 0)),
            pl.BlockSpec((GATE_TM, d), lambda i: (i, 0)),
            pl.BlockSpec((1, d), lambda i: (0, 0)),
            pl.BlockSpec((1, d), lambda i: (0, 0)),
        ],
        out_specs=pl.BlockSpec((GATE_TM, d), lambda i: (i, 0)),
        out_shape=jax.ShapeDtypeStruct((m, d), F32),
        compiler_params=_params("parallel"),
        name="gate_final",
    )(gn, wg, p2, wp, h, gp, gf)


def _rope_tables(seq):
    rows = seq // GRID_W
    row = np.repeat(np.arange(rows, dtype=np.float32), GRID_W)
    col = np.tile(np.arange(GRID_W, dtype=np.float32), rows)
    half = HEAD_DIM // 2
    inv_freq = np.float32(ROPE_THETA) ** (-np.arange(0, half, 2, dtype=np.float32) / np.float32(half))
    ang_r = row[:, None] * inv_freq
    ang_c = col[:, None] * inv_freq
    cr, sr, cc, sc = np.cos(ang_r), np.sin(ang_r), np.cos(ang_c), np.sin(ang_c)
    cos = np.concatenate([cr, cr, cc, cc], axis=-1).astype(np.float32)
    sin_signed = np.concatenate([-sr, sr, -sc, sc], axis=-1).astype(np.float32)
    return jnp.asarray(cos), jnp.asarray(sin_signed)


def _t5_bucket(rel):
    nb = N_BUCKETS // 2
    ret = jnp.where(rel > 0, nb, 0)
    n = jnp.abs(rel)
    max_exact = nb // 2
    nf = jnp.maximum(n, 1).astype(F32)
    large = max_exact + (jnp.log(nf / max_exact) / math.log(MAX_DISTANCE / max_exact)
                         * (nb - max_exact)).astype(jnp.int32)
    large = jnp.minimum(large, nb - 1)
    return ret + jnp.where(n < max_exact, n, large)


def _window_buckets():
    j = jnp.arange(WA_BAND, dtype=jnp.int32)
    rel = jnp.broadcast_to(j[None, :] - WINDOW, (8, WA_BAND))
    return jnp.where(jnp.abs(rel) <= WINDOW, _t5_bucket(rel), -1).astype(jnp.int32)


def kernel(x, p, attn_norm_g, w_in, q_norm_g, k_norm_g, sink_logits, w_out, mlp_norm_g, w_up, w_down, ple_w,
           ple_norm_g, gate_norm_g, w_gate, rel_bias_table, final_norm_g):
    batch, seq, d = x.shape
    assert w_in.shape[0] == 1, "gate_final fuses the final RMSNorm, which is only valid for a single layer"
    row = lambda v: v.reshape(1, -1).astype(F32)
    cos, sin_signed = _rope_tables(seq)
    h = x.reshape(batch * seq, d)
    proj = _in_proj(h, row(attn_norm_g), w_in[0].astype(BF16), cos, sin_signed, row(q_norm_g), row(k_norm_g), seq)
    oa = _global_attn(proj, batch, seq)
    ob = _window_attn(proj, _window_buckets(), rel_bias_table.reshape(-1).astype(F32),
                      sink_logits.reshape(-1).astype(F32), batch, seq)
    h1, mn = _out_proj(oa, ob, w_out[0].astype(BF16), h, row(mlp_norm_g))
    h2, gn = _mlp(mn, h1, w_up[0].astype(BF16), w_down[0].astype(BF16), row(gate_norm_g))
    out = _gate_final(gn, w_gate[0].astype(BF16), p.reshape(batch * seq, -1), ple_w[0].astype(BF16), h2,
                      row(ple_norm_g), row(final_norm_g))
    return out.reshape(batch, seq, d)
```

```python
import functools
import math

import jax
import jax.numpy as jnp
import numpy as np
from jax import lax
from jax.experimental import pallas as pl
from jax.experimental.pallas import tpu as pltpu

HEAD_DIM = 128
N_HEADS_A = 8
N_KV_A = 2
N_HEADS_B = 8
N_KV_B = 2
GROUP = 4
GRID_W = 64
BLOCK_Q = 128
WINDOW = 128
N_BUCKETS = 32
MAX_DISTANCE = 128
ROPE_THETA = 10000.0
EPS = 1e-6
NEG_INF = -1e30
LOG2_E = math.log2(math.e)
Q_SCALE = HEAD_DIM ** -0.5 * LOG2_E

V7X_VMEM_BYTES = 64 * 1024 * 1024
VMEM_LIMIT_BYTES = 56 * 1024 * 1024

SUB_ROWS = 256

BF16 = jnp.bfloat16
F32 = jnp.float32


def _params(*semantics):
    return pltpu.CompilerParams(dimension_semantics=semantics, vmem_limit_bytes=VMEM_LIMIT_BYTES)


def _rms(x, g):
    return x * lax.rsqrt(jnp.mean(x * x, axis=-1, keepdims=True) + EPS) * g


IN_TM = 512
IN_TN = 512

_PLAIN, _Q_A, _K_A, _Q_B = range(4)
_HEAD_KINDS = ([_Q_A] * N_HEADS_A + [_K_A] * N_KV_A + [_PLAIN] * N_KV_A
               + [_Q_B] * N_HEADS_B + [_PLAIN] * (2 * N_KV_B))


def _rope(y, cos, sin_signed):
    lane = lax.broadcasted_iota(jnp.int32, y.shape, 1)
    partner = jnp.where((lane % 64) < 32, pltpu.roll(y, 96, 1), pltpu.roll(y, 32, 1))
    return y * cos + partner * sin_signed


def _in_proj_kernel(x_ref, g_ref, w_ref, cos_ref, sin_ref, gq_ref, gk_ref, o_ref):
    heads_per_dot = IN_TN // HEAD_DIM
    for r in range(IN_TM // SUB_ROWS):
        rows = pl.ds(r * SUB_ROWS, SUB_ROWS)
        u = _rms(x_ref[rows, :], g_ref[...]).astype(BF16)
        cos, sin_signed = cos_ref[rows, :], sin_ref[rows, :]
        for c in range(w_ref.shape[1] // IN_TN):
            acc = jnp.dot(u, w_ref[:, c * IN_TN:(c + 1) * IN_TN], preferred_element_type=F32)
            for hh in range(heads_per_dot):
                head = c * heads_per_dot + hh
                a = acc[:, hh * HEAD_DIM:(hh + 1) * HEAD_DIM]
                kind = _HEAD_KINDS[head]
                if kind == _Q_A:
                    a = _rope(_rms(a, gq_ref[...]), cos, sin_signed) * Q_SCALE
                elif kind == _K_A:
                    a = _rope(_rms(a, gk_ref[...]), cos, sin_signed)
                elif kind == _Q_B:
                    a = a * Q_SCALE
                o_ref[rows, head * HEAD_DIM:(head + 1) * HEAD_DIM] = a.astype(BF16)


def _in_proj(x2, g, w, cos, sin_signed, gq, gk, seq):
    m, d = x2.shape
    n = w.shape[1]
    assert n == len(_HEAD_KINDS) * HEAD_DIM
    pos_tiles = seq // IN_TM
    return pl.pallas_call(
        _in_proj_kernel,
        grid=(m // IN_TM,),
        in_specs=[
            pl.BlockSpec((IN_TM, d), lambda i: (i, 0)),
            pl.BlockSpec((1, d), lambda i: (0, 0)),
            pl.BlockSpec((d, n), lambda i: (0, 0)),
            pl.BlockSpec((IN_TM, HEAD_DIM), lambda i: (i % pos_tiles, 0)),
            pl.BlockSpec((IN_TM, HEAD_DIM), lambda i: (i % pos_tiles, 0)),
            pl.BlockSpec((1, HEAD_DIM), lambda i: (0, 0)),
            pl.BlockSpec((1, HEAD_DIM), lambda i: (0, 0)),
        ],
        out_specs=pl.BlockSpec((IN_TM, n), lambda i: (i, 0)),
        out_shape=jax.ShapeDtypeStruct((m, n), BF16),
        compiler_params=_params("parallel"),
        name="in_proj",
    )(x2, g, w, cos, sin_signed, gq, gk)


GA_ROWS = 256
CAST_COLS = 2048


def _global_attn_kernel(*refs, n_cast):
    q_ref, k_ref, v_ref = refs[:3]
    w_f32 = refs[3:3 + n_cast]
    o_ref = refs[3 + n_cast]
    w_bf16 = refs[4 + n_cast:4 + 2 * n_cast]
    v1_ref, s_ref, m_ref = refs[4 + 2 * n_cast:7 + 2 * n_cast]
    in_bufs = refs[7 + 2 * n_cast:7 + 3 * n_cast]
    out_bufs = refs[7 + 3 * n_cast:7 + 4 * n_cast]
    in_sem, out_sem = refs[7 + 4 * n_cast:]

    seq = q_ref.shape[0]
    n_blocks = seq // GA_ROWS
    first_chunk = (pl.program_id(0) * pl.num_programs(1) + pl.program_id(1)) * n_blocks

    def in_copy(w, chunk, slot):
        rows = in_bufs[w].shape[1]
        src = w_f32[w].at[pl.ds(pl.multiple_of(chunk * rows, rows), rows)]
        return pltpu.make_async_copy(src, in_bufs[w].at[slot], in_sem.at[w, slot])

    def out_copy(w, chunk):
        rows = out_bufs[w].shape[0]
        dst = w_bf16[w].at[pl.ds(pl.multiple_of(chunk * rows, rows), rows)]
        return pltpu.make_async_copy(out_bufs[w], dst, out_sem.at[w])

    def cast_chunk(r, first, last):
        chunk = first_chunk + r
        slot = r % 2
        for w in range(n_cast):
            in_copy(w, chunk, slot).wait()
            if not last:
                in_copy(w, chunk + 1, 1 - slot).start()
            if not first:
                out_copy(w, chunk - 1).wait()
            out_bufs[w][...] = in_bufs[w][slot].astype(BF16)
            out_copy(w, chunk).start()

    for w in range(n_cast):
        in_copy(w, first_chunk, 0).start()

    v1_ref[:, :HEAD_DIM] = v_ref[...]
    v1_ref[:, HEAD_DIM:] = jnp.ones(v_ref.shape, BF16)

    def rows_of(r):
        return pl.ds(pl.multiple_of(r * GA_ROWS, GA_ROWS), GA_ROWS)

    def scores(r, g, slot):
        q = q_ref[rows_of(r), g * HEAD_DIM:(g + 1) * HEAD_DIM]
        s = lax.dot_general(q, k_ref[...], (((1,), (1,)), ((), ())), preferred_element_type=F32)
        s_ref[slot] = s
        m_ref[slot] = jnp.max(s, axis=-1, keepdims=True)

    def apply(r, g, slot):
        p = jnp.exp2(s_ref[slot] - m_ref[slot]).astype(BF16)
        o = jnp.dot(p, v1_ref[...], preferred_element_type=F32)
        o_ref[rows_of(r), g * HEAD_DIM:(g + 1) * HEAD_DIM] = (o[:, :HEAD_DIM] / o[:, HEAD_DIM:]).astype(BF16)

    def row_block(r, first, last):
        for g in range(GROUP):
            slot = g % 2
            if g + 1 < GROUP:
                scores(r, g + 1, 1 - slot)
            elif not last:
                scores(r + 1, 0, 1 - slot)
            apply(r, g, slot)
        cast_chunk(r, first, last)

    scores(0, 0, 0)
    row_block(0, first=True, last=False)

    def body(r, carry):
        row_block(r, first=False, last=False)
        return carry

    lax.fori_loop(1, n_blocks - 1, body, 0)
    row_block(n_blocks - 1, first=False, last=True)
    for w in range(n_cast):
        out_copy(w, first_chunk + n_blocks - 1).wait()


def _global_attn(proj, weights, batch, seq):
    assert GROUP % 2 == 0, "score slots alternate per head and must line up across row blocks"
    n_blocks = seq // GA_ROWS
    assert n_blocks >= 3 and n_blocks % 2 == 0
    n_chunks = batch * N_KV_A * n_blocks
    chunk_rows = []
    for w in weights:
        assert w.shape[1] == CAST_COLS and w.shape[0] % (16 * n_chunks) == 0, w.shape
        chunk_rows.append(w.shape[0] // n_chunks)
    n_cast = len(weights)
    k_col = N_HEADS_A
    v_col = N_HEADS_A + N_KV_A
    any_spec = pl.BlockSpec(memory_space=pl.ANY)
    attn_spec = pl.BlockSpec((seq, GROUP * HEAD_DIM), lambda b, k: (b, k))
    outs = pl.pallas_call(
        functools.partial(_global_attn_kernel, n_cast=n_cast),
        grid=(batch, N_KV_A),
        in_specs=[
            attn_spec,
            pl.BlockSpec((seq, HEAD_DIM), lambda b, k: (b, k_col + k)),
            pl.BlockSpec((seq, HEAD_DIM), lambda b, k: (b, v_col + k)),
        ] + [any_spec] * n_cast,
        out_specs=[attn_spec] + [any_spec] * n_cast,
        out_shape=[jax.ShapeDtypeStruct((batch * seq, N_HEADS_A * HEAD_DIM), BF16)]
        + [jax.ShapeDtypeStruct(w.shape, BF16) for w in weights],
        scratch_shapes=[
            pltpu.VMEM((seq, 2 * HEAD_DIM), BF16),
            pltpu.VMEM((2, GA_ROWS, seq), F32),
            pltpu.VMEM((2, GA_ROWS, 1), F32),
        ] + [pltpu.VMEM((2, rows, CAST_COLS), F32) for rows in chunk_rows]
        + [pltpu.VMEM((rows, CAST_COLS), BF16) for rows in chunk_rows]
        + [pltpu.SemaphoreType.DMA((n_cast, 2)), pltpu.SemaphoreType.DMA((n_cast,))],
        compiler_params=_params("arbitrary", "arbitrary"),
        name="global_attn",
    )(proj, proj, proj, *weights)
    return outs[0], outs[1:]


WA_ROWS = 256
WA_BAND = WA_ROWS + 2 * WINDOW


def _window_attn_kernel(sink_ref, table_ref, q_ref, k_ref, v_ref, bucket_ref, o_ref,
                        kpad_ref, v1pad_ref, bias_ref, sink_ref_b, s_ref, m_ref):
    kvh = pl.program_id(1)
    seq = q_ref.shape[0]
    n_stages = seq // WA_ROWS

    zeros = jnp.zeros((WINDOW, 2 * HEAD_DIM), BF16)
    kpad_ref[:WINDOW, :] = zeros[:, :HEAD_DIM]
    kpad_ref[WINDOW + seq:, :] = zeros[:, :HEAD_DIM]
    kpad_ref[WINDOW:WINDOW + seq, :] = k_ref[...]
    v1pad_ref[:WINDOW, :] = zeros
    v1pad_ref[WINDOW + seq:, :] = zeros
    v1pad_ref[WINDOW:WINDOW + seq, :HEAD_DIM] = v_ref[...]
    v1pad_ref[WINDOW:WINDOW + seq, HEAD_DIM:] = jnp.ones(v_ref.shape, BF16)

    bucket = bucket_ref[...]
    for g in range(GROUP):
        head = kvh * GROUP + g
        line = jnp.full(bucket.shape, NEG_INF, F32)
        for b in range(N_BUCKETS):
            line = jnp.where(bucket == b, table_ref[b * N_HEADS_B + head] * LOG2_E, line)
        tile = jnp.broadcast_to(line[:1], (WA_ROWS, WA_BAND))
        bias_ref[g * WA_ROWS:(g + 1) * WA_ROWS, :] = pltpu.roll(tile, 0, 1, stride=1, stride_axis=0)
        sink_ref_b[g * WA_ROWS:(g + 1) * WA_ROWS, :] = jnp.full((WA_ROWS, HEAD_DIM), sink_ref[head] * LOG2_E, F32)

    col = lax.broadcasted_iota(jnp.int32, (GROUP * WA_ROWS, WA_BAND), 1)

    def rows_of(n):
        return pl.ds(pl.multiple_of(n * WA_ROWS, WA_ROWS), WA_ROWS)

    def band_of(n):
        return pl.ds(pl.multiple_of(n * WA_ROWS, WA_ROWS), WA_BAND)

    def scores(n, slot):
        q = jnp.concatenate([q_ref[rows_of(n), g * HEAD_DIM:(g + 1) * HEAD_DIM] for g in range(GROUP)], axis=0)
        s = lax.dot_general(q, kpad_ref[band_of(n), :], (((1,), (1,)), ((), ())), preferred_element_type=F32)
        in_range = ((col >= WINDOW) | (n > 0)) & ((col < WA_BAND - WINDOW) | (n < n_stages - 1))
        s = jnp.where(in_range, s + bias_ref[...], NEG_INF)
        s_ref[slot] = s
        row_max = jnp.broadcast_to(jnp.max(s, axis=-1, keepdims=True), sink_ref_b.shape)
        m_ref[slot] = jnp.maximum(row_max, sink_ref_b[...])

    def apply(n, slot):
        m = m_ref[slot]
        p = jnp.concatenate([jnp.exp2(s_ref[slot, :, c * HEAD_DIM:(c + 1) * HEAD_DIM] - m)
                             for c in range(WA_BAND // HEAD_DIM)], axis=1).astype(BF16)
        o = jnp.dot(p, v1pad_ref[band_of(n), :], preferred_element_type=F32)
        o = o[:, :HEAD_DIM] / (o[:, HEAD_DIM:] + jnp.exp2(sink_ref_b[...] - m))
        for g in range(GROUP):
            o_ref[rows_of(n), g * HEAD_DIM:(g + 1) * HEAD_DIM] = o[g * WA_ROWS:(g + 1) * WA_ROWS].astype(BF16)

    scores(0, 0)

    def body(i, carry):
        for slot in range(2):
            n = 2 * i + slot
            scores(n + 1, 1 - slot)
            apply(n, slot)
        return carry

    lax.fori_loop(0, n_stages // 2 - 1, body, 0)
    scores(n_stages - 1, 1)
    apply(n_stages - 2, 0)
    apply(n_stages - 1, 1)


def _window_attn(proj, bucket, table, sink, batch, seq):
    assert (seq // WA_ROWS) % 2 == 0, "the stage loop is unrolled by two so score slots are static"
    q_col = (N_HEADS_A + 2 * N_KV_A) // GROUP
    k_col = N_HEADS_A + 2 * N_KV_A + N_HEADS_B
    v_col = k_col + N_KV_B
    grid_spec = pltpu.PrefetchScalarGridSpec(
        num_scalar_prefetch=2,
        grid=(batch, N_KV_B),
        in_specs=[
            pl.BlockSpec((seq, GROUP * HEAD_DIM), lambda b, k, s, t: (b, q_col + k)),
            pl.BlockSpec((seq, HEAD_DIM), lambda b, k, s, t: (b, k_col + k)),
            pl.BlockSpec((seq, HEAD_DIM), lambda b, k, s, t: (b, v_col + k)),
            pl.BlockSpec((8, WA_BAND), lambda b, k, s, t: (0, 0)),
        ],
        out_specs=pl.BlockSpec((seq, GROUP * HEAD_DIM), lambda b, k, s, t: (b, k)),
        scratch_shapes=[
            pltpu.VMEM((seq + 2 * WINDOW, HEAD_DIM), BF16),
            pltpu.VMEM((seq + 2 * WINDOW, 2 * HEAD_DIM), BF16),
            pltpu.VMEM((GROUP * WA_ROWS, WA_BAND), F32),
            pltpu.VMEM((GROUP * WA_ROWS, HEAD_DIM), F32),
            pltpu.VMEM((2, GROUP * WA_ROWS, WA_BAND), F32),
            pltpu.VMEM((2, GROUP * WA_ROWS, HEAD_DIM), F32),
        ],
    )
    return pl.pallas_call(
        _window_attn_kernel,
        grid_spec=grid_spec,
        out_shape=jax.ShapeDtypeStruct((batch * seq, N_HEADS_B * HEAD_DIM), BF16),
        compiler_params=_params("parallel", "parallel"),
        name="window_attn",
    )(sink, table, proj, proj, proj, bucket)


OUT_TM = 512


def _out_proj_kernel(oa_ref, ob_ref, w_ref, x_ref, g_ref, h_ref, m_ref):
    ka = oa_ref.shape[1]
    for r in range(OUT_TM // SUB_ROWS):
        rows = pl.ds(r * SUB_ROWS, SUB_ROWS)
        h = x_ref[rows, :]
        h = h + jnp.dot(oa_ref[rows, :], w_ref[:ka, :], preferred_element_type=F32)
        h = h + jnp.dot(ob_ref[rows, :], w_ref[ka:, :], preferred_element_type=F32)
        h_ref[rows, :] = h
        m_ref[rows, :] = _rms(h, g_ref[...]).astype(BF16)


def _out_proj(oa, ob, w, x2, g):
    m, d = x2.shape
    ka, kb = oa.shape[1], ob.shape[1]
    return pl.pallas_call(
        _out_proj_kernel,
        grid=(m // OUT_TM,),
        in_specs=[
            pl.BlockSpec((OUT_TM, ka), lambda i: (i, 0)),
            pl.BlockSpec((OUT_TM, kb), lambda i: (i, 0)),
            pl.BlockSpec((ka + kb, d), lambda i: (0, 0)),
            pl.BlockSpec((OUT_TM, d), lambda i: (i, 0)),
            pl.BlockSpec((1, d), lambda i: (0, 0)),
        ],
        out_specs=[pl.BlockSpec((OUT_TM, d), lambda i: (i, 0)), pl.BlockSpec((OUT_TM, d), lambda i: (i, 0))],
        out_shape=[jax.ShapeDtypeStruct((m, d), F32), jax.ShapeDtypeStruct((m, d), BF16)],
        compiler_params=_params("parallel"),
        name="out_proj",
    )(oa, ob, w, x2, g)


MLP_TM = 512
MLP_TF = 1024


def _mlp_kernel(m_ref, h_ref, wu_ref, wd_ref, g_ref, h2_ref, gn_ref):
    f = pl.program_id(1)

    @pl.when(f == 0)
    def _():
        h2_ref[...] = h_ref[...]

    a = jnp.dot(m_ref[...], wu_ref[...], preferred_element_type=F32)
    act = jnp.square(jnp.maximum(a, 0.0)).astype(BF16)
    h2_ref[...] += jnp.dot(act, wd_ref[...], preferred_element_type=F32)

    @pl.when(f == pl.num_programs(1) - 1)
    def _():
        gn_ref[...] = _rms(h2_ref[...], g_ref[...]).astype(BF16)


def _mlp(mn, h, wu, wd, g):
    m, d = h.shape
    dff = wu.shape[1]
    return pl.pallas_call(
        _mlp_kernel,
        grid=(m // MLP_TM, dff // MLP_TF),
        in_specs=[
            pl.BlockSpec((MLP_TM, d), lambda i, f: (i, 0)),
            pl.BlockSpec((MLP_TM, d), lambda i, f: (i, 0)),
            pl.BlockSpec((d, MLP_TF), lambda i, f: (0, f)),
            pl.BlockSpec((MLP_TF, d), lambda i, f: (f, 0)),
            pl.BlockSpec((1, d), lambda i, f: (0, 0)),
        ],
        out_specs=[pl.BlockSpec((MLP_TM, d), lambda i, f: (i, 0)), pl.BlockSpec((MLP_TM, d), lambda i, f: (i, 0))],
        out_shape=[jax.ShapeDtypeStruct((m, d), F32), jax.ShapeDtypeStruct((m, d), BF16)],
        compiler_params=_params("parallel", "arbitrary"),
        name="mlp",
    )(mn, h, wu, wd, g)


GATE_TM = 512


def _gate_final_kernel(gn_ref, wg_ref, p_ref, wp_ref, h_ref, gp_ref, gf_ref, o_ref):
    for r in range(GATE_TM // SUB_ROWS):
        rows = pl.ds(r * SUB_ROWS, SUB_ROWS)
        gate = jax.nn.sigmoid(jnp.dot(gn_ref[rows, :], wg_ref[...], preferred_element_type=F32))
        e = _rms(jnp.dot(p_ref[rows, :].astype(BF16), wp_ref[...], preferred_element_type=F32), gp_ref[...])
        o_ref[rows, :] = _rms(h_ref[rows, :] + gate * e, gf_ref[...])


def _gate_final(gn, wg, p2, wp, h, gp, gf):
    m, d = h.shape
    dp = p2.shape[1]
    return pl.pallas_call(
        _gate_final_kernel,
        grid=(m // GATE_TM,),
        in_specs=[
            pl.BlockSpec((GATE_TM, d), lambda i: (i, 0)),
            pl.BlockSpec((d, d), lambda i: (0, 0)),
            pl.BlockSpec((GATE_TM, dp), lambda i: (i, 0)),
            pl.BlockSpec((dp, d), lambda i: (0, 0)),
            pl.BlockSpec((GATE_TM, d), lambda i: (i, 0)),
            pl.BlockSpec((1, d), lambda i: (0, 0)),
            pl.BlockSpec((1, d), lambda i: (0, 0)),
        ],
        out_specs=pl.BlockSpec((GATE_TM, d), lambda i: (i, 0)),
        out_shape=jax.ShapeDtypeStruct((m, d), F32),
        compiler_params=_params("parallel"),
        name="gate_final",
    )(gn, wg, p2, wp, h, gp, gf)


def _rope_tables(seq):
    rows = seq // GRID_W
    row = np.repeat(np.arange(rows, dtype=np.float32), GRID_W)
    col = np.tile(np.arange(GRID_W, dtype=np.float32), rows)
    half = HEAD_DIM // 2
    inv_freq = np.float32(ROPE_THETA) ** (-np.arange(0, half, 2, dtype=np.float32) / np.float32(half))
    ang_r = row[:, None] * inv_freq
    ang_c = col[:, None] * inv_freq
    cr, sr, cc, sc = np.cos(ang_r), np.sin(ang_r), np.cos(ang_c), np.sin(ang_c)
    cos = np.concatenate([cr, cr, cc, cc], axis=-1).astype(np.float32)
    sin_signed = np.concatenate([-sr, sr, -sc, sc], axis=-1).astype(np.float32)
    return jnp.asarray(cos), jnp.asarray(sin_signed)


def _t5_bucket(rel):
    nb = N_BUCKETS // 2
    ret = jnp.where(rel > 0, nb, 0)
    n = jnp.abs(rel)
    max_exact = nb // 2
    nf = jnp.maximum(n, 1).astype(F32)
    large = max_exact + (jnp.log(nf / max_exact) / math.log(MAX_DISTANCE / max_exact)
                         * (nb - max_exact)).astype(jnp.int32)
    large = jnp.minimum(large, nb - 1)
    return ret + jnp.where(n < max_exact, n, large)


def _window_buckets():
    j = jnp.arange(WA_BAND, dtype=jnp.int32)
    rel = jnp.broadcast_to(j[None, :] - WINDOW, (8, WA_BAND))
    return jnp.where(jnp.abs(rel) <= WINDOW, _t5_bucket(rel), -1).astype(jnp.int32)


def kernel(x, p, attn_norm_g, w_in, q_norm_g, k_norm_g, sink_logits, w_out, mlp_norm_g, w_up, w_down, ple_w,
           ple_norm_g, gate_norm_g, w_gate, rel_bias_table, final_norm_g):
    batch, seq, d = x.shape
    assert w_in.shape[0] == 1, "gate_final fuses the final RMSNorm, which is only valid for a single layer"
    row = lambda v: v.reshape(1, -1).astype(F32)
    cos, sin_signed = _rope_tables(seq)
    h = x.reshape(batch * seq, d)
    proj = _in_proj(h, row(attn_norm_g), w_in[0].astype(BF16), cos, sin_signed, row(q_norm_g), row(k_norm_g), seq)
    later = (w_up[0], w_down[0], w_out[0], w_gate[0])
    oa, casted = _global_attn(proj, [w.reshape(-1, CAST_COLS) for w in later], batch, seq)
    wu, wd, wo, wg = (c.reshape(w.shape) for c, w in zip(casted, later))
    ob = _window_attn(proj, _window_buckets(), rel_bias_table.reshape(-1).astype(F32),
                      sink_logits.reshape(-1).astype(F32), batch, seq)
    h1, mn = _out_proj(oa, ob, wo, h, row(mlp_norm_g))
    h2, gn = _mlp(mn, h1, wu, wd, row(gate_norm_g))
    out = _gate_final(gn, wg, p.reshape(batch * seq, -1), ple_w[0].astype(BF16), h2,
                      row(ple_norm_g), row(final_norm_g))
    return out.reshape(batch, seq, d)
```

```python
import functools
import math

import jax
import jax.numpy as jnp
import numpy as np
from jax import lax
from jax.experimental import pallas as pl
from jax.experimental.pallas import tpu as pltpu

HEAD_DIM = 128
N_HEADS_A = 8
N_KV_A = 2
N_HEADS_B = 8
N_KV_B = 2
GROUP = 4
GRID_W = 64
BLOCK_Q = 128
WINDOW = 128
N_BUCKETS = 32
MAX_DISTANCE = 128
ROPE_THETA = 10000.0
EPS = 1e-6
NEG_INF = -1e30
LOG2_E = math.log2(math.e)
Q_SCALE = HEAD_DIM ** -0.5 * LOG2_E

V7X_VMEM_BYTES = 64 * 1024 * 1024
VMEM_LIMIT_BYTES = 56 * 1024 * 1024

SUB_ROWS = 256

BF16 = jnp.bfloat16
F32 = jnp.float32


def _params(*semantics):
    return pltpu.CompilerParams(dimension_semantics=semantics, vmem_limit_bytes=VMEM_LIMIT_BYTES)


def _rms(x, g):
    return x * lax.rsqrt(jnp.mean(x * x, axis=-1, keepdims=True) + EPS) * g


IN_TM = 512
IN_TN = 512

_PLAIN, _Q_A, _K_A, _Q_B = range(4)
_HEAD_KINDS = ([_Q_A] * N_HEADS_A + [_K_A] * N_KV_A + [_PLAIN] * N_KV_A
               + [_Q_B] * N_HEADS_B + [_PLAIN] * (2 * N_KV_B))


def _rope(y, cos, sin_signed):
    lane = lax.broadcasted_iota(jnp.int32, y.shape, 1)
    partner = jnp.where((lane % 64) < 32, pltpu.roll(y, 96, 1), pltpu.roll(y, 32, 1))
    return y * cos + partner * sin_signed


def _in_proj_kernel(x_ref, g_ref, w_ref, cos_ref, sin_ref, gq_ref, gk_ref, o_ref):
    heads_per_dot = IN_TN // HEAD_DIM
    for r in range(IN_TM // SUB_ROWS):
        rows = pl.ds(r * SUB_ROWS, SUB_ROWS)
        u = _rms(x_ref[rows, :], g_ref[...]).astype(BF16)
        cos, sin_signed = cos_ref[rows, :], sin_ref[rows, :]
        for c in range(w_ref.shape[1] // IN_TN):
            acc = jnp.dot(u, w_ref[:, c * IN_TN:(c + 1) * IN_TN], preferred_element_type=F32)
            for hh in range(heads_per_dot):
                head = c * heads_per_dot + hh
                a = acc[:, hh * HEAD_DIM:(hh + 1) * HEAD_DIM]
                kind = _HEAD_KINDS[head]
                if kind == _Q_A:
                    a = _rope(_rms(a, gq_ref[...]), cos, sin_signed) * Q_SCALE
                elif kind == _K_A:
                    a = _rope(_rms(a, gk_ref[...]), cos, sin_signed)
                elif kind == _Q_B:
                    a = a * Q_SCALE
                o_ref[rows, head * HEAD_DIM:(head + 1) * HEAD_DIM] = a.astype(BF16)


def _in_proj(x2, g, w, cos, sin_signed, gq, gk, seq):
    m, d = x2.shape
    n = w.shape[1]
    assert n == len(_HEAD_KINDS) * HEAD_DIM
    pos_tiles = seq // IN_TM
    return pl.pallas_call(
        _in_proj_kernel,
        grid=(m // IN_TM,),
        in_specs=[
            pl.BlockSpec((IN_TM, d), lambda i: (i, 0)),
            pl.BlockSpec((1, d), lambda i: (0, 0)),
            pl.BlockSpec((d, n), lambda i: (0, 0)),
            pl.BlockSpec((IN_TM, HEAD_DIM), lambda i: (i % pos_tiles, 0)),
            pl.BlockSpec((IN_TM, HEAD_DIM), lambda i: (i % pos_tiles, 0)),
            pl.BlockSpec((1, HEAD_DIM), lambda i: (0, 0)),
            pl.BlockSpec((1, HEAD_DIM), lambda i: (0, 0)),
        ],
        out_specs=pl.BlockSpec((IN_TM, n), lambda i: (i, 0)),
        out_shape=jax.ShapeDtypeStruct((m, n), BF16),
        compiler_params=_params("parallel"),
        name="in_proj",
    )(x2, g, w, cos, sin_signed, gq, gk)


GA_ROWS = 256


def _global_attn_kernel(*refs, n_cast):
    q_ref, k_ref, v_ref = refs[:3]
    w_f32 = refs[3:3 + n_cast]
    o_ref = refs[3 + n_cast]
    w_bf16 = refs[4 + n_cast:4 + 2 * n_cast]
    v1_ref, s_ref, m_ref = refs[4 + 2 * n_cast:7 + 2 * n_cast]
    in_bufs = refs[7 + 2 * n_cast:7 + 3 * n_cast]
    out_bufs = refs[7 + 3 * n_cast:7 + 4 * n_cast]
    in_sem, out_sem = refs[7 + 4 * n_cast:]

    seq = q_ref.shape[0]
    n_blocks = seq // GA_ROWS
    first_chunk = (pl.program_id(0) * pl.num_programs(1) + pl.program_id(1)) * n_blocks

    def in_copy(w, chunk, slot):
        rows = in_bufs[w].shape[1]
        src = w_f32[w].at[0, pl.ds(pl.multiple_of(chunk * rows, rows), rows)]
        return pltpu.make_async_copy(src, in_bufs[w].at[slot], in_sem.at[w, slot])

    def out_copy(w, chunk):
        rows = out_bufs[w].shape[0]
        dst = w_bf16[w].at[pl.ds(pl.multiple_of(chunk * rows, rows), rows)]
        return pltpu.make_async_copy(out_bufs[w], dst, out_sem.at[w])

    def cast_chunk(r, first, last):
        chunk = first_chunk + r
        slot = r % 2
        for w in range(n_cast):
            in_copy(w, chunk, slot).wait()
            if not last:
                in_copy(w, chunk + 1, 1 - slot).start()
            if not first:
                out_copy(w, chunk - 1).wait()
            out_bufs[w][...] = in_bufs[w][slot].astype(BF16)
            out_copy(w, chunk).start()

    for w in range(n_cast):
        in_copy(w, first_chunk, 0).start()

    v1_ref[:, :HEAD_DIM] = v_ref[...]
    v1_ref[:, HEAD_DIM:] = jnp.ones(v_ref.shape, BF16)

    def rows_of(r):
        return pl.ds(pl.multiple_of(r * GA_ROWS, GA_ROWS), GA_ROWS)

    def scores(r, g, slot):
        q = q_ref[rows_of(r), g * HEAD_DIM:(g + 1) * HEAD_DIM]
        s = lax.dot_general(q, k_ref[...], (((1,), (1,)), ((), ())), preferred_element_type=F32)
        s_ref[slot] = s
        m_ref[slot] = jnp.max(s, axis=-1, keepdims=True)

    def apply(r, g, slot):
        p = jnp.exp2(s_ref[slot] - m_ref[slot]).astype(BF16)
        o = jnp.dot(p, v1_ref[...], preferred_element_type=F32)
        o_ref[rows_of(r), g * HEAD_DIM:(g + 1) * HEAD_DIM] = (o[:, :HEAD_DIM] / o[:, HEAD_DIM:]).astype(BF16)

    def row_block(r, first, last):
        for g in range(GROUP):
            slot = g % 2
            if g + 1 < GROUP:
                scores(r, g + 1, 1 - slot)
            elif not last:
                scores(r + 1, 0, 1 - slot)
            apply(r, g, slot)
        cast_chunk(r, first, last)

    scores(0, 0, 0)
    row_block(0, first=True, last=False)

    def body(r, carry):
        row_block(r, first=False, last=False)
        return carry

    lax.fori_loop(1, n_blocks - 1, body, 0)
    row_block(n_blocks - 1, first=False, last=True)
    for w in range(n_cast):
        out_copy(w, first_chunk + n_blocks - 1).wait()


def _global_attn(proj, weights, batch, seq):
    assert GROUP % 2 == 0, "score slots alternate per head and must line up across row blocks"
    n_blocks = seq // GA_ROWS
    assert n_blocks >= 3 and n_blocks % 2 == 0
    n_chunks = batch * N_KV_A * n_blocks
    chunk_rows = []
    for w in weights:
        assert w.shape[0] == 1 and w.shape[1] % (16 * n_chunks) == 0, w.shape
        chunk_rows.append(w.shape[1] // n_chunks)
    n_cast = len(weights)
    k_col = N_HEADS_A
    v_col = N_HEADS_A + N_KV_A
    any_spec = pl.BlockSpec(memory_space=pl.ANY)
    attn_spec = pl.BlockSpec((seq, GROUP * HEAD_DIM), lambda b, k: (b, k))
    outs = pl.pallas_call(
        functools.partial(_global_attn_kernel, n_cast=n_cast),
        grid=(batch, N_KV_A),
        in_specs=[
            attn_spec,
            pl.BlockSpec((seq, HEAD_DIM), lambda b, k: (b, k_col + k)),
            pl.BlockSpec((seq, HEAD_DIM), lambda b, k: (b, v_col + k)),
        ] + [any_spec] * n_cast,
        out_specs=[attn_spec] + [any_spec] * n_cast,
        out_shape=[jax.ShapeDtypeStruct((batch * seq, N_HEADS_A * HEAD_DIM), BF16)]
        + [jax.ShapeDtypeStruct(w.shape[1:], BF16) for w in weights],
        scratch_shapes=[
            pltpu.VMEM((seq, 2 * HEAD_DIM), BF16),
            pltpu.VMEM((2, GA_ROWS, seq), F32),
            pltpu.VMEM((2, GA_ROWS, 1), F32),
        ] + [pltpu.VMEM((2, rows, w.shape[2]), F32) for rows, w in zip(chunk_rows, weights)]
        + [pltpu.VMEM((rows, w.shape[2]), BF16) for rows, w in zip(chunk_rows, weights)]
        + [pltpu.SemaphoreType.DMA((n_cast, 2)), pltpu.SemaphoreType.DMA((n_cast,))],
        compiler_params=_params("arbitrary", "arbitrary"),
        name="global_attn",
    )(proj, proj, proj, *weights)
    return outs[0], outs[1:]


WA_ROWS = 256
WA_BAND = WA_ROWS + 2 * WINDOW


def _window_attn_kernel(sink_ref, table_ref, q_ref, k_ref, v_ref, bucket_ref, o_ref,
                        kpad_ref, v1pad_ref, bias_ref, sink_ref_b, s_ref, m_ref):
    kvh = pl.program_id(1)
    seq = q_ref.shape[0]
    n_stages = seq // WA_ROWS

    zeros = jnp.zeros((WINDOW, 2 * HEAD_DIM), BF16)
    kpad_ref[:WINDOW, :] = zeros[:, :HEAD_DIM]
    kpad_ref[WINDOW + seq:, :] = zeros[:, :HEAD_DIM]
    kpad_ref[WINDOW:WINDOW + seq, :] = k_ref[...]
    v1pad_ref[:WINDOW, :] = zeros
    v1pad_ref[WINDOW + seq:, :] = zeros
    v1pad_ref[WINDOW:WINDOW + seq, :HEAD_DIM] = v_ref[...]
    v1pad_ref[WINDOW:WINDOW + seq, HEAD_DIM:] = jnp.ones(v_ref.shape, BF16)

    bucket = bucket_ref[...]
    for g in range(GROUP):
        head = kvh * GROUP + g
        line = jnp.full(bucket.shape, NEG_INF, F32)
        for b in range(N_BUCKETS):
            line = jnp.where(bucket == b, table_ref[b * N_HEADS_B + head] * LOG2_E, line)
        tile = jnp.broadcast_to(line[:1], (WA_ROWS, WA_BAND))
        bias_ref[g * WA_ROWS:(g + 1) * WA_ROWS, :] = pltpu.roll(tile, 0, 1, stride=1, stride_axis=0)
        sink_ref_b[g * WA_ROWS:(g + 1) * WA_ROWS, :] = jnp.full((WA_ROWS, HEAD_DIM), sink_ref[head] * LOG2_E, F32)

    col = lax.broadcasted_iota(jnp.int32, (GROUP * WA_ROWS, WA_BAND), 1)

    def rows_of(n):
        return pl.ds(pl.multiple_of(n * WA_ROWS, WA_ROWS), WA_ROWS)

    def band_of(n):
        return pl.ds(pl.multiple_of(n * WA_ROWS, WA_ROWS), WA_BAND)

    def scores(n, slot):
        q = jnp.concatenate([q_ref[rows_of(n), g * HEAD_DIM:(g + 1) * HEAD_DIM] for g in range(GROUP)], axis=0)
        s = lax.dot_general(q, kpad_ref[band_of(n), :], (((1,), (1,)), ((), ())), preferred_element_type=F32)
        in_range = ((col >= WINDOW) | (n > 0)) & ((col < WA_BAND - WINDOW) | (n < n_stages - 1))
        s = jnp.where(in_range, s + bias_ref[...], NEG_INF)
        s_ref[slot] = s
        row_max = jnp.broadcast_to(jnp.max(s, axis=-1, keepdims=True), sink_ref_b.shape)
        m_ref[slot] = jnp.maximum(row_max, sink_ref_b[...])

    def apply(n, slot):
        m = m_ref[slot]
        p = jnp.concatenate([jnp.exp2(s_ref[slot, :, c * HEAD_DIM:(c + 1) * HEAD_DIM] - m)
                             for c in range(WA_BAND // HEAD_DIM)], axis=1).astype(BF16)
        o = jnp.dot(p, v1pad_ref[band_of(n), :], preferred_element_type=F32)
        o = o[:, :HEAD_DIM] / (o[:, HEAD_DIM:] + jnp.exp2(sink_ref_b[...] - m))
        for g in range(GROUP):
            o_ref[rows_of(n), g * HEAD_DIM:(g + 1) * HEAD_DIM] = o[g * WA_ROWS:(g + 1) * WA_ROWS].astype(BF16)

    scores(0, 0)

    def body(i, carry):
        for slot in range(2):
            n = 2 * i + slot
            scores(n + 1, 1 - slot)
            apply(n, slot)
        return carry

    lax.fori_loop(0, n_stages // 2 - 1, body, 0)
    scores(n_stages - 1, 1)
    apply(n_stages - 2, 0)
    apply(n_stages - 1, 1)


def _window_attn(proj, bucket, table, sink, batch, seq):
    assert (seq // WA_ROWS) % 2 == 0, "the stage loop is unrolled by two so score slots are static"
    q_col = (N_HEADS_A + 2 * N_KV_A) // GROUP
    k_col = N_HEADS_A + 2 * N_KV_A + N_HEADS_B
    v_col = k_col + N_KV_B
    grid_spec = pltpu.PrefetchScalarGridSpec(
        num_scalar_prefetch=2,
        grid=(batch, N_KV_B),
        in_specs=[
            pl.BlockSpec((seq, GROUP * HEAD_DIM), lambda b, k, s, t: (b, q_col + k)),
            pl.BlockSpec((seq, HEAD_DIM), lambda b, k, s, t: (b, k_col + k)),
            pl.BlockSpec((seq, HEAD_DIM), lambda b, k, s, t: (b, v_col + k)),
            pl.BlockSpec((8, WA_BAND), lambda b, k, s, t: (0, 0)),
        ],
        out_specs=pl.BlockSpec((seq, GROUP * HEAD_DIM), lambda b, k, s, t: (b, k)),
        scratch_shapes=[
            pltpu.VMEM((seq + 2 * WINDOW, HEAD_DIM), BF16),
            pltpu.VMEM((seq + 2 * WINDOW, 2 * HEAD_DIM), BF16),
            pltpu.VMEM((GROUP * WA_ROWS, WA_BAND), F32),
            pltpu.VMEM((GROUP * WA_ROWS, HEAD_DIM), F32),
            pltpu.VMEM((2, GROUP * WA_ROWS, WA_BAND), F32),
            pltpu.VMEM((2, GROUP * WA_ROWS, HEAD_DIM), F32),
        ],
    )
    return pl.pallas_call(
        _window_attn_kernel,
        grid_spec=grid_spec,
        out_shape=jax.ShapeDtypeStruct((batch * seq, N_HEADS_B * HEAD_DIM), BF16),
        compiler_params=_params("parallel", "parallel"),
        name="window_attn",
    )(sink, table, proj, proj, proj, bucket)


OUT_TM = 512


def _out_proj_kernel(oa_ref, ob_ref, w_ref, x_ref, g_ref, h_ref, m_ref):
    ka = oa_ref.shape[1]
    for r in range(OUT_TM // SUB_ROWS):
        rows = pl.ds(r * SUB_ROWS, SUB_ROWS)
        h = x_ref[rows, :]
        h = h + jnp.dot(oa_ref[rows, :], w_ref[:ka, :], preferred_element_type=F32)
        h = h + jnp.dot(ob_ref[rows, :], w_ref[ka:, :], preferred_element_type=F32)
        h_ref[rows, :] = h
        m_ref[rows, :] = _rms(h, g_ref[...]).astype(BF16)


def _out_proj(oa, ob, w, x2, g):
    m, d = x2.shape
    ka, kb = oa.shape[1], ob.shape[1]
    return pl.pallas_call(
        _out_proj_kernel,
        grid=(m // OUT_TM,),
        in_specs=[
            pl.BlockSpec((OUT_TM, ka), lambda i: (i, 0)),
            pl.BlockSpec((OUT_TM, kb), lambda i: (i, 0)),
            pl.BlockSpec((ka + kb, d), lambda i: (0, 0)),
            pl.BlockSpec((OUT_TM, d), lambda i: (i, 0)),
            pl.BlockSpec((1, d), lambda i: (0, 0)),
        ],
        out_specs=[pl.BlockSpec((OUT_TM, d), lambda i: (i, 0)), pl.BlockSpec((OUT_TM, d), lambda i: (i, 0))],
        out_shape=[jax.ShapeDtypeStruct((m, d), F32), jax.ShapeDtypeStruct((m, d), BF16)],
        compiler_params=_params("parallel"),
        name="out_proj",
    )(oa, ob, w, x2, g)


MLP_TM = 512
MLP_TF = 1024


def _mlp_kernel(m_ref, h_ref, wu_ref, wd_ref, g_ref, h2_ref, gn_ref):
    f = pl.program_id(1)

    @pl.when(f == 0)
    def _():
        h2_ref[...] = h_ref[...]

    a = jnp.dot(m_ref[...], wu_ref[...], preferred_element_type=F32)
    act = jnp.square(jnp.maximum(a, 0.0)).astype(BF16)
    h2_ref[...] += jnp.dot(act, wd_ref[...], preferred_element_type=F32)

    @pl.when(f == pl.num_programs(1) - 1)
    def _():
        gn_ref[...] = _rms(h2_ref[...], g_ref[...]).astype(BF16)


def _mlp(mn, h, wu, wd, g):
    m, d = h.shape
    dff = wu.shape[1]
    return pl.pallas_call(
        _mlp_kernel,
        grid=(m // MLP_TM, dff // MLP_TF),
        in_specs=[
            pl.BlockSpec((MLP_TM, d), lambda i, f: (i, 0)),
            pl.BlockSpec((MLP_TM, d), lambda i, f: (i, 0)),
            pl.BlockSpec((d, MLP_TF), lambda i, f: (0, f)),
            pl.BlockSpec((MLP_TF, d), lambda i, f: (f, 0)),
            pl.BlockSpec((1, d), lambda i, f: (0, 0)),
        ],
        out_specs=[pl.BlockSpec((MLP_TM, d), lambda i, f: (i, 0)), pl.BlockSpec((MLP_TM, d), lambda i, f: (i, 0))],
        out_shape=[jax.ShapeDtypeStruct((m, d), F32), jax.ShapeDtypeStruct((m, d), BF16)],
        compiler_params=_params("parallel", "arbitrary"),
        name="mlp",
    )(mn, h, wu, wd, g)


GATE_TM = 512


def _gate_final_kernel(gn_ref, wg_ref, p_ref, wp_ref, h_ref, gp_ref, gf_ref, o_ref):
    for r in range(GATE_TM // SUB_ROWS):
        rows = pl.ds(r * SUB_ROWS, SUB_ROWS)
        gate = jax.nn.sigmoid(jnp.dot(gn_ref[rows, :], wg_ref[...], preferred_element_type=F32))
        e = _rms(jnp.dot(p_ref[rows, :].astype(BF16), wp_ref[...], preferred_element_type=F32), gp_ref[...])
        o_ref[rows, :] = _rms(h_ref[rows, :] + gate * e, gf_ref[...])


def _gate_final(gn, wg, p2, wp, h, gp, gf):
    m, d = h.shape
    dp = p2.shape[1]
    return pl.pallas_call(
        _gate_final_kernel,
        grid=(m // GATE_TM,),
        in_specs=[
            pl.BlockSpec((GATE_TM, d), lambda i: (i, 0)),
            pl.BlockSpec((d, d), lambda i: (0, 0)),
            pl.BlockSpec((GATE_TM, dp), lambda i: (i, 0)),
            pl.BlockSpec((dp, d), lambda i: (0, 0)),
            pl.BlockSpec((GATE_TM, d), lambda i: (i, 0)),
            pl.BlockSpec((1, d), lambda i: (0, 0)),
            pl.BlockSpec((1, d), lambda i: (0, 0)),
        ],
        out_specs=pl.BlockSpec((GATE_TM, d), lambda i: (i, 0)),
        out_shape=jax.ShapeDtypeStruct((m, d), F32),
        compiler_params=_params("parallel"),
        name="gate_final",
    )(gn, wg, p2, wp, h, gp, gf)


def _rope_tables(seq):
    rows = seq // GRID_W
    row = np.repeat(np.arange(rows, dtype=np.float32), GRID_W)
    col = np.tile(np.arange(GRID_W, dtype=np.float32), rows)
    half = HEAD_DIM // 2
    inv_freq = np.float32(ROPE_THETA) ** (-np.arange(0, half, 2, dtype=np.float32) / np.float32(half))
    ang_r = row[:, None] * inv_freq
    ang_c = col[:, None] * inv_freq
    cr, sr, cc, sc = np.cos(ang_r), np.sin(ang_r), np.cos(ang_c), np.sin(ang_c)
    cos = np.concatenate([cr, cr, cc, cc], axis=-1).astype(np.float32)
    sin_signed = np.concatenate([-sr, sr, -sc, sc], axis=-1).astype(np.float32)
    return jnp.asarray(cos), jnp.asarray(sin_signed)


def _t5_bucket(rel):
    nb = N_BUCKETS // 2
    ret = jnp.where(rel > 0, nb, 0)
    n = jnp.abs(rel)
    max_exact = nb // 2
    nf = jnp.maximum(n, 1).astype(F32)
    large = max_exact + (jnp.log(nf / max_exact) / math.log(MAX_DISTANCE / max_exact)
                         * (nb - max_exact)).astype(jnp.int32)
    large = jnp.minimum(large, nb - 1)
    return ret + jnp.where(n < max_exact, n, large)


def _window_buckets():
    j = jnp.arange(WA_BAND, dtype=jnp.int32)
    rel = jnp.broadcast_to(j[None, :] - WINDOW, (8, WA_BAND))
    return jnp.where(jnp.abs(rel) <= WINDOW, _t5_bucket(rel), -1).astype(jnp.int32)


def kernel(x, p, attn_norm_g, w_in, q_norm_g, k_norm_g, sink_logits, w_out, mlp_norm_g, w_up, w_down, ple_w,
           ple_norm_g, gate_norm_g, w_gate, rel_bias_table, final_norm_g):
    batch, seq, d = x.shape
    assert w_in.shape[0] == 1, "gate_final fuses the final RMSNorm, which is only valid for a single layer"
    row = lambda v: v.reshape(1, -1).astype(F32)
    cos, sin_signed = _rope_tables(seq)
    h = x.reshape(batch * seq, d)
    proj = _in_proj(h, row(attn_norm_g), w_in[0].astype(BF16), cos, sin_signed, row(q_norm_g), row(k_norm_g), seq)
    oa, (wu, wd, wo, wg) = _global_attn(proj, (w_up, w_down, w_out, w_gate), batch, seq)
    ob = _window_attn(proj, _window_buckets(), rel_bias_table.reshape(-1).astype(F32),
                      sink_logits.reshape(-1).astype(F32), batch, seq)
    h1, mn = _out_proj(oa, ob, wo, h, row(mlp_norm_g))
    h2, gn = _mlp(mn, h1, wu, wd, row(gate_norm_g))
    out = _gate_final(gn, wg, p.reshape(batch * seq, -1), ple_w[0].astype(BF16), h2,
                      row(ple_norm_g), row(final_norm_g))
    return out.reshape(batch, seq, d)
```

```python
import functools
import math

import jax
import jax.numpy as jnp
import numpy as np
from jax import lax
from jax.experimental import pallas as pl
from jax.experimental.pallas import tpu as pltpu

HEAD_DIM = 128
N_HEADS_A = 8
N_KV_A = 2
N_HEADS_B = 8
N_KV_B = 2
GROUP = 4
GRID_W = 64
WINDOW = 128
N_BUCKETS = 32
MAX_DISTANCE = 128
ROPE_THETA = 10000.0
EPS = 1e-6
NEG_INF = -1e30
LOG2_E = math.log2(math.e)
Q_SCALE = HEAD_DIM ** -0.5 * LOG2_E

V7X_VMEM_BYTES = 64 * 1024 * 1024
VMEM_RESERVE_BYTES = 8 * 1024 * 1024
VMEM_LIMIT_BYTES = V7X_VMEM_BYTES - VMEM_RESERVE_BYTES
F32_SUBLANES = 8
BF16_SUBLANES = 16

SUB_ROWS = 256

BF16 = jnp.bfloat16
F32 = jnp.float32


def _params(*semantics):
    return pltpu.CompilerParams(dimension_semantics=semantics, vmem_limit_bytes=VMEM_LIMIT_BYTES)


def _rms(x, g):
    return x * lax.rsqrt(jnp.mean(x * x, axis=-1, keepdims=True) + EPS) * g


IN_TM = 512
IN_TN = 512

_PLAIN, _Q_A, _K_A, _Q_B = range(4)
_HEAD_KINDS = ([_Q_A] * N_HEADS_A + [_K_A] * N_KV_A + [_PLAIN] * N_KV_A
               + [_Q_B] * N_HEADS_B + [_PLAIN] * (2 * N_KV_B))


def _rope(y, cos, sin_signed):
    lane = lax.broadcasted_iota(jnp.int32, y.shape, 1)
    partner = jnp.where((lane % 64) < 32, pltpu.roll(y, 96, 1), pltpu.roll(y, 32, 1))
    return y * cos + partner * sin_signed


def _in_proj_kernel(x_ref, g_ref, w_ref, cos_ref, sin_ref, gq_ref, gk_ref, o_ref):
    heads_per_dot = IN_TN // HEAD_DIM
    for r in range(IN_TM // SUB_ROWS):
        rows = pl.ds(r * SUB_ROWS, SUB_ROWS)
        u = _rms(x_ref[rows, :], g_ref[...]).astype(BF16)
        cos, sin_signed = cos_ref[rows, :], sin_ref[rows, :]
        for c in range(w_ref.shape[1] // IN_TN):
            acc = jnp.dot(u, w_ref[:, c * IN_TN:(c + 1) * IN_TN], preferred_element_type=F32)
            for hh in range(heads_per_dot):
                head = c * heads_per_dot + hh
                a = acc[:, hh * HEAD_DIM:(hh + 1) * HEAD_DIM]
                kind = _HEAD_KINDS[head]
                if kind == _Q_A:
                    a = _rope(_rms(a, gq_ref[...]), cos, sin_signed) * Q_SCALE
                elif kind == _K_A:
                    a = _rope(_rms(a, gk_ref[...]), cos, sin_signed)
                elif kind == _Q_B:
                    a = a * Q_SCALE
                o_ref[rows, head * HEAD_DIM:(head + 1) * HEAD_DIM] = a.astype(BF16)


def _in_proj(x2, g, w, cos, sin_signed, gq, gk, seq):
    m, d = x2.shape
    n = w.shape[1]
    assert n == len(_HEAD_KINDS) * HEAD_DIM
    pos_tiles = seq // IN_TM
    return pl.pallas_call(
        _in_proj_kernel,
        grid=(m // IN_TM,),
        in_specs=[
            pl.BlockSpec((IN_TM, d), lambda i: (i, 0)),
            pl.BlockSpec((1, d), lambda i: (0, 0)),
            pl.BlockSpec((d, n), lambda i: (0, 0)),
            pl.BlockSpec((IN_TM, HEAD_DIM), lambda i: (i % pos_tiles, 0)),
            pl.BlockSpec((IN_TM, HEAD_DIM), lambda i: (i % pos_tiles, 0)),
            pl.BlockSpec((1, HEAD_DIM), lambda i: (0, 0)),
            pl.BlockSpec((1, HEAD_DIM), lambda i: (0, 0)),
        ],
        out_specs=pl.BlockSpec((IN_TM, n), lambda i: (i, 0)),
        out_shape=jax.ShapeDtypeStruct((m, n), BF16),
        compiler_params=_params("parallel"),
        name="in_proj",
    )(x2, g, w, cos, sin_signed, gq, gk)


GA_ROWS = 256


def _global_attn_kernel(*refs, n_cast):
    q_ref, k_ref, v_ref = refs[:3]
    w_f32 = refs[3:3 + n_cast]
    o_ref = refs[3 + n_cast]
    w_bf16 = refs[4 + n_cast:4 + 2 * n_cast]
    v1_ref, s_ref, m_ref = refs[4 + 2 * n_cast:7 + 2 * n_cast]
    in_bufs = refs[7 + 2 * n_cast:7 + 3 * n_cast]
    out_bufs = refs[7 + 3 * n_cast:7 + 4 * n_cast]
    in_sem, out_sem = refs[7 + 4 * n_cast:]

    seq = q_ref.shape[0]
    n_blocks = seq // GA_ROWS
    first_chunk = (pl.program_id(0) * pl.num_programs(1) + pl.program_id(1)) * n_blocks

    def in_copy(w, chunk, slot):
        rows = in_bufs[w].shape[1]
        src = w_f32[w].at[0, pl.ds(pl.multiple_of(chunk * rows, rows), rows)]
        return pltpu.make_async_copy(src, in_bufs[w].at[slot], in_sem.at[w, slot])

    def out_copy(w, chunk):
        rows = out_bufs[w].shape[0]
        dst = w_bf16[w].at[pl.ds(pl.multiple_of(chunk * rows, rows), rows)]
        return pltpu.make_async_copy(out_bufs[w], dst, out_sem.at[w])

    def cast_chunk(r, first, last):
        chunk = first_chunk + r
        slot = r % 2
        for w in range(n_cast):
            in_copy(w, chunk, slot).wait()
            if not last:
                in_copy(w, chunk + 1, 1 - slot).start()
            if not first:
                out_copy(w, chunk - 1).wait()
            out_bufs[w][...] = in_bufs[w][slot].astype(BF16)
            out_copy(w, chunk).start()

    for w in range(n_cast):
        in_copy(w, first_chunk, 0).start()

    v1_ref[:, :HEAD_DIM] = v_ref[...]
    v1_ref[:, HEAD_DIM:] = jnp.ones(v_ref.shape, BF16)

    def rows_of(r):
        return pl.ds(pl.multiple_of(r * GA_ROWS, GA_ROWS), GA_ROWS)

    def scores(r, g, slot):
        q = q_ref[rows_of(r), g * HEAD_DIM:(g + 1) * HEAD_DIM]
        s = lax.dot_general(q, k_ref[...], (((1,), (1,)), ((), ())), preferred_element_type=F32)
        s_ref[slot] = s
        m_ref[slot] = jnp.max(s, axis=-1, keepdims=True)

    def apply(r, g, slot):
        p = jnp.exp2(s_ref[slot] - m_ref[slot]).astype(BF16)
        o = jnp.dot(p, v1_ref[...], preferred_element_type=F32)
        o_ref[rows_of(r), g * HEAD_DIM:(g + 1) * HEAD_DIM] = (o[:, :HEAD_DIM] / o[:, HEAD_DIM:]).astype(BF16)

    def row_block(r, first, last):
        for g in range(GROUP):
            slot = g % 2
            if g + 1 < GROUP:
                scores(r, g + 1, 1 - slot)
            elif not last:
                scores(r + 1, 0, 1 - slot)
            apply(r, g, slot)
        cast_chunk(r, first, last)

    scores(0, 0, 0)
    row_block(0, first=True, last=False)

    def body(r, carry):
        row_block(r, first=False, last=False)
        return carry

    lax.fori_loop(1, n_blocks - 1, body, 0)
    row_block(n_blocks - 1, first=False, last=True)
    for w in range(n_cast):
        out_copy(w, first_chunk + n_blocks - 1).wait()


def _global_attn(proj, weights, batch, seq):
    assert GROUP % 2 == 0, "score slots alternate per head and must line up across row blocks"
    n_blocks = seq // GA_ROWS
    assert n_blocks >= 3 and n_blocks % 2 == 0
    n_chunks = batch * N_KV_A * n_blocks
    chunk_rows = []
    for w in weights:
        assert w.shape[0] == 1 and w.shape[1] % (BF16_SUBLANES * n_chunks) == 0, w.shape
        chunk_rows.append(w.shape[1] // n_chunks)
    n_cast = len(weights)
    k_col = N_HEADS_A
    v_col = N_HEADS_A + N_KV_A
    any_spec = pl.BlockSpec(memory_space=pl.ANY)
    attn_spec = pl.BlockSpec((seq, GROUP * HEAD_DIM), lambda b, k: (b, k))
    outs = pl.pallas_call(
        functools.partial(_global_attn_kernel, n_cast=n_cast),
        grid=(batch, N_KV_A),
        in_specs=[
            attn_spec,
            pl.BlockSpec((seq, HEAD_DIM), lambda b, k: (b, k_col + k)),
            pl.BlockSpec((seq, HEAD_DIM), lambda b, k: (b, v_col + k)),
        ] + [any_spec] * n_cast,
        out_specs=[attn_spec] + [any_spec] * n_cast,
        out_shape=[jax.ShapeDtypeStruct((batch * seq, N_HEADS_A * HEAD_DIM), BF16)]
        + [jax.ShapeDtypeStruct(w.shape[1:], BF16) for w in weights],
        scratch_shapes=[
            pltpu.VMEM((seq, 2 * HEAD_DIM), BF16),
            pltpu.VMEM((2, GA_ROWS, seq), F32),
            pltpu.VMEM((2, GA_ROWS, 1), F32),
        ] + [pltpu.VMEM((2, rows, w.shape[2]), F32) for rows, w in zip(chunk_rows, weights)]
        + [pltpu.VMEM((rows, w.shape[2]), BF16) for rows, w in zip(chunk_rows, weights)]
        + [pltpu.SemaphoreType.DMA((n_cast, 2)), pltpu.SemaphoreType.DMA((n_cast,))],
        compiler_params=_params("arbitrary", "arbitrary"),
        name="global_attn",
    )(proj, proj, proj, *weights)
    return outs[0], outs[1:]


WA_ROWS = 256
WA_BAND = WA_ROWS + 2 * WINDOW


def _window_attn_kernel(sink_ref, table_ref, q_ref, k_ref, v_ref, bucket_ref, o_ref,
                        kpad_ref, v1pad_ref, bias_ref, sink_ref_b, s_ref, m_ref):
    kvh = pl.program_id(1)
    seq = q_ref.shape[0]
    n_stages = seq // WA_ROWS

    zeros = jnp.zeros((WINDOW, 2 * HEAD_DIM), BF16)
    kpad_ref[:WINDOW, :] = zeros[:, :HEAD_DIM]
    kpad_ref[WINDOW + seq:, :] = zeros[:, :HEAD_DIM]
    kpad_ref[WINDOW:WINDOW + seq, :] = k_ref[...]
    v1pad_ref[:WINDOW, :] = zeros
    v1pad_ref[WINDOW + seq:, :] = zeros
    v1pad_ref[WINDOW:WINDOW + seq, :HEAD_DIM] = v_ref[...]
    v1pad_ref[WINDOW:WINDOW + seq, HEAD_DIM:] = jnp.ones(v_ref.shape, BF16)

    bucket = bucket_ref[...]
    for g in range(GROUP):
        head = kvh * GROUP + g
        line = jnp.full(bucket.shape, NEG_INF, F32)
        for b in range(N_BUCKETS):
            line = jnp.where(bucket == b, table_ref[b * N_HEADS_B + head] * LOG2_E, line)
        tile = jnp.broadcast_to(line[:1], (WA_ROWS, WA_BAND))
        bias_ref[g * WA_ROWS:(g + 1) * WA_ROWS, :] = pltpu.roll(tile, 0, 1, stride=1, stride_axis=0)
        sink_ref_b[g * WA_ROWS:(g + 1) * WA_ROWS, :] = jnp.full((WA_ROWS, HEAD_DIM), sink_ref[head] * LOG2_E, F32)

    col = lax.broadcasted_iota(jnp.int32, (GROUP * WA_ROWS, WA_BAND), 1)

    def rows_of(n):
        return pl.ds(pl.multiple_of(n * WA_ROWS, WA_ROWS), WA_ROWS)

    def band_of(n):
        return pl.ds(pl.multiple_of(n * WA_ROWS, WA_ROWS), WA_BAND)

    def scores(n, slot):
        q = jnp.concatenate([q_ref[rows_of(n), g * HEAD_DIM:(g + 1) * HEAD_DIM] for g in range(GROUP)], axis=0)
        s = lax.dot_general(q, kpad_ref[band_of(n), :], (((1,), (1,)), ((), ())), preferred_element_type=F32)
        in_range = ((col >= WINDOW) | (n > 0)) & ((col < WA_BAND - WINDOW) | (n < n_stages - 1))
        s = jnp.where(in_range, s + bias_ref[...], NEG_INF)
        s_ref[slot] = s
        row_max = jnp.broadcast_to(jnp.max(s, axis=-1, keepdims=True), sink_ref_b.shape)
        m_ref[slot] = jnp.maximum(row_max, sink_ref_b[...])

    def apply(n, slot):
        m = m_ref[slot]
        p = jnp.concatenate([jnp.exp2(s_ref[slot, :, c * HEAD_DIM:(c + 1) * HEAD_DIM] - m)
                             for c in range(WA_BAND // HEAD_DIM)], axis=1).astype(BF16)
        o = jnp.dot(p, v1pad_ref[band_of(n), :], preferred_element_type=F32)
        o = o[:, :HEAD_DIM] / (o[:, HEAD_DIM:] + jnp.exp2(sink_ref_b[...] - m))
        for g in range(GROUP):
            o_ref[rows_of(n), g * HEAD_DIM:(g + 1) * HEAD_DIM] = o[g * WA_ROWS:(g + 1) * WA_ROWS].astype(BF16)

    scores(0, 0)

    def body(i, carry):
        for slot in range(2):
            n = 2 * i + slot
            scores(n + 1, 1 - slot)
            apply(n, slot)
        return carry

    lax.fori_loop(0, n_stages // 2 - 1, body, 0)
    scores(n_stages - 1, 1)
    apply(n_stages - 2, 0)
    apply(n_stages - 1, 1)


def _window_attn(proj, bucket, table, sink, batch, seq):
    assert (seq // WA_ROWS) % 2 == 0, "the stage loop is unrolled by two so score slots are static"
    q_col = (N_HEADS_A + 2 * N_KV_A) // GROUP
    k_col = N_HEADS_A + 2 * N_KV_A + N_HEADS_B
    v_col = k_col + N_KV_B
    grid_spec = pltpu.PrefetchScalarGridSpec(
        num_scalar_prefetch=2,
        grid=(batch, N_KV_B),
        in_specs=[
            pl.BlockSpec((seq, GROUP * HEAD_DIM), lambda b, k, s, t: (b, q_col + k)),
            pl.BlockSpec((seq, HEAD_DIM), lambda b, k, s, t: (b, k_col + k)),
            pl.BlockSpec((seq, HEAD_DIM), lambda b, k, s, t: (b, v_col + k)),
            pl.BlockSpec((F32_SUBLANES, WA_BAND), lambda b, k, s, t: (0, 0)),
        ],
        out_specs=pl.BlockSpec((seq, GROUP * HEAD_DIM), lambda b, k, s, t: (b, k)),
        scratch_shapes=[
            pltpu.VMEM((seq + 2 * WINDOW, HEAD_DIM), BF16),
            pltpu.VMEM((seq + 2 * WINDOW, 2 * HEAD_DIM), BF16),
            pltpu.VMEM((GROUP * WA_ROWS, WA_BAND), F32),
            pltpu.VMEM((GROUP * WA_ROWS, HEAD_DIM), F32),
            pltpu.VMEM((2, GROUP * WA_ROWS, WA_BAND), F32),
            pltpu.VMEM((2, GROUP * WA_ROWS, HEAD_DIM), F32),
        ],
    )
    return pl.pallas_call(
        _window_attn_kernel,
        grid_spec=grid_spec,
        out_shape=jax.ShapeDtypeStruct((batch * seq, N_HEADS_B * HEAD_DIM), BF16),
        compiler_params=_params("parallel", "parallel"),
        name="window_attn",
    )(sink, table, proj, proj, proj, bucket)


OUT_TM = 512


def _out_proj_kernel(oa_ref, ob_ref, w_ref, x_ref, g_ref, h_ref, m_ref):
    ka = oa_ref.shape[1]
    for r in range(OUT_TM // SUB_ROWS):
        rows = pl.ds(r * SUB_ROWS, SUB_ROWS)
        h = x_ref[rows, :]
        h = h + jnp.dot(oa_ref[rows, :], w_ref[:ka, :], preferred_element_type=F32)
        h = h + jnp.dot(ob_ref[rows, :], w_ref[ka:, :], preferred_element_type=F32)
        h_ref[rows, :] = h
        m_ref[rows, :] = _rms(h, g_ref[...]).astype(BF16)


def _out_proj(oa, ob, w, x2, g):
    m, d = x2.shape
    ka, kb = oa.shape[1], ob.shape[1]
    return pl.pallas_call(
        _out_proj_kernel,
        grid=(m // OUT_TM,),
        in_specs=[
            pl.BlockSpec((OUT_TM, ka), lambda i: (i, 0)),
            pl.BlockSpec((OUT_TM, kb), lambda i: (i, 0)),
            pl.BlockSpec((ka + kb, d), lambda i: (0, 0)),
            pl.BlockSpec((OUT_TM, d), lambda i: (i, 0)),
            pl.BlockSpec((1, d), lambda i: (0, 0)),
        ],
        out_specs=[pl.BlockSpec((OUT_TM, d), lambda i: (i, 0)), pl.BlockSpec((OUT_TM, d), lambda i: (i, 0))],
        out_shape=[jax.ShapeDtypeStruct((m, d), F32), jax.ShapeDtypeStruct((m, d), BF16)],
        compiler_params=_params("parallel"),
        name="out_proj",
    )(oa, ob, w, x2, g)


MLP_TM = 1024
MLP_TF = 1024


def _mlp_kernel(m_ref, wu_ref, wd_ref, d_ref):
    @pl.when(pl.program_id(1) == 0)
    def _():
        d_ref[...] = jnp.zeros(d_ref.shape, F32)

    a = jnp.dot(m_ref[...], wu_ref[...], preferred_element_type=F32)
    act = jnp.square(jnp.maximum(a, 0.0)).astype(BF16)
    d_ref[...] += jnp.dot(act, wd_ref[...], preferred_element_type=F32)


def _mlp(mn, wu, wd):
    m, d = mn.shape
    dff = wu.shape[1]
    return pl.pallas_call(
        _mlp_kernel,
        grid=(m // MLP_TM, dff // MLP_TF),
        in_specs=[
            pl.BlockSpec((MLP_TM, d), lambda i, f: (i, 0)),
            pl.BlockSpec((d, MLP_TF), lambda i, f: (0, f)),
            pl.BlockSpec((MLP_TF, d), lambda i, f: (f, 0)),
        ],
        out_specs=pl.BlockSpec((MLP_TM, d), lambda i, f: (i, 0)),
        out_shape=jax.ShapeDtypeStruct((m, d), F32),
        compiler_params=_params("parallel", "arbitrary"),
        name="mlp",
    )(mn, wu, wd)


GATE_TM = 512


def _gate_final_kernel(h_ref, d_ref, wg_ref, p_ref, wp_ref, gg_ref, gp_ref, gf_ref, o_ref):
    for r in range(GATE_TM // SUB_ROWS):
        rows = pl.ds(r * SUB_ROWS, SUB_ROWS)
        h = h_ref[rows, :] + d_ref[rows, :]
        gate = jax.nn.sigmoid(jnp.dot(_rms(h, gg_ref[...]).astype(BF16), wg_ref[...], preferred_element_type=F32))
        e = _rms(jnp.dot(p_ref[rows, :].astype(BF16), wp_ref[...], preferred_element_type=F32), gp_ref[...])
        o_ref[rows, :] = _rms(h + gate * e, gf_ref[...])


def _gate_final(h, delta, wg, p2, wp, gg, gp, gf):
    m, d = h.shape
    dp = p2.shape[1]
    tile = pl.BlockSpec((GATE_TM, d), lambda i: (i, 0))
    gain = pl.BlockSpec((1, d), lambda i: (0, 0))
    return pl.pallas_call(
        _gate_final_kernel,
        grid=(m // GATE_TM,),
        in_specs=[
            tile,
            tile,
            pl.BlockSpec((d, d), lambda i: (0, 0)),
            pl.BlockSpec((GATE_TM, dp), lambda i: (i, 0)),
            pl.BlockSpec((dp, d), lambda i: (0, 0)),
            gain, gain, gain,
        ],
        out_specs=tile,
        out_shape=jax.ShapeDtypeStruct((m, d), F32),
        compiler_params=_params("parallel"),
        name="gate_final",
    )(h, delta, wg, p2, wp, gg, gp, gf)


def _rope_tables(seq):
    rows = seq // GRID_W
    row = np.repeat(np.arange(rows, dtype=np.float32), GRID_W)
    col = np.tile(np.arange(GRID_W, dtype=np.float32), rows)
    half = HEAD_DIM // 2
    inv_freq = np.float32(ROPE_THETA) ** (-np.arange(0, half, 2, dtype=np.float32) / np.float32(half))
    ang_r = row[:, None] * inv_freq
    ang_c = col[:, None] * inv_freq
    cr, sr, cc, sc = np.cos(ang_r), np.sin(ang_r), np.cos(ang_c), np.sin(ang_c)
    cos = np.concatenate([cr, cr, cc, cc], axis=-1).astype(np.float32)
    sin_signed = np.concatenate([-sr, sr, -sc, sc], axis=-1).astype(np.float32)
    return jnp.asarray(cos), jnp.asarray(sin_signed)


def _t5_bucket(rel):
    nb = N_BUCKETS // 2
    ret = jnp.where(rel > 0, nb, 0)
    n = jnp.abs(rel)
    max_exact = nb // 2
    nf = jnp.maximum(n, 1).astype(F32)
    large = max_exact + (jnp.log(nf / max_exact) / math.log(MAX_DISTANCE / max_exact)
                         * (nb - max_exact)).astype(jnp.int32)
    large = jnp.minimum(large, nb - 1)
    return ret + jnp.where(n < max_exact, n, large)


def _window_buckets():
    j = jnp.arange(WA_BAND, dtype=jnp.int32)
    rel = jnp.broadcast_to(j[None, :] - WINDOW, (F32_SUBLANES, WA_BAND))
    return jnp.where(jnp.abs(rel) <= WINDOW, _t5_bucket(rel), -1).astype(jnp.int32)


def kernel(x, p, attn_norm_g, w_in, q_norm_g, k_norm_g, sink_logits, w_out, mlp_norm_g, w_up, w_down, ple_w,
           ple_norm_g, gate_norm_g, w_gate, rel_bias_table, final_norm_g):
    batch, seq, d = x.shape
    assert w_in.shape[0] == 1, "gate_final fuses the final RMSNorm, which is only valid for a single layer"
    row = lambda v: v.reshape(1, -1).astype(F32)
    cos, sin_signed = _rope_tables(seq)
    h = x.reshape(batch * seq, d)
    proj = _in_proj(h, row(attn_norm_g), w_in[0].astype(BF16), cos, sin_signed, row(q_norm_g), row(k_norm_g), seq)
    oa, (wu, wd, wo, wg) = _global_attn(proj, (w_up, w_down, w_out, w_gate), batch, seq)
    ob = _window_attn(proj, _window_buckets(), rel_bias_table.reshape(-1).astype(F32),
                      sink_logits.reshape(-1).astype(F32), batch, seq)
    h1, mn = _out_proj(oa, ob, wo, h, row(mlp_norm_g))
    delta = _mlp(mn, wu, wd)
    out = _gate_final(h1, delta, wg, p.reshape(batch * seq, -1), ple_w[0].astype(BF16),
                      row(gate_norm_g), row(ple_norm_g), row(final_norm_g))
    return out.reshape(batch, seq, d)
```

```python
import functools
import math

import jax
import jax.numpy as jnp
import numpy as np
from jax import lax
from jax.experimental import pallas as pl
from jax.experimental.pallas import tpu as pltpu

HEAD_DIM = 128
N_HEADS_A = 8
N_KV_A = 2
N_HEADS_B = 8
N_KV_B = 2
GROUP = 4
GRID_W = 64
WINDOW = 128
N_BUCKETS = 32
MAX_DISTANCE = 128
ROPE_THETA = 10000.0
EPS = 1e-6
NEG_INF = -1e30
LOG2_E = math.log2(math.e)
Q_SCALE = HEAD_DIM ** -0.5 * LOG2_E

V7X_VMEM_BYTES = 64 * 1024 * 1024
VMEM_RESERVE_BYTES = 8 * 1024 * 1024
VMEM_LIMIT_BYTES = V7X_VMEM_BYTES - VMEM_RESERVE_BYTES
F32_SUBLANES = 8
BF16_SUBLANES = 16

SUB_ROWS = 256

BF16 = jnp.bfloat16
F32 = jnp.float32


def _params(*semantics):
    return pltpu.CompilerParams(dimension_semantics=semantics, vmem_limit_bytes=VMEM_LIMIT_BYTES)


def _rms(x, g):
    return x * lax.rsqrt(jnp.mean(x * x, axis=-1, keepdims=True) + EPS) * g


IN_TM = 512
IN_TN = 512

_PLAIN, _Q_A, _K_A, _Q_B = range(4)
_HEAD_KINDS = ([_Q_A] * N_HEADS_A + [_K_A] * N_KV_A + [_PLAIN] * N_KV_A
               + [_Q_B] * N_HEADS_B + [_PLAIN] * (2 * N_KV_B))


def _rope(y, cos, sin_signed):
    lane = lax.broadcasted_iota(jnp.int32, y.shape, 1)
    partner = jnp.where((lane % 64) < 32, pltpu.roll(y, 96, 1), pltpu.roll(y, 32, 1))
    return y * cos + partner * sin_signed


def _in_proj_kernel(x_ref, g_ref, w_ref, cos_ref, sin_ref, gq_ref, gk_ref, o_ref):
    heads_per_dot = IN_TN // HEAD_DIM
    for r in range(IN_TM // SUB_ROWS):
        rows = pl.ds(r * SUB_ROWS, SUB_ROWS)
        u = _rms(x_ref[rows, :], g_ref[...]).astype(BF16)
        cos, sin_signed = cos_ref[rows, :], sin_ref[rows, :]
        for c in range(w_ref.shape[1] // IN_TN):
            acc = jnp.dot(u, w_ref[:, c * IN_TN:(c + 1) * IN_TN], preferred_element_type=F32)
            for hh in range(heads_per_dot):
                head = c * heads_per_dot + hh
                a = acc[:, hh * HEAD_DIM:(hh + 1) * HEAD_DIM]
                kind = _HEAD_KINDS[head]
                if kind == _Q_A:
                    a = _rope(_rms(a, gq_ref[...]), cos, sin_signed) * Q_SCALE
                elif kind == _K_A:
                    a = _rope(_rms(a, gk_ref[...]), cos, sin_signed)
                elif kind == _Q_B:
                    a = a * Q_SCALE
                o_ref[rows, head * HEAD_DIM:(head + 1) * HEAD_DIM] = a.astype(BF16)


def _in_proj(x2, g, w, cos, sin_signed, gq, gk, seq):
    m, d = x2.shape
    n = w.shape[1]
    assert n == len(_HEAD_KINDS) * HEAD_DIM
    pos_tiles = seq // IN_TM
    return pl.pallas_call(
        _in_proj_kernel,
        grid=(m // IN_TM,),
        in_specs=[
            pl.BlockSpec((IN_TM, d), lambda i: (i, 0)),
            pl.BlockSpec((1, d), lambda i: (0, 0)),
            pl.BlockSpec((d, n), lambda i: (0, 0)),
            pl.BlockSpec((IN_TM, HEAD_DIM), lambda i: (i % pos_tiles, 0)),
            pl.BlockSpec((IN_TM, HEAD_DIM), lambda i: (i % pos_tiles, 0)),
            pl.BlockSpec((1, HEAD_DIM), lambda i: (0, 0)),
            pl.BlockSpec((1, HEAD_DIM), lambda i: (0, 0)),
        ],
        out_specs=pl.BlockSpec((IN_TM, n), lambda i: (i, 0)),
        out_shape=jax.ShapeDtypeStruct((m, n), BF16),
        compiler_params=_params("parallel"),
        name="in_proj",
    )(x2, g, w, cos, sin_signed, gq, gk)


GA_ROWS = 256


def _global_attn_kernel(*refs, n_cast):
    q_ref, k_ref, v_ref = refs[:3]
    w_f32 = refs[3:3 + n_cast]
    o_ref = refs[3 + n_cast]
    w_bf16 = refs[4 + n_cast:4 + 2 * n_cast]
    v1_ref, s_ref, m_ref = refs[4 + 2 * n_cast:7 + 2 * n_cast]
    in_bufs = refs[7 + 2 * n_cast:7 + 3 * n_cast]
    out_bufs = refs[7 + 3 * n_cast:7 + 4 * n_cast]
    in_sem, out_sem = refs[7 + 4 * n_cast:]

    seq = q_ref.shape[0]
    n_pairs = seq // (2 * GA_ROWS)
    first_chunk = (pl.program_id(0) * pl.num_programs(1) + pl.program_id(1)) * n_pairs

    def in_copy(w, chunk, slot):
        rows = in_bufs[w].shape[1]
        src = w_f32[w].at[0, pl.ds(pl.multiple_of(chunk * rows, rows), rows)]
        return pltpu.make_async_copy(src, in_bufs[w].at[slot], in_sem.at[w, slot])

    def out_copy(w, chunk):
        rows = out_bufs[w].shape[0]
        dst = w_bf16[w].at[pl.ds(pl.multiple_of(chunk * rows, rows), rows)]
        return pltpu.make_async_copy(out_bufs[w], dst, out_sem.at[w])

    def cast_chunk(i, first, last):
        chunk = first_chunk + i
        slot = i % 2
        for w in range(n_cast):
            in_copy(w, chunk, slot).wait()
            if not last:
                in_copy(w, chunk + 1, 1 - slot).start()
            if not first:
                out_copy(w, chunk - 1).wait()
            out_bufs[w][...] = in_bufs[w][slot].astype(BF16)
            out_copy(w, chunk).start()

    for w in range(n_cast):
        in_copy(w, first_chunk, 0).start()

    v1_ref[:, :HEAD_DIM] = v_ref[...]
    v1_ref[:, HEAD_DIM:] = jnp.ones(v_ref.shape, BF16)

    def rows_of(r):
        return pl.ds(pl.multiple_of(r * GA_ROWS, GA_ROWS), GA_ROWS)

    def scores(r, g, slot):
        q = q_ref[rows_of(r), g * HEAD_DIM:(g + 1) * HEAD_DIM]
        s = lax.dot_general(q, k_ref[...], (((1,), (1,)), ((), ())), preferred_element_type=F32)
        s_ref[slot] = s
        m_ref[slot] = jnp.max(s, axis=-1, keepdims=True)

    def apply(r, g, slot):
        p = jnp.exp2(s_ref[slot] - m_ref[slot]).astype(BF16)
        o = jnp.dot(p, v1_ref[...], preferred_element_type=F32)
        o_ref[rows_of(r), g * HEAD_DIM:(g + 1) * HEAD_DIM] = (o[:, :HEAD_DIM] / o[:, HEAD_DIM:]).astype(BF16)

    def row_block(r, last):
        for g in range(GROUP):
            slot = g % 2
            if g + 1 < GROUP:
                scores(r, g + 1, 1 - slot)
            elif not last:
                scores(r + 1, 0, 1 - slot)
            apply(r, g, slot)

    def pair(i, first, last):
        row_block(2 * i, last=False)
        row_block(2 * i + 1, last=last)
        cast_chunk(i, first, last)

    scores(0, 0, 0)
    pair(0, first=True, last=False)

    def body(i, carry):
        pair(i, first=False, last=False)
        return carry

    lax.fori_loop(1, n_pairs - 1, body, 0)
    pair(n_pairs - 1, first=False, last=True)
    for w in range(n_cast):
        out_copy(w, first_chunk + n_pairs - 1).wait()


def _global_attn(proj, weights, batch, seq):
    assert GROUP % 2 == 0, "score slots alternate per head and must line up across row blocks"
    n_pairs = seq // (2 * GA_ROWS)
    assert seq % (2 * GA_ROWS) == 0 and n_pairs >= 3
    n_chunks = batch * N_KV_A * n_pairs
    chunk_rows = []
    for w in weights:
        assert w.shape[0] == 1 and w.shape[1] % (BF16_SUBLANES * n_chunks) == 0, w.shape
        chunk_rows.append(w.shape[1] // n_chunks)
    n_cast = len(weights)
    k_col = N_HEADS_A
    v_col = N_HEADS_A + N_KV_A
    any_spec = pl.BlockSpec(memory_space=pl.ANY)
    attn_spec = pl.BlockSpec((seq, GROUP * HEAD_DIM), lambda b, k: (b, k))
    outs = pl.pallas_call(
        functools.partial(_global_attn_kernel, n_cast=n_cast),
        grid=(batch, N_KV_A),
        in_specs=[
            attn_spec,
            pl.BlockSpec((seq, HEAD_DIM), lambda b, k: (b, k_col + k)),
            pl.BlockSpec((seq, HEAD_DIM), lambda b, k: (b, v_col + k)),
        ] + [any_spec] * n_cast,
        out_specs=[attn_spec] + [any_spec] * n_cast,
        out_shape=[jax.ShapeDtypeStruct((batch * seq, N_HEADS_A * HEAD_DIM), BF16)]
        + [jax.ShapeDtypeStruct(w.shape[1:], BF16) for w in weights],
        scratch_shapes=[
            pltpu.VMEM((seq, 2 * HEAD_DIM), BF16),
            pltpu.VMEM((2, GA_ROWS, seq), F32),
            pltpu.VMEM((2, GA_ROWS, 1), F32),
        ] + [pltpu.VMEM((2, rows, w.shape[2]), F32) for rows, w in zip(chunk_rows, weights)]
        + [pltpu.VMEM((rows, w.shape[2]), BF16) for rows, w in zip(chunk_rows, weights)]
        + [pltpu.SemaphoreType.DMA((n_cast, 2)), pltpu.SemaphoreType.DMA((n_cast,))],
        compiler_params=_params("arbitrary", "arbitrary"),
        name="global_attn",
    )(proj, proj, proj, *weights)
    return outs[0], outs[1:]


WA_ROWS = 256
WA_BAND = WA_ROWS + 2 * WINDOW


def _window_attn_kernel(sink_ref, q_ref, k_ref, v_ref, line_ref, o_ref,
                        kpad_ref, v1pad_ref, bias_ref, sink_ref_b, s_ref, m_ref):
    kvh = pl.program_id(1)
    seq = q_ref.shape[0]
    n_stages = seq // WA_ROWS

    zeros = jnp.zeros((WINDOW, 2 * HEAD_DIM), BF16)
    kpad_ref[:WINDOW, :] = zeros[:, :HEAD_DIM]
    kpad_ref[WINDOW + seq:, :] = zeros[:, :HEAD_DIM]
    kpad_ref[WINDOW:WINDOW + seq, :] = k_ref[...]
    v1pad_ref[:WINDOW, :] = zeros
    v1pad_ref[WINDOW + seq:, :] = zeros
    v1pad_ref[WINDOW:WINDOW + seq, :HEAD_DIM] = v_ref[...]
    v1pad_ref[WINDOW:WINDOW + seq, HEAD_DIM:] = jnp.ones(v_ref.shape, BF16)

    for g in range(GROUP):
        head = kvh * GROUP + g
        tile = jnp.broadcast_to(line_ref[g, :1, :], (WA_ROWS, WA_BAND))
        bias_ref[g * WA_ROWS:(g + 1) * WA_ROWS, :] = pltpu.roll(tile, 0, 1, stride=1, stride_axis=0)
        sink_ref_b[g * WA_ROWS:(g + 1) * WA_ROWS, :] = jnp.full((WA_ROWS, HEAD_DIM), sink_ref[head] * LOG2_E, F32)

    col = lax.broadcasted_iota(jnp.int32, (GROUP * WA_ROWS, WA_BAND), 1)

    def rows_of(n):
        return pl.ds(pl.multiple_of(n * WA_ROWS, WA_ROWS), WA_ROWS)

    def band_of(n):
        return pl.ds(pl.multiple_of(n * WA_ROWS, WA_ROWS), WA_BAND)

    def scores(n, slot):
        q = jnp.concatenate([q_ref[rows_of(n), g * HEAD_DIM:(g + 1) * HEAD_DIM] for g in range(GROUP)], axis=0)
        s = lax.dot_general(q, kpad_ref[band_of(n), :], (((1,), (1,)), ((), ())), preferred_element_type=F32)
        in_range = ((col >= WINDOW) | (n > 0)) & ((col < WA_BAND - WINDOW) | (n < n_stages - 1))
        s = jnp.where(in_range, s + bias_ref[...], NEG_INF)
        s_ref[slot] = s
        row_max = jnp.broadcast_to(jnp.max(s, axis=-1, keepdims=True), sink_ref_b.shape)
        m_ref[slot] = jnp.maximum(row_max, sink_ref_b[...])

    def apply(n, slot):
        m = m_ref[slot]
        p = jnp.concatenate([jnp.exp2(s_ref[slot, :, c * HEAD_DIM:(c + 1) * HEAD_DIM] - m)
                             for c in range(WA_BAND // HEAD_DIM)], axis=1).astype(BF16)
        o = jnp.dot(p, v1pad_ref[band_of(n), :], preferred_element_type=F32)
        o = o[:, :HEAD_DIM] / (o[:, HEAD_DIM:] + jnp.exp2(sink_ref_b[...] - m))
        for g in range(GROUP):
            o_ref[rows_of(n), g * HEAD_DIM:(g + 1) * HEAD_DIM] = o[g * WA_ROWS:(g + 1) * WA_ROWS].astype(BF16)

    scores(0, 0)

    def body(i, carry):
        for slot in range(2):
            n = 2 * i + slot
            scores(n + 1, 1 - slot)
            apply(n, slot)
        return carry

    lax.fori_loop(0, n_stages // 2 - 1, body, 0)
    scores(n_stages - 1, 1)
    apply(n_stages - 2, 0)
    apply(n_stages - 1, 1)


def _window_attn(proj, bias_lines, sink, batch, seq):
    assert (seq // WA_ROWS) % 2 == 0, "the stage loop is unrolled by two so score slots are static"
    q_col = (N_HEADS_A + 2 * N_KV_A) // GROUP
    k_col = N_HEADS_A + 2 * N_KV_A + N_HEADS_B
    v_col = k_col + N_KV_B
    grid_spec = pltpu.PrefetchScalarGridSpec(
        num_scalar_prefetch=1,
        grid=(batch, N_KV_B),
        in_specs=[
            pl.BlockSpec((seq, GROUP * HEAD_DIM), lambda b, k, s: (b, q_col + k)),
            pl.BlockSpec((seq, HEAD_DIM), lambda b, k, s: (b, k_col + k)),
            pl.BlockSpec((seq, HEAD_DIM), lambda b, k, s: (b, v_col + k)),
            pl.BlockSpec((GROUP, F32_SUBLANES, WA_BAND), lambda b, k, s: (k, 0, 0)),
        ],
        out_specs=pl.BlockSpec((seq, GROUP * HEAD_DIM), lambda b, k, s: (b, k)),
        scratch_shapes=[
            pltpu.VMEM((seq + 2 * WINDOW, HEAD_DIM), BF16),
            pltpu.VMEM((seq + 2 * WINDOW, 2 * HEAD_DIM), BF16),
            pltpu.VMEM((GROUP * WA_ROWS, WA_BAND), F32),
            pltpu.VMEM((GROUP * WA_ROWS, HEAD_DIM), F32),
            pltpu.VMEM((2, GROUP * WA_ROWS, WA_BAND), F32),
            pltpu.VMEM((2, GROUP * WA_ROWS, HEAD_DIM), F32),
        ],
    )
    return pl.pallas_call(
        _window_attn_kernel,
        grid_spec=grid_spec,
        out_shape=jax.ShapeDtypeStruct((batch * seq, N_HEADS_B * HEAD_DIM), BF16),
        compiler_params=_params("parallel", "parallel"),
        name="window_attn",
    )(sink, proj, proj, proj, bias_lines)


OUT_TM = 512


def _out_proj_kernel(oa_ref, ob_ref, w_ref, x_ref, g_ref, h_ref, m_ref):
    ka = oa_ref.shape[1]
    for r in range(OUT_TM // SUB_ROWS):
        rows = pl.ds(r * SUB_ROWS, SUB_ROWS)
        h = x_ref[rows, :]
        h = h + jnp.dot(oa_ref[rows, :], w_ref[:ka, :], preferred_element_type=F32)
        h = h + jnp.dot(ob_ref[rows, :], w_ref[ka:, :], preferred_element_type=F32)
        h_ref[rows, :] = h
        m_ref[rows, :] = _rms(h, g_ref[...]).astype(BF16)


def _out_proj(oa, ob, w, x2, g):
    m, d = x2.shape
    ka, kb = oa.shape[1], ob.shape[1]
    return pl.pallas_call(
        _out_proj_kernel,
        grid=(m // OUT_TM,),
        in_specs=[
            pl.BlockSpec((OUT_TM, ka), lambda i: (i, 0)),
            pl.BlockSpec((OUT_TM, kb), lambda i: (i, 0)),
            pl.BlockSpec((ka + kb, d), lambda i: (0, 0)),
            pl.BlockSpec((OUT_TM, d), lambda i: (i, 0)),
            pl.BlockSpec((1, d), lambda i: (0, 0)),
        ],
        out_specs=[pl.BlockSpec((OUT_TM, d), lambda i: (i, 0)), pl.BlockSpec((OUT_TM, d), lambda i: (i, 0))],
        out_shape=[jax.ShapeDtypeStruct((m, d), F32), jax.ShapeDtypeStruct((m, d), BF16)],
        compiler_params=_params("parallel"),
        name="out_proj",
    )(oa, ob, w, x2, g)


MLP_TM = 1024
MLP_TF = 1024


def _mlp_kernel(m_ref, wu_ref, wd_ref, d_ref):
    @pl.when(pl.program_id(1) == 0)
    def _():
        d_ref[...] = jnp.zeros(d_ref.shape, F32)

    a = jnp.dot(m_ref[...], wu_ref[...], preferred_element_type=F32)
    act = jnp.square(jnp.maximum(a, 0.0)).astype(BF16)
    d_ref[...] += jnp.dot(act, wd_ref[...], preferred_element_type=F32)


def _mlp(mn, wu, wd):
    m, d = mn.shape
    dff = wu.shape[1]
    return pl.pallas_call(
        _mlp_kernel,
        grid=(m // MLP_TM, dff // MLP_TF),
        in_specs=[
            pl.BlockSpec((MLP_TM, d), lambda i, f: (i, 0)),
            pl.BlockSpec((d, MLP_TF), lambda i, f: (0, f)),
            pl.BlockSpec((MLP_TF, d), lambda i, f: (f, 0)),
        ],
        out_specs=pl.BlockSpec((MLP_TM, d), lambda i, f: (i, 0)),
        out_shape=jax.ShapeDtypeStruct((m, d), F32),
        compiler_params=_params("parallel", "arbitrary"),
        name="mlp",
    )(mn, wu, wd)


GATE_TM = 512


def _gate_final_kernel(h_ref, d_ref, wg_ref, p_ref, wp_ref, gg_ref, gp_ref, gf_ref, o_ref):
    for r in range(GATE_TM // SUB_ROWS):
        rows = pl.ds(r * SUB_ROWS, SUB_ROWS)
        h = h_ref[rows, :] + d_ref[rows, :]
        gate = jax.nn.sigmoid(jnp.dot(_rms(h, gg_ref[...]).astype(BF16), wg_ref[...], preferred_element_type=F32))
        e = _rms(jnp.dot(p_ref[rows, :].astype(BF16), wp_ref[...], preferred_element_type=F32), gp_ref[...])
        o_ref[rows, :] = _rms(h + gate * e, gf_ref[...])


def _gate_final(h, delta, wg, p2, wp, gg, gp, gf):
    m, d = h.shape
    dp = p2.shape[1]
    tile = pl.BlockSpec((GATE_TM, d), lambda i: (i, 0))
    gain = pl.BlockSpec((1, d), lambda i: (0, 0))
    return pl.pallas_call(
        _gate_final_kernel,
        grid=(m // GATE_TM,),
        in_specs=[
            tile,
            tile,
            pl.BlockSpec((d, d), lambda i: (0, 0)),
            pl.BlockSpec((GATE_TM, dp), lambda i: (i, 0)),
            pl.BlockSpec((dp, d), lambda i: (0, 0)),
            gain, gain, gain,
        ],
        out_specs=tile,
        out_shape=jax.ShapeDtypeStruct((m, d), F32),
        compiler_params=_params("parallel"),
        name="gate_final",
    )(h, delta, wg, p2, wp, gg, gp, gf)


def _rope_tables(seq):
    rows = seq // GRID_W
    row = np.repeat(np.arange(rows, dtype=np.float32), GRID_W)
    col = np.tile(np.arange(GRID_W, dtype=np.float32), rows)
    half = HEAD_DIM // 2
    inv_freq = np.float32(ROPE_THETA) ** (-np.arange(0, half, 2, dtype=np.float32) / np.float32(half))
    ang_r = row[:, None] * inv_freq
    ang_c = col[:, None] * inv_freq
    cr, sr, cc, sc = np.cos(ang_r), np.sin(ang_r), np.cos(ang_c), np.sin(ang_c)
    cos = np.concatenate([cr, cr, cc, cc], axis=-1).astype(np.float32)
    sin_signed = np.concatenate([-sr, sr, -sc, sc], axis=-1).astype(np.float32)
    return jnp.asarray(cos), jnp.asarray(sin_signed)


def _t5_bucket(rel):
    nb = N_BUCKETS // 2
    ret = jnp.where(rel > 0, nb, 0)
    n = jnp.abs(rel)
    max_exact = nb // 2
    nf = jnp.maximum(n, 1).astype(F32)
    large = max_exact + (jnp.log(nf / max_exact) / math.log(MAX_DISTANCE / max_exact)
                         * (nb - max_exact)).astype(jnp.int32)
    large = jnp.minimum(large, nb - 1)
    return ret + jnp.where(n < max_exact, n, large)


def _window_bias_lines(rel_bias_table):
    rel = jnp.arange(WA_BAND, dtype=jnp.int32) - WINDOW
    line = rel_bias_table[_t5_bucket(rel)].astype(F32).T * LOG2_E
    line = jnp.where((jnp.abs(rel) <= WINDOW)[None, :], line, NEG_INF)
    return jnp.broadcast_to(line[:, None, :], (line.shape[0], F32_SUBLANES, WA_BAND))


def kernel(x, p, attn_norm_g, w_in, q_norm_g, k_norm_g, sink_logits, w_out, mlp_norm_g, w_up, w_down, ple_w,
           ple_norm_g, gate_norm_g, w_gate, rel_bias_table, final_norm_g):
    batch, seq, d = x.shape
    assert w_in.shape[0] == 1, "gate_final fuses the final RMSNorm, which is only valid for a single layer"
    row = lambda v: v.reshape(1, -1).astype(F32)
    cos, sin_signed = _rope_tables(seq)
    h = x.reshape(batch * seq, d)
    proj = _in_proj(h, row(attn_norm_g), w_in[0].astype(BF16), cos, sin_signed, row(q_norm_g), row(k_norm_g), seq)
    oa, (wu, wd, wo, wg) = _global_attn(proj, (w_up, w_down, w_out, w_gate), batch, seq)
    ob = _window_attn(proj, _window_bias_lines(rel_bias_table), sink_logits.reshape(-1).astype(F32), batch, seq)
    h1, mn = _out_proj(oa, ob, wo, h, row(mlp_norm_g))
    delta = _mlp(mn, wu, wd)
    out = _gate_final(h1, delta, wg, p.reshape(batch * seq, -1), ple_w[0].astype(BF16),
                      row(gate_norm_g), row(ple_norm_g), row(final_norm_g))
    return out.reshape(batch, seq, d)
```

```python
import functools
import math

import jax
import jax.numpy as jnp
import numpy as np
from jax import lax
from jax.experimental import pallas as pl
from jax.experimental.pallas import tpu as pltpu

HEAD_DIM = 128
N_HEADS_A = 8
N_KV_A = 2
N_HEADS_B = 8
N_KV_B = 2
GROUP = 4
GRID_W = 64
WINDOW = 128
N_BUCKETS = 32
MAX_DISTANCE = 128
ROPE_THETA = 10000.0
EPS = 1e-6
NEG_INF = -1e30
LOG2_E = math.log2(math.e)
Q_SCALE = HEAD_DIM ** -0.5 * LOG2_E

V7X_VMEM_BYTES = 64 * 1024 * 1024
VMEM_RESERVE_BYTES = 8 * 1024 * 1024
VMEM_LIMIT_BYTES = V7X_VMEM_BYTES - VMEM_RESERVE_BYTES
F32_SUBLANES = 8
BF16_SUBLANES = 16

SUB_ROWS = 256

BF16 = jnp.bfloat16
F32 = jnp.float32


def _params(*semantics):
    return pltpu.CompilerParams(dimension_semantics=semantics, vmem_limit_bytes=VMEM_LIMIT_BYTES)


def _rms(x, g):
    return x * lax.rsqrt(jnp.mean(x * x, axis=-1, keepdims=True) + EPS) * g


IN_TM = 512
IN_TN = 512

_PLAIN, _Q_A, _K_A, _Q_B = range(4)
_HEAD_KINDS = ([_Q_A] * N_HEADS_A + [_K_A] * N_KV_A + [_PLAIN] * N_KV_A
               + [_Q_B] * N_HEADS_B + [_PLAIN] * (2 * N_KV_B))


def _rope(y, cos, sin_signed):
    lane = lax.broadcasted_iota(jnp.int32, y.shape, 1)
    partner = jnp.where((lane % 64) < 32, pltpu.roll(y, 96, 1), pltpu.roll(y, 32, 1))
    return y * cos + partner * sin_signed


def _in_proj_kernel(x_ref, g_ref, w_ref, cos_ref, sin_ref, gq_ref, gk_ref, o_ref):
    heads_per_dot = IN_TN // HEAD_DIM
    for r in range(IN_TM // SUB_ROWS):
        rows = pl.ds(r * SUB_ROWS, SUB_ROWS)
        u = _rms(x_ref[rows, :], g_ref[...]).astype(BF16)
        cos, sin_signed = cos_ref[rows, :], sin_ref[rows, :]
        for c in range(w_ref.shape[1] // IN_TN):
            acc = jnp.dot(u, w_ref[:, c * IN_TN:(c + 1) * IN_TN], preferred_element_type=F32)
            for hh in range(heads_per_dot):
                head = c * heads_per_dot + hh
                a = acc[:, hh * HEAD_DIM:(hh + 1) * HEAD_DIM]
                kind = _HEAD_KINDS[head]
                if kind == _Q_A:
                    a = _rope(_rms(a, gq_ref[...]), cos, sin_signed) * Q_SCALE
                elif kind == _K_A:
                    a = _rope(_rms(a, gk_ref[...]), cos, sin_signed)
                elif kind == _Q_B:
                    a = a * Q_SCALE
                o_ref[rows, head * HEAD_DIM:(head + 1) * HEAD_DIM] = a.astype(BF16)


def _in_proj(x2, g, w, cos, sin_signed, gq, gk, seq):
    m, d = x2.shape
    n = w.shape[1]
    assert n == len(_HEAD_KINDS) * HEAD_DIM
    pos_tiles = seq // IN_TM
    return pl.pallas_call(
        _in_proj_kernel,
        grid=(m // IN_TM,),
        in_specs=[
            pl.BlockSpec((IN_TM, d), lambda i: (i, 0)),
            pl.BlockSpec((1, d), lambda i: (0, 0)),
            pl.BlockSpec((d, n), lambda i: (0, 0)),
            pl.BlockSpec((IN_TM, HEAD_DIM), lambda i: (i % pos_tiles, 0)),
            pl.BlockSpec((IN_TM, HEAD_DIM), lambda i: (i % pos_tiles, 0)),
            pl.BlockSpec((1, HEAD_DIM), lambda i: (0, 0)),
            pl.BlockSpec((1, HEAD_DIM), lambda i: (0, 0)),
        ],
        out_specs=pl.BlockSpec((IN_TM, n), lambda i: (i, 0)),
        out_shape=jax.ShapeDtypeStruct((m, n), BF16),
        compiler_params=_params("parallel"),
        name="in_proj",
    )(x2, g, w, cos, sin_signed, gq, gk)


GA_ROWS = 256


def _global_attn_kernel(*refs, n_cast):
    q_ref, k_ref, v_ref = refs[:3]
    w_f32 = refs[3:3 + n_cast]
    o_ref = refs[3 + n_cast]
    w_bf16 = refs[4 + n_cast:4 + 2 * n_cast]
    v1_ref, kt_ref, s_ref, m_ref = refs[4 + 2 * n_cast:8 + 2 * n_cast]
    in_bufs = refs[8 + 2 * n_cast:8 + 3 * n_cast]
    out_bufs = refs[8 + 3 * n_cast:8 + 4 * n_cast]
    in_sem, out_sem = refs[8 + 4 * n_cast:]

    seq = q_ref.shape[0]
    n_blocks = seq // GA_ROWS
    first_chunk = (pl.program_id(0) * pl.num_programs(1) + pl.program_id(1)) * n_blocks

    def in_copy(w, chunk, slot):
        rows = in_bufs[w].shape[1]
        src = w_f32[w].at[0, pl.ds(pl.multiple_of(chunk * rows, rows), rows)]
        return pltpu.make_async_copy(src, in_bufs[w].at[slot], in_sem.at[w, slot])

    def out_copy(w, chunk):
        rows = out_bufs[w].shape[0]
        dst = w_bf16[w].at[pl.ds(pl.multiple_of(chunk * rows, rows), rows)]
        return pltpu.make_async_copy(out_bufs[w], dst, out_sem.at[w])

    def cast_chunk(r, first, last):
        chunk = first_chunk + r
        slot = r % 2
        for w in range(n_cast):
            in_copy(w, chunk, slot).wait()
            if not last:
                in_copy(w, chunk + 1, 1 - slot).start()
            if not first:
                out_copy(w, chunk - 1).wait()
            out_bufs[w][...] = in_bufs[w][slot].astype(BF16)
            out_copy(w, chunk).start()

    for w in range(n_cast):
        in_copy(w, first_chunk, 0).start()

    v1_ref[:, :HEAD_DIM] = v_ref[...]
    v1_ref[:, HEAD_DIM:] = jnp.ones(v_ref.shape, BF16)
    kt_ref[...] = k_ref[...].T

    def rows_of(r):
        return pl.ds(pl.multiple_of(r * GA_ROWS, GA_ROWS), GA_ROWS)

    def scores(r, g, slot):
        q = q_ref[rows_of(r), g * HEAD_DIM:(g + 1) * HEAD_DIM]
        s = jnp.dot(q, kt_ref[...], preferred_element_type=F32)
        s_ref[slot] = s
        m_ref[slot] = jnp.max(s, axis=-1, keepdims=True)

    def apply(r, g, slot):
        p = jnp.exp2(s_ref[slot] - m_ref[slot]).astype(BF16)
        o = jnp.dot(p, v1_ref[...], preferred_element_type=F32)
        o_ref[rows_of(r), g * HEAD_DIM:(g + 1) * HEAD_DIM] = (o[:, :HEAD_DIM] / o[:, HEAD_DIM:]).astype(BF16)

    def row_block(r, first, last):
        for g in range(GROUP):
            slot = g % 2
            if g + 1 < GROUP:
                scores(r, g + 1, 1 - slot)
            elif not last:
                scores(r + 1, 0, 1 - slot)
            apply(r, g, slot)
        cast_chunk(r, first, last)

    scores(0, 0, 0)
    row_block(0, first=True, last=False)

    def body(r, carry):
        row_block(r, first=False, last=False)
        return carry

    lax.fori_loop(1, n_blocks - 1, body, 0)
    row_block(n_blocks - 1, first=False, last=True)
    for w in range(n_cast):
        out_copy(w, first_chunk + n_blocks - 1).wait()


def _global_attn(proj, weights, batch, seq):
    assert GROUP % 2 == 0, "score slots alternate per head and must line up across row blocks"
    n_blocks = seq // GA_ROWS
    assert n_blocks >= 3 and n_blocks % 2 == 0
    n_chunks = batch * N_KV_A * n_blocks
    chunk_rows = []
    for w in weights:
        assert w.shape[0] == 1 and w.shape[1] % (BF16_SUBLANES * n_chunks) == 0, w.shape
        chunk_rows.append(w.shape[1] // n_chunks)
    n_cast = len(weights)
    k_col = N_HEADS_A
    v_col = N_HEADS_A + N_KV_A
    any_spec = pl.BlockSpec(memory_space=pl.ANY)
    attn_spec = pl.BlockSpec((seq, GROUP * HEAD_DIM), lambda b, k: (b, k))
    outs = pl.pallas_call(
        functools.partial(_global_attn_kernel, n_cast=n_cast),
        grid=(batch, N_KV_A),
        in_specs=[
            attn_spec,
            pl.BlockSpec((seq, HEAD_DIM), lambda b, k: (b, k_col + k)),
            pl.BlockSpec((seq, HEAD_DIM), lambda b, k: (b, v_col + k)),
        ] + [any_spec] * n_cast,
        out_specs=[attn_spec] + [any_spec] * n_cast,
        out_shape=[jax.ShapeDtypeStruct((batch * seq, N_HEADS_A * HEAD_DIM), BF16)]
        + [jax.ShapeDtypeStruct(w.shape[1:], BF16) for w in weights],
        scratch_shapes=[
            pltpu.VMEM((seq, 2 * HEAD_DIM), BF16),
            pltpu.VMEM((HEAD_DIM, seq), BF16),
            pltpu.VMEM((2, GA_ROWS, seq), F32),
            pltpu.VMEM((2, GA_ROWS, 1), F32),
        ] + [pltpu.VMEM((2, rows, w.shape[2]), F32) for rows, w in zip(chunk_rows, weights)]
        + [pltpu.VMEM((rows, w.shape[2]), BF16) for rows, w in zip(chunk_rows, weights)]
        + [pltpu.SemaphoreType.DMA((n_cast, 2)), pltpu.SemaphoreType.DMA((n_cast,))],
        compiler_params=_params("arbitrary", "arbitrary"),
        name="global_attn",
    )(proj, proj, proj, *weights)
    return outs[0], outs[1:]


WA_ROWS = 256
WA_BAND = WA_ROWS + 2 * WINDOW


def _window_attn_kernel(sink_ref, q_ref, k_ref, v_ref, line_ref, o_ref,
                        kpad_ref, v1pad_ref, bias_ref, sink_ref_b, s_ref, m_ref):
    kvh = pl.program_id(1)
    seq = q_ref.shape[0]
    n_stages = seq // WA_ROWS

    zeros = jnp.zeros((WINDOW, 2 * HEAD_DIM), BF16)
    kpad_ref[:WINDOW, :] = zeros[:, :HEAD_DIM]
    kpad_ref[WINDOW + seq:, :] = zeros[:, :HEAD_DIM]
    kpad_ref[WINDOW:WINDOW + seq, :] = k_ref[...]
    v1pad_ref[:WINDOW, :] = zeros
    v1pad_ref[WINDOW + seq:, :] = zeros
    v1pad_ref[WINDOW:WINDOW + seq, :HEAD_DIM] = v_ref[...]
    v1pad_ref[WINDOW:WINDOW + seq, HEAD_DIM:] = jnp.ones(v_ref.shape, BF16)

    for g in range(GROUP):
        head = kvh * GROUP + g
        tile = jnp.broadcast_to(line_ref[g, :1, :], (WA_ROWS, WA_BAND))
        bias_ref[g * WA_ROWS:(g + 1) * WA_ROWS, :] = pltpu.roll(tile, 0, 1, stride=1, stride_axis=0)
        sink_ref_b[g * WA_ROWS:(g + 1) * WA_ROWS, :] = jnp.full((WA_ROWS, HEAD_DIM), sink_ref[head] * LOG2_E, F32)

    col = lax.broadcasted_iota(jnp.int32, (GROUP * WA_ROWS, WA_BAND), 1)

    def rows_of(n):
        return pl.ds(pl.multiple_of(n * WA_ROWS, WA_ROWS), WA_ROWS)

    def band_of(n):
        return pl.ds(pl.multiple_of(n * WA_ROWS, WA_ROWS), WA_BAND)

    def scores(n, slot):
        q = jnp.concatenate([q_ref[rows_of(n), g * HEAD_DIM:(g + 1) * HEAD_DIM] for g in range(GROUP)], axis=0)
        s = lax.dot_general(q, kpad_ref[band_of(n), :], (((1,), (1,)), ((), ())), preferred_element_type=F32)
        in_range = ((col >= WINDOW) | (n > 0)) & ((col < WA_BAND - WINDOW) | (n < n_stages - 1))
        s = jnp.where(in_range, s + bias_ref[...], NEG_INF)
        s_ref[slot] = s
        row_max = jnp.broadcast_to(jnp.max(s, axis=-1, keepdims=True), sink_ref_b.shape)
        m_ref[slot] = jnp.maximum(row_max, sink_ref_b[...])

    def apply(n, slot):
        m = m_ref[slot]
        p = jnp.concatenate([jnp.exp2(s_ref[slot, :, c * HEAD_DIM:(c + 1) * HEAD_DIM] - m)
                             for c in range(WA_BAND // HEAD_DIM)], axis=1).astype(BF16)
        o = jnp.dot(p, v1pad_ref[band_of(n), :], preferred_element_type=F32)
        o = o[:, :HEAD_DIM] / (o[:, HEAD_DIM:] + jnp.exp2(sink_ref_b[...] - m))
        for g in range(GROUP):
            o_ref[rows_of(n), g * HEAD_DIM:(g + 1) * HEAD_DIM] = o[g * WA_ROWS:(g + 1) * WA_ROWS].astype(BF16)

    scores(0, 0)

    def body(i, carry):
        for slot in range(2):
            n = 2 * i + slot
            scores(n + 1, 1 - slot)
            apply(n, slot)
        return carry

    lax.fori_loop(0, n_stages // 2 - 1, body, 0)
    scores(n_stages - 1, 1)
    apply(n_stages - 2, 0)
    apply(n_stages - 1, 1)


def _window_attn(proj, bias_lines, sink, batch, seq):
    assert (seq // WA_ROWS) % 2 == 0, "the stage loop is unrolled by two so score slots are static"
    q_col = (N_HEADS_A + 2 * N_KV_A) // GROUP
    k_col = N_HEADS_A + 2 * N_KV_A + N_HEADS_B
    v_col = k_col + N_KV_B
    grid_spec = pltpu.PrefetchScalarGridSpec(
        num_scalar_prefetch=1,
        grid=(batch, N_KV_B),
        in_specs=[
            pl.BlockSpec((seq, GROUP * HEAD_DIM), lambda b, k, s: (b, q_col + k)),
            pl.BlockSpec((seq, HEAD_DIM), lambda b, k, s: (b, k_col + k)),
            pl.BlockSpec((seq, HEAD_DIM), lambda b, k, s: (b, v_col + k)),
            pl.BlockSpec((GROUP, F32_SUBLANES, WA_BAND), lambda b, k, s: (k, 0, 0)),
        ],
        out_specs=pl.BlockSpec((seq, GROUP * HEAD_DIM), lambda b, k, s: (b, k)),
        scratch_shapes=[
            pltpu.VMEM((seq + 2 * WINDOW, HEAD_DIM), BF16),
            pltpu.VMEM((seq + 2 * WINDOW, 2 * HEAD_DIM), BF16),
            pltpu.VMEM((GROUP * WA_ROWS, WA_BAND), F32),
            pltpu.VMEM((GROUP * WA_ROWS, HEAD_DIM), F32),
            pltpu.VMEM((2, GROUP * WA_ROWS, WA_BAND), F32),
            pltpu.VMEM((2, GROUP * WA_ROWS, HEAD_DIM), F32),
        ],
    )
    return pl.pallas_call(
        _window_attn_kernel,
        grid_spec=grid_spec,
        out_shape=jax.ShapeDtypeStruct((batch * seq, N_HEADS_B * HEAD_DIM), BF16),
        compiler_params=_params("parallel", "parallel"),
        name="window_attn",
    )(sink, proj, proj, proj, bias_lines)


OUT_TM = 512


def _out_proj_kernel(oa_ref, ob_ref, w_ref, x_ref, g_ref, h_ref, m_ref):
    ka = oa_ref.shape[1]
    for r in range(OUT_TM // SUB_ROWS):
        rows = pl.ds(r * SUB_ROWS, SUB_ROWS)
        h = x_ref[rows, :]
        h = h + jnp.dot(oa_ref[rows, :], w_ref[:ka, :], preferred_element_type=F32)
        h = h + jnp.dot(ob_ref[rows, :], w_ref[ka:, :], preferred_element_type=F32)
        h_ref[rows, :] = h
        m_ref[rows, :] = _rms(h, g_ref[...]).astype(BF16)


def _out_proj(oa, ob, w, x2, g):
    m, d = x2.shape
    ka, kb = oa.shape[1], ob.shape[1]
    return pl.pallas_call(
        _out_proj_kernel,
        grid=(m // OUT_TM,),
        in_specs=[
            pl.BlockSpec((OUT_TM, ka), lambda i: (i, 0)),
            pl.BlockSpec((OUT_TM, kb), lambda i: (i, 0)),
            pl.BlockSpec((ka + kb, d), lambda i: (0, 0)),
            pl.BlockSpec((OUT_TM, d), lambda i: (i, 0)),
            pl.BlockSpec((1, d), lambda i: (0, 0)),
        ],
        out_specs=[pl.BlockSpec((OUT_TM, d), lambda i: (i, 0)), pl.BlockSpec((OUT_TM, d), lambda i: (i, 0))],
        out_shape=[jax.ShapeDtypeStruct((m, d), F32), jax.ShapeDtypeStruct((m, d), BF16)],
        compiler_params=_params("parallel"),
        name="out_proj",
    )(oa, ob, w, x2, g)


MLP_TM = 1024
MLP_TF = 1024


def _mlp_kernel(m_ref, wu_ref, wd_ref, d_ref):
    @pl.when(pl.program_id(1) == 0)
    def _():
        d_ref[...] = jnp.zeros(d_ref.shape, F32)

    a = jnp.dot(m_ref[...], wu_ref[...], preferred_element_type=F32)
    act = jnp.square(jnp.maximum(a, 0.0)).astype(BF16)
    d_ref[...] += jnp.dot(act, wd_ref[...], preferred_element_type=F32)


def _mlp(mn, wu, wd):
    m, d = mn.shape
    dff = wu.shape[1]
    return pl.pallas_call(
        _mlp_kernel,
        grid=(m // MLP_TM, dff // MLP_TF),
        in_specs=[
            pl.BlockSpec((MLP_TM, d), lambda i, f: (i, 0)),
            pl.BlockSpec((d, MLP_TF), lambda i, f: (0, f)),
            pl.BlockSpec((MLP_TF, d), lambda i, f: (f, 0)),
        ],
        out_specs=pl.BlockSpec((MLP_TM, d), lambda i, f: (i, 0)),
        out_shape=jax.ShapeDtypeStruct((m, d), F32),
        compiler_params=_params("parallel", "arbitrary"),
        name="mlp",
    )(mn, wu, wd)


GATE_TM = 512


def _gate_final_kernel(h_ref, d_ref, wg_ref, p_ref, wp_ref, gg_ref, gp_ref, gf_ref, o_ref):
    for r in range(GATE_TM // SUB_ROWS):
        rows = pl.ds(r * SUB_ROWS, SUB_ROWS)
        h = h_ref[rows, :] + d_ref[rows, :]
        gate = jax.nn.sigmoid(jnp.dot(_rms(h, gg_ref[...]).astype(BF16), wg_ref[...], preferred_element_type=F32))
        e = _rms(jnp.dot(p_ref[rows, :].astype(BF16), wp_ref[...], preferred_element_type=F32), gp_ref[...])
        o_ref[rows, :] = _rms(h + gate * e, gf_ref[...])


def _gate_final(h, delta, wg, p2, wp, gg, gp, gf):
    m, d = h.shape
    dp = p2.shape[1]
    tile = pl.BlockSpec((GATE_TM, d), lambda i: (i, 0))
    gain = pl.BlockSpec((1, d), lambda i: (0, 0))
    return pl.pallas_call(
        _gate_final_kernel,
        grid=(m // GATE_TM,),
        in_specs=[
            tile,
            tile,
            pl.BlockSpec((d, d), lambda i: (0, 0)),
            pl.BlockSpec((GATE_TM, dp), lambda i: (i, 0)),
            pl.BlockSpec((dp, d), lambda i: (0, 0)),
            gain, gain, gain,
        ],
        out_specs=tile,
        out_shape=jax.ShapeDtypeStruct((m, d), F32),
        compiler_params=_params("parallel"),
        name="gate_final",
    )(h, delta, wg, p2, wp, gg, gp, gf)


def _rope_tables(seq):
    rows = seq // GRID_W
    row = np.repeat(np.arange(rows, dtype=np.float32), GRID_W)
    col = np.tile(np.arange(GRID_W, dtype=np.float32), rows)
    half = HEAD_DIM // 2
    inv_freq = np.float32(ROPE_THETA) ** (-np.arange(0, half, 2, dtype=np.float32) / np.float32(half))
    ang_r = row[:, None] * inv_freq
    ang_c = col[:, None] * inv_freq
    cr, sr, cc, sc = np.cos(ang_r), np.sin(ang_r), np.cos(ang_c), np.sin(ang_c)
    cos = np.concatenate([cr, cr, cc, cc], axis=-1).astype(np.float32)
    sin_signed = np.concatenate([-sr, sr, -sc, sc], axis=-1).astype(np.float32)
    return jnp.asarray(cos), jnp.asarray(sin_signed)


def _t5_bucket(rel):
    nb = N_BUCKETS // 2
    ret = jnp.where(rel > 0, nb, 0)
    n = jnp.abs(rel)
    max_exact = nb // 2
    nf = jnp.maximum(n, 1).astype(F32)
    large = max_exact + (jnp.log(nf / max_exact) / math.log(MAX_DISTANCE / max_exact)
                         * (nb - max_exact)).astype(jnp.int32)
    large = jnp.minimum(large, nb - 1)
    return ret + jnp.where(n < max_exact, n, large)


def _window_bias_lines(rel_bias_table):
    rel = jnp.arange(WA_BAND, dtype=jnp.int32) - WINDOW
    line = rel_bias_table[_t5_bucket(rel)].astype(F32).T * LOG2_E
    line = jnp.where((jnp.abs(rel) <= WINDOW)[None, :], line, NEG_INF)
    return jnp.broadcast_to(line[:, None, :], (line.shape[0], F32_SUBLANES, WA_BAND))


def kernel(x, p, attn_norm_g, w_in, q_norm_g, k_norm_g, sink_logits, w_out, mlp_norm_g, w_up, w_down, ple_w,
           ple_norm_g, gate_norm_g, w_gate, rel_bias_table, final_norm_g):
    batch, seq, d = x.shape
    assert w_in.shape[0] == 1, "gate_final fuses the final RMSNorm, which is only valid for a single layer"
    row = lambda v: v.reshape(1, -1).astype(F32)
    cos, sin_signed = _rope_tables(seq)
    h = x.reshape(batch * seq, d)
    proj = _in_proj(h, row(attn_norm_g), w_in[0].astype(BF16), cos, sin_signed, row(q_norm_g), row(k_norm_g), seq)
    oa, (wu, wd, wo, wg) = _global_attn(proj, (w_up, w_down, w_out, w_gate), batch, seq)
    ob = _window_attn(proj, _window_bias_lines(rel_bias_table), sink_logits.reshape(-1).astype(F32), batch, seq)
    h1, mn = _out_proj(oa, ob, wo, h, row(mlp_norm_g))
    delta = _mlp(mn, wu, wd)
    out = _gate_final(h1, delta, wg, p.reshape(batch * seq, -1), ple_w[0].astype(BF16),
                      row(gate_norm_g), row(ple_norm_g), row(final_norm_g))
    return out.reshape(batch, seq, d)
```

```python
import functools
import math

import jax
import jax.numpy as jnp
import numpy as np
from jax import lax
from jax.experimental import pallas as pl
from jax.experimental.pallas import tpu as pltpu

HEAD_DIM = 128
N_HEADS_A = 8
N_KV_A = 2
N_HEADS_B = 8
N_KV_B = 2
GROUP = 4
GRID_W = 64
WINDOW = 128
N_BUCKETS = 32
MAX_DISTANCE = 128
ROPE_THETA = 10000.0
EPS = 1e-6
NEG_INF = -1e30
LOG2_E = math.log2(math.e)
Q_SCALE = HEAD_DIM ** -0.5 * LOG2_E

V7X_VMEM_BYTES = 64 * 1024 * 1024
VMEM_RESERVE_BYTES = 8 * 1024 * 1024
VMEM_LIMIT_BYTES = V7X_VMEM_BYTES - VMEM_RESERVE_BYTES
F32_SUBLANES = 8
BF16_SUBLANES = 16

SUB_ROWS = 256

BF16 = jnp.bfloat16
F32 = jnp.float32


def _params(*semantics):
    return pltpu.CompilerParams(dimension_semantics=semantics, vmem_limit_bytes=VMEM_LIMIT_BYTES)


def _rms(x, g):
    return x * lax.rsqrt(jnp.mean(x * x, axis=-1, keepdims=True) + EPS) * g


IN_TM = 512
IN_TN = 512

_PLAIN, _Q_A, _K_A, _Q_B = range(4)
_HEAD_KINDS = ([_Q_A] * N_HEADS_A + [_K_A] * N_KV_A + [_PLAIN] * N_KV_A
               + [_Q_B] * N_HEADS_B + [_PLAIN] * (2 * N_KV_B))


def _rope(y, cos, sin_signed):
    lane = lax.broadcasted_iota(jnp.int32, y.shape, 1)
    partner = jnp.where((lane % 64) < 32, pltpu.roll(y, 96, 1), pltpu.roll(y, 32, 1))
    return y * cos + partner * sin_signed


def _in_proj_kernel(x_ref, g_ref, w_ref, cos_ref, sin_ref, gq_ref, gk_ref, o_ref):
    heads_per_dot = IN_TN // HEAD_DIM
    for r in range(IN_TM // SUB_ROWS):
        rows = pl.ds(r * SUB_ROWS, SUB_ROWS)
        u = _rms(x_ref[rows, :], g_ref[...]).astype(BF16)
        cos, sin_signed = cos_ref[rows, :], sin_ref[rows, :]
        for c in range(w_ref.shape[1] // IN_TN):
            acc = jnp.dot(u, w_ref[:, c * IN_TN:(c + 1) * IN_TN], preferred_element_type=F32)
            for hh in range(heads_per_dot):
                head = c * heads_per_dot + hh
                a = acc[:, hh * HEAD_DIM:(hh + 1) * HEAD_DIM]
                kind = _HEAD_KINDS[head]
                if kind == _Q_A:
                    a = _rope(_rms(a, gq_ref[...]), cos, sin_signed) * Q_SCALE
                elif kind == _K_A:
                    a = _rope(_rms(a, gk_ref[...]), cos, sin_signed)
                elif kind == _Q_B:
                    a = a * Q_SCALE
                o_ref[rows, head * HEAD_DIM:(head + 1) * HEAD_DIM] = a.astype(BF16)


def _in_proj(x2, g, w, cos, sin_signed, gq, gk, seq):
    m, d = x2.shape
    n = w.shape[1]
    assert n == len(_HEAD_KINDS) * HEAD_DIM
    pos_tiles = seq // IN_TM
    return pl.pallas_call(
        _in_proj_kernel,
        grid=(m // IN_TM,),
        in_specs=[
            pl.BlockSpec((IN_TM, d), lambda i: (i, 0)),
            pl.BlockSpec((1, d), lambda i: (0, 0)),
            pl.BlockSpec((d, n), lambda i: (0, 0)),
            pl.BlockSpec((IN_TM, HEAD_DIM), lambda i: (i % pos_tiles, 0)),
            pl.BlockSpec((IN_TM, HEAD_DIM), lambda i: (i % pos_tiles, 0)),
            pl.BlockSpec((1, HEAD_DIM), lambda i: (0, 0)),
            pl.BlockSpec((1, HEAD_DIM), lambda i: (0, 0)),
        ],
        out_specs=pl.BlockSpec((IN_TM, n), lambda i: (i, 0)),
        out_shape=jax.ShapeDtypeStruct((m, n), BF16),
        compiler_params=_params("parallel"),
        name="in_proj",
    )(x2, g, w, cos, sin_signed, gq, gk)


GA_ROWS = 256
GA_KEYS = 512


def _global_attn_kernel(*refs, n_cast):
    q_ref, k_ref, v_ref = refs[:3]
    w_f32 = refs[3:3 + n_cast]
    o_ref = refs[3 + n_cast]
    w_bf16 = refs[4 + n_cast:4 + 2 * n_cast]
    v1_ref, kt_ref, s_ref, m_ref = refs[4 + 2 * n_cast:8 + 2 * n_cast]
    in_bufs = refs[8 + 2 * n_cast:8 + 3 * n_cast]
    out_bufs = refs[8 + 3 * n_cast:8 + 4 * n_cast]
    in_sem, out_sem = refs[8 + 4 * n_cast:]

    seq = q_ref.shape[0]
    n_blocks = seq // GA_ROWS
    first_chunk = (pl.program_id(0) * pl.num_programs(1) + pl.program_id(1)) * n_blocks

    def in_copy(w, chunk, slot):
        rows = in_bufs[w].shape[1]
        src = w_f32[w].at[0, pl.ds(pl.multiple_of(chunk * rows, rows), rows)]
        return pltpu.make_async_copy(src, in_bufs[w].at[slot], in_sem.at[w, slot])

    def out_copy(w, chunk):
        rows = out_bufs[w].shape[0]
        dst = w_bf16[w].at[pl.ds(pl.multiple_of(chunk * rows, rows), rows)]
        return pltpu.make_async_copy(out_bufs[w], dst, out_sem.at[w])

    def cast_chunk(r, first, last):
        chunk = first_chunk + r
        slot = r % 2
        for w in range(n_cast):
            in_copy(w, chunk, slot).wait()
            if not last:
                in_copy(w, chunk + 1, 1 - slot).start()
            if not first:
                out_copy(w, chunk - 1).wait()
            out_bufs[w][...] = in_bufs[w][slot].astype(BF16)
            out_copy(w, chunk).start()

    for w in range(n_cast):
        in_copy(w, first_chunk, 0).start()

    v1_ref[:, :HEAD_DIM] = v_ref[...]
    v1_ref[:, HEAD_DIM:] = jnp.ones(v_ref.shape, BF16)
    kt_ref[...] = k_ref[...].T

    def rows_of(r):
        return pl.ds(pl.multiple_of(r * GA_ROWS, GA_ROWS), GA_ROWS)

    def scores(r, g, slot):
        q = q_ref[rows_of(r), g * HEAD_DIM:(g + 1) * HEAD_DIM]
        part = None
        for c in range(seq // GA_KEYS):
            keys = slice(c * GA_KEYS, (c + 1) * GA_KEYS)
            s = jnp.dot(q, kt_ref[:, keys], preferred_element_type=F32)
            s_ref[slot, :, keys] = s
            for l in range(GA_KEYS // HEAD_DIM):
                piece = s[:, l * HEAD_DIM:(l + 1) * HEAD_DIM]
                part = piece if part is None else jnp.maximum(part, piece)
        m_ref[slot] = jnp.broadcast_to(jnp.max(part, axis=-1, keepdims=True), (GA_ROWS, HEAD_DIM))

    def apply(r, g, slot):
        m = m_ref[slot]
        o = jnp.zeros((GA_ROWS, 2 * HEAD_DIM), F32)
        for c in range(seq // GA_KEYS):
            keys = slice(c * GA_KEYS, (c + 1) * GA_KEYS)
            p = jnp.concatenate([jnp.exp2(s_ref[slot, :, c * GA_KEYS + l * HEAD_DIM:c * GA_KEYS + (l + 1) * HEAD_DIM] - m)
                                 for l in range(GA_KEYS // HEAD_DIM)], axis=1).astype(BF16)
            o = o + jnp.dot(p, v1_ref[keys, :], preferred_element_type=F32)
        o_ref[rows_of(r), g * HEAD_DIM:(g + 1) * HEAD_DIM] = (o[:, :HEAD_DIM] / o[:, HEAD_DIM:]).astype(BF16)

    def row_block(r, first, last):
        for g in range(GROUP):
            slot = g % 2
            if g + 1 < GROUP:
                scores(r, g + 1, 1 - slot)
            elif not last:
                scores(r + 1, 0, 1 - slot)
            apply(r, g, slot)
        cast_chunk(r, first, last)

    scores(0, 0, 0)
    row_block(0, first=True, last=False)

    def body(r, carry):
        row_block(r, first=False, last=False)
        return carry

    lax.fori_loop(1, n_blocks - 1, body, 0)
    row_block(n_blocks - 1, first=False, last=True)
    for w in range(n_cast):
        out_copy(w, first_chunk + n_blocks - 1).wait()


def _global_attn(proj, weights, batch, seq):
    assert GROUP % 2 == 0, "score slots alternate per head and must line up across row blocks"
    n_blocks = seq // GA_ROWS
    assert n_blocks >= 3 and n_blocks % 2 == 0
    n_chunks = batch * N_KV_A * n_blocks
    chunk_rows = []
    for w in weights:
        assert w.shape[0] == 1 and w.shape[1] % (BF16_SUBLANES * n_chunks) == 0, w.shape
        chunk_rows.append(w.shape[1] // n_chunks)
    n_cast = len(weights)
    k_col = N_HEADS_A
    v_col = N_HEADS_A + N_KV_A
    any_spec = pl.BlockSpec(memory_space=pl.ANY)
    attn_spec = pl.BlockSpec((seq, GROUP * HEAD_DIM), lambda b, k: (b, k))
    outs = pl.pallas_call(
        functools.partial(_global_attn_kernel, n_cast=n_cast),
        grid=(batch, N_KV_A),
        in_specs=[
            attn_spec,
            pl.BlockSpec((seq, HEAD_DIM), lambda b, k: (b, k_col + k)),
            pl.BlockSpec((seq, HEAD_DIM), lambda b, k: (b, v_col + k)),
        ] + [any_spec] * n_cast,
        out_specs=[attn_spec] + [any_spec] * n_cast,
        out_shape=[jax.ShapeDtypeStruct((batch * seq, N_HEADS_A * HEAD_DIM), BF16)]
        + [jax.ShapeDtypeStruct(w.shape[1:], BF16) for w in weights],
        scratch_shapes=[
            pltpu.VMEM((seq, 2 * HEAD_DIM), BF16),
            pltpu.VMEM((HEAD_DIM, seq), BF16),
            pltpu.VMEM((2, GA_ROWS, seq), F32),
            pltpu.VMEM((2, GA_ROWS, HEAD_DIM), F32),
        ] + [pltpu.VMEM((2, rows, w.shape[2]), F32) for rows, w in zip(chunk_rows, weights)]
        + [pltpu.VMEM((rows, w.shape[2]), BF16) for rows, w in zip(chunk_rows, weights)]
        + [pltpu.SemaphoreType.DMA((n_cast, 2)), pltpu.SemaphoreType.DMA((n_cast,))],
        compiler_params=_params("arbitrary", "arbitrary"),
        name="global_attn",
    )(proj, proj, proj, *weights)
    return outs[0], outs[1:]


WA_ROWS = 256
WA_BAND = WA_ROWS + 2 * WINDOW


def _window_attn_kernel(sink_ref, q_ref, k_ref, v_ref, line_ref, o_ref,
                        kpad_ref, v1pad_ref, bias_ref, sink_ref_b, s_ref, m_ref):
    kvh = pl.program_id(1)
    seq = q_ref.shape[0]
    n_stages = seq // WA_ROWS

    zeros = jnp.zeros((WINDOW, 2 * HEAD_DIM), BF16)
    kpad_ref[:WINDOW, :] = zeros[:, :HEAD_DIM]
    kpad_ref[WINDOW + seq:, :] = zeros[:, :HEAD_DIM]
    kpad_ref[WINDOW:WINDOW + seq, :] = k_ref[...]
    v1pad_ref[:WINDOW, :] = zeros
    v1pad_ref[WINDOW + seq:, :] = zeros
    v1pad_ref[WINDOW:WINDOW + seq, :HEAD_DIM] = v_ref[...]
    v1pad_ref[WINDOW:WINDOW + seq, HEAD_DIM:] = jnp.ones(v_ref.shape, BF16)

    for g in range(GROUP):
        head = kvh * GROUP + g
        tile = jnp.broadcast_to(line_ref[g, :1, :], (WA_ROWS, WA_BAND))
        bias_ref[g * WA_ROWS:(g + 1) * WA_ROWS, :] = pltpu.roll(tile, 0, 1, stride=1, stride_axis=0)
        sink_ref_b[g * WA_ROWS:(g + 1) * WA_ROWS, :] = jnp.full((WA_ROWS, HEAD_DIM), sink_ref[head] * LOG2_E, F32)

    col = lax.broadcasted_iota(jnp.int32, (GROUP * WA_ROWS, WA_BAND), 1)

    def rows_of(n):
        return pl.ds(pl.multiple_of(n * WA_ROWS, WA_ROWS), WA_ROWS)

    def band_of(n):
        return pl.ds(pl.multiple_of(n * WA_ROWS, WA_ROWS), WA_BAND)

    def scores(n, slot):
        q = jnp.concatenate([q_ref[rows_of(n), g * HEAD_DIM:(g + 1) * HEAD_DIM] for g in range(GROUP)], axis=0)
        s = lax.dot_general(q, kpad_ref[band_of(n), :], (((1,), (1,)), ((), ())), preferred_element_type=F32)
        in_range = ((col >= WINDOW) | (n > 0)) & ((col < WA_BAND - WINDOW) | (n < n_stages - 1))
        s = jnp.where(in_range, s + bias_ref[...], NEG_INF)
        s_ref[slot] = s
        row_max = jnp.broadcast_to(jnp.max(s, axis=-1, keepdims=True), sink_ref_b.shape)
        m_ref[slot] = jnp.maximum(row_max, sink_ref_b[...])

    def apply(n, slot):
        m = m_ref[slot]
        p = jnp.concatenate([jnp.exp2(s_ref[slot, :, c * HEAD_DIM:(c + 1) * HEAD_DIM] - m)
                             for c in range(WA_BAND // HEAD_DIM)], axis=1).astype(BF16)
        o = jnp.dot(p, v1pad_ref[band_of(n), :], preferred_element_type=F32)
        o = o[:, :HEAD_DIM] / (o[:, HEAD_DIM:] + jnp.exp2(sink_ref_b[...] - m))
        for g in range(GROUP):
            o_ref[rows_of(n), g * HEAD_DIM:(g + 1) * HEAD_DIM] = o[g * WA_ROWS:(g + 1) * WA_ROWS].astype(BF16)

    scores(0, 0)

    def body(i, carry):
        for slot in range(2):
            n = 2 * i + slot
            scores(n + 1, 1 - slot)
            apply(n, slot)
        return carry

    lax.fori_loop(0, n_stages // 2 - 1, body, 0)
    scores(n_stages - 1, 1)
    apply(n_stages - 2, 0)
    apply(n_stages - 1, 1)


def _window_attn(proj, bias_lines, sink, batch, seq):
    assert (seq // WA_ROWS) % 2 == 0, "the stage loop is unrolled by two so score slots are static"
    q_col = (N_HEADS_A + 2 * N_KV_A) // GROUP
    k_col = N_HEADS_A + 2 * N_KV_A + N_HEADS_B
    v_col = k_col + N_KV_B
    grid_spec = pltpu.PrefetchScalarGridSpec(
        num_scalar_prefetch=1,
        grid=(batch, N_KV_B),
        in_specs=[
            pl.BlockSpec((seq, GROUP * HEAD_DIM), lambda b, k, s: (b, q_col + k)),
            pl.BlockSpec((seq, HEAD_DIM), lambda b, k, s: (b, k_col + k)),
            pl.BlockSpec((seq, HEAD_DIM), lambda b, k, s: (b, v_col + k)),
            pl.BlockSpec((GROUP, F32_SUBLANES, WA_BAND), lambda b, k, s: (k, 0, 0)),
        ],
        out_specs=pl.BlockSpec((seq, GROUP * HEAD_DIM), lambda b, k, s: (b, k)),
        scratch_shapes=[
            pltpu.VMEM((seq + 2 * WINDOW, HEAD_DIM), BF16),
            pltpu.VMEM((seq + 2 * WINDOW, 2 * HEAD_DIM), BF16),
            pltpu.VMEM((GROUP * WA_ROWS, WA_BAND), F32),
            pltpu.VMEM((GROUP * WA_ROWS, HEAD_DIM), F32),
            pltpu.VMEM((2, GROUP * WA_ROWS, WA_BAND), F32),
            pltpu.VMEM((2, GROUP * WA_ROWS, HEAD_DIM), F32),
        ],
    )
    return pl.pallas_call(
        _window_attn_kernel,
        grid_spec=grid_spec,
        out_shape=jax.ShapeDtypeStruct((batch * seq, N_HEADS_B * HEAD_DIM), BF16),
        compiler_params=_params("parallel", "parallel"),
        name="window_attn",
    )(sink, proj, proj, proj, bias_lines)


OUT_TM = 512


def _out_proj_kernel(oa_ref, ob_ref, w_ref, x_ref, g_ref, h_ref, m_ref):
    ka = oa_ref.shape[1]
    for r in range(OUT_TM // SUB_ROWS):
        rows = pl.ds(r * SUB_ROWS, SUB_ROWS)
        h = x_ref[rows, :]
        h = h + jnp.dot(oa_ref[rows, :], w_ref[:ka, :], preferred_element_type=F32)
        h = h + jnp.dot(ob_ref[rows, :], w_ref[ka:, :], preferred_element_type=F32)
        h_ref[rows, :] = h
        m_ref[rows, :] = _rms(h, g_ref[...]).astype(BF16)


def _out_proj(oa, ob, w, x2, g):
    m, d = x2.shape
    ka, kb = oa.shape[1], ob.shape[1]
    return pl.pallas_call(
        _out_proj_kernel,
        grid=(m // OUT_TM,),
        in_specs=[
            pl.BlockSpec((OUT_TM, ka), lambda i: (i, 0)),
            pl.BlockSpec((OUT_TM, kb), lambda i: (i, 0)),
            pl.BlockSpec((ka + kb, d), lambda i: (0, 0)),
            pl.BlockSpec((OUT_TM, d), lambda i: (i, 0)),
            pl.BlockSpec((1, d), lambda i: (0, 0)),
        ],
        out_specs=[pl.BlockSpec((OUT_TM, d), lambda i: (i, 0)), pl.BlockSpec((OUT_TM, d), lambda i: (i, 0))],
        out_shape=[jax.ShapeDtypeStruct((m, d), F32), jax.ShapeDtypeStruct((m, d), BF16)],
        compiler_params=_params("parallel"),
        name="out_proj",
    )(oa, ob, w, x2, g)


MLP_TM = 1024
MLP_TF = 1024


def _mlp_kernel(m_ref, wu_ref, wd_ref, d_ref):
    @pl.when(pl.program_id(1) == 0)
    def _():
        d_ref[...] = jnp.zeros(d_ref.shape, F32)

    a = jnp.dot(m_ref[...], wu_ref[...], preferred_element_type=F32)
    act = jnp.square(jnp.maximum(a, 0.0)).astype(BF16)
    d_ref[...] += jnp.dot(act, wd_ref[...], preferred_element_type=F32)


def _mlp(mn, wu, wd):
    m, d = mn.shape
    dff = wu.shape[1]
    return pl.pallas_call(
        _mlp_kernel,
        grid=(m // MLP_TM, dff // MLP_TF),
        in_specs=[
            pl.BlockSpec((MLP_TM, d), lambda i, f: (i, 0)),
            pl.BlockSpec((d, MLP_TF), lambda i, f: (0, f)),
            pl.BlockSpec((MLP_TF, d), lambda i, f: (f, 0)),
        ],
        out_specs=pl.BlockSpec((MLP_TM, d), lambda i, f: (i, 0)),
        out_shape=jax.ShapeDtypeStruct((m, d), F32),
        compiler_params=_params("parallel", "arbitrary"),
        name="mlp",
    )(mn, wu, wd)


GATE_TM = 512


def _gate_final_kernel(h_ref, d_ref, wg_ref, p_ref, wp_ref, gg_ref, gp_ref, gf_ref, o_ref):
    for r in range(GATE_TM // SUB_ROWS):
        rows = pl.ds(r * SUB_ROWS, SUB_ROWS)
        h = h_ref[rows, :] + d_ref[rows, :]
        gate = jax.nn.sigmoid(jnp.dot(_rms(h, gg_ref[...]).astype(BF16), wg_ref[...], preferred_element_type=F32))
        e = _rms(jnp.dot(p_ref[rows, :].astype(BF16), wp_ref[...], preferred_element_type=F32), gp_ref[...])
        o_ref[rows, :] = _rms(h + gate * e, gf_ref[...])


def _gate_final(h, delta, wg, p2, wp, gg, gp, gf):
    m, d = h.shape
    dp = p2.shape[1]
    tile = pl.BlockSpec((GATE_TM, d), lambda i: (i, 0))
    gain = pl.BlockSpec((1, d), lambda i: (0, 0))
    return pl.pallas_call(
        _gate_final_kernel,
        grid=(m // GATE_TM,),
        in_specs=[
            tile,
            tile,
            pl.BlockSpec((d, d), lambda i: (0, 0)),
            pl.BlockSpec((GATE_TM, dp), lambda i: (i, 0)),
            pl.BlockSpec((dp, d), lambda i: (0, 0)),
            gain, gain, gain,
        ],
        out_specs=tile,
        out_shape=jax.ShapeDtypeStruct((m, d), F32),
        compiler_params=_params("parallel"),
        name="gate_final",
    )(h, delta, wg, p2, wp, gg, gp, gf)


def _rope_tables(seq):
    rows = seq // GRID_W
    row = np.repeat(np.arange(rows, dtype=np.float32), GRID_W)
    col = np.tile(np.arange(GRID_W, dtype=np.float32), rows)
    half = HEAD_DIM // 2
    inv_freq = np.float32(ROPE_THETA) ** (-np.arange(0, half, 2, dtype=np.float32) / np.float32(half))
    ang_r = row[:, None] * inv_freq
    ang_c = col[:, None] * inv_freq
    cr, sr, cc, sc = np.cos(ang_r), np.sin(ang_r), np.cos(ang_c), np.sin(ang_c)
    cos = np.concatenate([cr, cr, cc, cc], axis=-1).astype(np.float32)
    sin_signed = np.concatenate([-sr, sr, -sc, sc], axis=-1).astype(np.float32)
    return jnp.asarray(cos), jnp.asarray(sin_signed)


def _t5_bucket(rel):
    nb = N_BUCKETS // 2
    ret = jnp.where(rel > 0, nb, 0)
    n = jnp.abs(rel)
    max_exact = nb // 2
    nf = jnp.maximum(n, 1).astype(F32)
    large = max_exact + (jnp.log(nf / max_exact) / math.log(MAX_DISTANCE / max_exact)
                         * (nb - max_exact)).astype(jnp.int32)
    large = jnp.minimum(large, nb - 1)
    return ret + jnp.where(n < max_exact, n, large)


def _window_bias_lines(rel_bias_table):
    rel = jnp.arange(WA_BAND, dtype=jnp.int32) - WINDOW
    line = rel_bias_table[_t5_bucket(rel)].astype(F32).T * LOG2_E
    line = jnp.where((jnp.abs(rel) <= WINDOW)[None, :], line, NEG_INF)
    return jnp.broadcast_to(line[:, None, :], (line.shape[0], F32_SUBLANES, WA_BAND))


def kernel(x, p, attn_norm_g, w_in, q_norm_g, k_norm_g, sink_logits, w_out, mlp_norm_g, w_up, w_down, ple_w,
           ple_norm_g, gate_norm_g, w_gate, rel_bias_table, final_norm_g):
    batch, seq, d = x.shape
    assert w_in.shape[0] == 1, "gate_final fuses the final RMSNorm, which is only valid for a single layer"
    row = lambda v: v.reshape(1, -1).astype(F32)
    cos, sin_signed = _rope_tables(seq)
    h = x.reshape(batch * seq, d)
    proj = _in_proj(h, row(attn_norm_g), w_in[0].astype(BF16), cos, sin_signed, row(q_norm_g), row(k_norm_g), seq)
    oa, (wu, wd, wo, wg) = _global_attn(proj, (w_up, w_down, w_out, w_gate), batch, seq)
    ob = _window_attn(proj, _window_bias_lines(rel_bias_table), sink_logits.reshape(-1).astype(F32), batch, seq)
    h1, mn = _out_proj(oa, ob, wo, h, row(mlp_norm_g))
    delta = _mlp(mn, wu, wd)
    out = _gate_final(h1, delta, wg, p.reshape(batch * seq, -1), ple_w[0].astype(BF16),
                      row(gate_norm_g), row(ple_norm_g), row(final_norm_g))
    return out.reshape(batch, seq, d)
```

```python
import functools
import math

import jax
import jax.numpy as jnp
import numpy as np
from jax import lax
from jax.experimental import pallas as pl
from jax.experimental.pallas import tpu as pltpu

HEAD_DIM = 128
N_HEADS_A = 8
N_KV_A = 2
N_HEADS_B = 8
N_KV_B = 2
GROUP = 4
GRID_W = 64
WINDOW = 128
N_BUCKETS = 32
MAX_DISTANCE = 128
ROPE_THETA = 10000.0
EPS = 1e-6
NEG_INF = -1e30
LOG2_E = math.log2(math.e)
Q_SCALE = HEAD_DIM ** -0.5 * LOG2_E

V7X_VMEM_BYTES = 64 * 1024 * 1024
VMEM_RESERVE_BYTES = 8 * 1024 * 1024
VMEM_LIMIT_BYTES = V7X_VMEM_BYTES - VMEM_RESERVE_BYTES
F32_SUBLANES = 8
BF16_SUBLANES = 16

SUB_ROWS = 256

BF16 = jnp.bfloat16
F32 = jnp.float32


def _params(*semantics):
    return pltpu.CompilerParams(dimension_semantics=semantics, vmem_limit_bytes=VMEM_LIMIT_BYTES)


def _rms(x, g):
    return x * lax.rsqrt(jnp.mean(x * x, axis=-1, keepdims=True) + EPS) * g


IN_TM = 512
IN_TN = 512

_PLAIN, _Q_A, _K_A, _Q_B = range(4)
_HEAD_KINDS = ([_Q_A] * N_HEADS_A + [_K_A] * N_KV_A + [_PLAIN] * N_KV_A
               + [_Q_B] * N_HEADS_B + [_PLAIN] * (2 * N_KV_B))


def _rope(y, cos, sin_signed):
    lane = lax.broadcasted_iota(jnp.int32, y.shape, 1)
    partner = jnp.where((lane % 64) < 32, pltpu.roll(y, 96, 1), pltpu.roll(y, 32, 1))
    return y * cos + partner * sin_signed


def _in_proj_kernel(x_ref, g_ref, w_ref, cos_ref, sin_ref, gq_ref, gk_ref, o_ref):
    heads_per_dot = IN_TN // HEAD_DIM
    for r in range(IN_TM // SUB_ROWS):
        rows = pl.ds(r * SUB_ROWS, SUB_ROWS)
        u = _rms(x_ref[rows, :], g_ref[...]).astype(BF16)
        cos, sin_signed = cos_ref[rows, :], sin_ref[rows, :]
        for c in range(w_ref.shape[1] // IN_TN):
            acc = jnp.dot(u, w_ref[:, c * IN_TN:(c + 1) * IN_TN], preferred_element_type=F32)
            for hh in range(heads_per_dot):
                head = c * heads_per_dot + hh
                a = acc[:, hh * HEAD_DIM:(hh + 1) * HEAD_DIM]
                kind = _HEAD_KINDS[head]
                if kind == _Q_A:
                    a = _rope(_rms(a, gq_ref[...]), cos, sin_signed) * Q_SCALE
                elif kind == _K_A:
                    a = _rope(_rms(a, gk_ref[...]), cos, sin_signed)
                elif kind == _Q_B:
                    a = a * Q_SCALE
                o_ref[rows, head * HEAD_DIM:(head + 1) * HEAD_DIM] = a.astype(BF16)


def _in_proj(x2, g, w, cos, sin_signed, gq, gk, seq):
    m, d = x2.shape
    n = w.shape[1]
    assert n == len(_HEAD_KINDS) * HEAD_DIM
    pos_tiles = seq // IN_TM
    return pl.pallas_call(
        _in_proj_kernel,
        grid=(m // IN_TM,),
        in_specs=[
            pl.BlockSpec((IN_TM, d), lambda i: (i, 0)),
            pl.BlockSpec((1, d), lambda i: (0, 0)),
            pl.BlockSpec((d, n), lambda i: (0, 0)),
            pl.BlockSpec((IN_TM, HEAD_DIM), lambda i: (i % pos_tiles, 0)),
            pl.BlockSpec((IN_TM, HEAD_DIM), lambda i: (i % pos_tiles, 0)),
            pl.BlockSpec((1, HEAD_DIM), lambda i: (0, 0)),
            pl.BlockSpec((1, HEAD_DIM), lambda i: (0, 0)),
        ],
        out_specs=pl.BlockSpec((IN_TM, n), lambda i: (i, 0)),
        out_shape=jax.ShapeDtypeStruct((m, n), BF16),
        compiler_params=_params("parallel"),
        name="in_proj",
    )(x2, g, w, cos, sin_signed, gq, gk)


GA_ROWS = 512
GA_KEYS = 512


def _global_attn_kernel(*refs, n_cast):
    q_ref, k_ref, v_ref = refs[:3]
    w_f32 = refs[3:3 + n_cast]
    o_ref = refs[3 + n_cast]
    w_bf16 = refs[4 + n_cast:4 + 2 * n_cast]
    v1_ref, kt_ref, s_ref, m_ref = refs[4 + 2 * n_cast:8 + 2 * n_cast]
    in_bufs = refs[8 + 2 * n_cast:8 + 3 * n_cast]
    out_bufs = refs[8 + 3 * n_cast:8 + 4 * n_cast]
    in_sem, out_sem = refs[8 + 4 * n_cast:]

    seq = q_ref.shape[0]
    n_blocks = seq // GA_ROWS
    first_chunk = (pl.program_id(0) * pl.num_programs(1) + pl.program_id(1)) * n_blocks

    def in_copy(w, chunk, slot):
        rows = in_bufs[w].shape[1]
        src = w_f32[w].at[0, pl.ds(pl.multiple_of(chunk * rows, rows), rows)]
        return pltpu.make_async_copy(src, in_bufs[w].at[slot], in_sem.at[w, slot])

    def out_copy(w, chunk):
        rows = out_bufs[w].shape[0]
        dst = w_bf16[w].at[pl.ds(pl.multiple_of(chunk * rows, rows), rows)]
        return pltpu.make_async_copy(out_bufs[w], dst, out_sem.at[w])

    def cast_chunk(r, first, last):
        chunk = first_chunk + r
        slot = r % 2
        for w in range(n_cast):
            in_copy(w, chunk, slot).wait()
            if not last:
                in_copy(w, chunk + 1, 1 - slot).start()
            if not first:
                out_copy(w, chunk - 1).wait()
            out_bufs[w][...] = in_bufs[w][slot].astype(BF16)
            out_copy(w, chunk).start()

    for w in range(n_cast):
        in_copy(w, first_chunk, 0).start()

    v1_ref[:, :HEAD_DIM] = v_ref[...]
    v1_ref[:, HEAD_DIM:] = jnp.ones(v_ref.shape, BF16)
    kt_ref[...] = k_ref[...].T

    def rows_of(r):
        return pl.ds(pl.multiple_of(r * GA_ROWS, GA_ROWS), GA_ROWS)

    def scores(r, g, slot):
        q = q_ref[rows_of(r), g * HEAD_DIM:(g + 1) * HEAD_DIM]
        part = None
        for c in range(seq // GA_KEYS):
            keys = slice(c * GA_KEYS, (c + 1) * GA_KEYS)
            s = jnp.dot(q, kt_ref[:, keys], preferred_element_type=F32)
            s_ref[slot, :, keys] = s
            for l in range(GA_KEYS // HEAD_DIM):
                piece = s[:, l * HEAD_DIM:(l + 1) * HEAD_DIM]
                part = piece if part is None else jnp.maximum(part, piece)
        m_ref[slot] = jnp.broadcast_to(jnp.max(part, axis=-1, keepdims=True), (GA_ROWS, HEAD_DIM))

    def apply(r, g, slot):
        m = m_ref[slot]
        o = jnp.zeros((GA_ROWS, 2 * HEAD_DIM), F32)
        for c in range(seq // GA_KEYS):
            keys = slice(c * GA_KEYS, (c + 1) * GA_KEYS)
            p = jnp.concatenate([jnp.exp2(s_ref[slot, :, c * GA_KEYS + l * HEAD_DIM:c * GA_KEYS + (l + 1) * HEAD_DIM] - m)
                                 for l in range(GA_KEYS // HEAD_DIM)], axis=1).astype(BF16)
            o = o + jnp.dot(p, v1_ref[keys, :], preferred_element_type=F32)
        o_ref[rows_of(r), g * HEAD_DIM:(g + 1) * HEAD_DIM] = (o[:, :HEAD_DIM] / o[:, HEAD_DIM:]).astype(BF16)

    def row_block(r, first, last):
        for g in range(GROUP):
            slot = g % 2
            if g + 1 < GROUP:
                scores(r, g + 1, 1 - slot)
            elif not last:
                scores(r + 1, 0, 1 - slot)
            apply(r, g, slot)
        cast_chunk(r, first, last)

    scores(0, 0, 0)
    row_block(0, first=True, last=False)

    def body(r, carry):
        row_block(r, first=False, last=False)
        return carry

    lax.fori_loop(1, n_blocks - 1, body, 0)
    row_block(n_blocks - 1, first=False, last=True)
    for w in range(n_cast):
        out_copy(w, first_chunk + n_blocks - 1).wait()


def _global_attn(proj, weights, batch, seq):
    assert GROUP % 2 == 0, "score slots alternate per head and must line up across row blocks"
    n_blocks = seq // GA_ROWS
    assert n_blocks >= 3 and n_blocks % 2 == 0
    n_chunks = batch * N_KV_A * n_blocks
    chunk_rows = []
    for w in weights:
        assert w.shape[0] == 1 and w.shape[1] % (BF16_SUBLANES * n_chunks) == 0, w.shape
        chunk_rows.append(w.shape[1] // n_chunks)
    n_cast = len(weights)
    k_col = N_HEADS_A
    v_col = N_HEADS_A + N_KV_A
    any_spec = pl.BlockSpec(memory_space=pl.ANY)
    attn_spec = pl.BlockSpec((seq, GROUP * HEAD_DIM), lambda b, k: (b, k))
    outs = pl.pallas_call(
        functools.partial(_global_attn_kernel, n_cast=n_cast),
        grid=(batch, N_KV_A),
        in_specs=[
            attn_spec,
            pl.BlockSpec((seq, HEAD_DIM), lambda b, k: (b, k_col + k)),
            pl.BlockSpec((seq, HEAD_DIM), lambda b, k: (b, v_col + k)),
        ] + [any_spec] * n_cast,
        out_specs=[attn_spec] + [any_spec] * n_cast,
        out_shape=[jax.ShapeDtypeStruct((batch * seq, N_HEADS_A * HEAD_DIM), BF16)]
        + [jax.ShapeDtypeStruct(w.shape[1:], BF16) for w in weights],
        scratch_shapes=[
            pltpu.VMEM((seq, 2 * HEAD_DIM), BF16),
            pltpu.VMEM((HEAD_DIM, seq), BF16),
            pltpu.VMEM((2, GA_ROWS, seq), F32),
            pltpu.VMEM((2, GA_ROWS, HEAD_DIM), F32),
        ] + [pltpu.VMEM((2, rows, w.shape[2]), F32) for rows, w in zip(chunk_rows, weights)]
        + [pltpu.VMEM((rows, w.shape[2]), BF16) for rows, w in zip(chunk_rows, weights)]
        + [pltpu.SemaphoreType.DMA((n_cast, 2)), pltpu.SemaphoreType.DMA((n_cast,))],
        compiler_params=_params("arbitrary", "arbitrary"),
        name="global_attn",
    )(proj, proj, proj, *weights)
    return outs[0], outs[1:]


WA_ROWS = 256
WA_BAND = WA_ROWS + 2 * WINDOW


def _window_attn_kernel(sink_ref, q_ref, k_ref, v_ref, line_ref, o_ref,
                        kpad_ref, v1pad_ref, bias_ref, sink_ref_b, s_ref, m_ref):
    kvh = pl.program_id(1)
    seq = q_ref.shape[0]
    n_stages = seq // WA_ROWS

    zeros = jnp.zeros((WINDOW, 2 * HEAD_DIM), BF16)
    kpad_ref[:WINDOW, :] = zeros[:, :HEAD_DIM]
    kpad_ref[WINDOW + seq:, :] = zeros[:, :HEAD_DIM]
    kpad_ref[WINDOW:WINDOW + seq, :] = k_ref[...]
    v1pad_ref[:WINDOW, :] = zeros
    v1pad_ref[WINDOW + seq:, :] = zeros
    v1pad_ref[WINDOW:WINDOW + seq, :HEAD_DIM] = v_ref[...]
    v1pad_ref[WINDOW:WINDOW + seq, HEAD_DIM:] = jnp.ones(v_ref.shape, BF16)

    for g in range(GROUP):
        head = kvh * GROUP + g
        tile = jnp.broadcast_to(line_ref[g, :1, :], (WA_ROWS, WA_BAND))
        bias_ref[g * WA_ROWS:(g + 1) * WA_ROWS, :] = pltpu.roll(tile, 0, 1, stride=1, stride_axis=0)
        sink_ref_b[g * WA_ROWS:(g + 1) * WA_ROWS, :] = jnp.full((WA_ROWS, HEAD_DIM), sink_ref[head] * LOG2_E, F32)

    col = lax.broadcasted_iota(jnp.int32, (GROUP * WA_ROWS, WA_BAND), 1)

    def rows_of(n):
        return pl.ds(pl.multiple_of(n * WA_ROWS, WA_ROWS), WA_ROWS)

    def band_of(n):
        return pl.ds(pl.multiple_of(n * WA_ROWS, WA_ROWS), WA_BAND)

    def scores(n, slot):
        q = jnp.concatenate([q_ref[rows_of(n), g * HEAD_DIM:(g + 1) * HEAD_DIM] for g in range(GROUP)], axis=0)
        s = lax.dot_general(q, kpad_ref[band_of(n), :], (((1,), (1,)), ((), ())), preferred_element_type=F32)
        in_range = ((col >= WINDOW) | (n > 0)) & ((col < WA_BAND - WINDOW) | (n < n_stages - 1))
        s = jnp.where(in_range, s + bias_ref[...], NEG_INF)
        s_ref[slot] = s
        row_max = jnp.broadcast_to(jnp.max(s, axis=-1, keepdims=True), sink_ref_b.shape)
        m_ref[slot] = jnp.maximum(row_max, sink_ref_b[...])

    def apply(n, slot):
        m = m_ref[slot]
        p = jnp.concatenate([jnp.exp2(s_ref[slot, :, c * HEAD_DIM:(c + 1) * HEAD_DIM] - m)
                             for c in range(WA_BAND // HEAD_DIM)], axis=1).astype(BF16)
        o = jnp.dot(p, v1pad_ref[band_of(n), :], preferred_element_type=F32)
        o = o[:, :HEAD_DIM] / (o[:, HEAD_DIM:] + jnp.exp2(sink_ref_b[...] - m))
        for g in range(GROUP):
            o_ref[rows_of(n), g * HEAD_DIM:(g + 1) * HEAD_DIM] = o[g * WA_ROWS:(g + 1) * WA_ROWS].astype(BF16)

    scores(0, 0)

    def body(i, carry):
        for slot in range(2):
            n = 2 * i + slot
            scores(n + 1, 1 - slot)
            apply(n, slot)
        return carry

    lax.fori_loop(0, n_stages // 2 - 1, body, 0)
    scores(n_stages - 1, 1)
    apply(n_stages - 2, 0)
    apply(n_stages - 1, 1)


def _window_attn(proj, bias_lines, sink, batch, seq):
    assert (seq // WA_ROWS) % 2 == 0, "the stage loop is unrolled by two so score slots are static"
    q_col = (N_HEADS_A + 2 * N_KV_A) // GROUP
    k_col = N_HEADS_A + 2 * N_KV_A + N_HEADS_B
    v_col = k_col + N_KV_B
    grid_spec = pltpu.PrefetchScalarGridSpec(
        num_scalar_prefetch=1,
        grid=(batch, N_KV_B),
        in_specs=[
            pl.BlockSpec((seq, GROUP * HEAD_DIM), lambda b, k, s: (b, q_col + k)),
            pl.BlockSpec((seq, HEAD_DIM), lambda b, k, s: (b, k_col + k)),
            pl.BlockSpec((seq, HEAD_DIM), lambda b, k, s: (b, v_col + k)),
            pl.BlockSpec((GROUP, F32_SUBLANES, WA_BAND), lambda b, k, s: (k, 0, 0)),
        ],
        out_specs=pl.BlockSpec((seq, GROUP * HEAD_DIM), lambda b, k, s: (b, k)),
        scratch_shapes=[
            pltpu.VMEM((seq + 2 * WINDOW, HEAD_DIM), BF16),
            pltpu.VMEM((seq + 2 * WINDOW, 2 * HEAD_DIM), BF16),
            pltpu.VMEM((GROUP * WA_ROWS, WA_BAND), F32),
            pltpu.VMEM((GROUP * WA_ROWS, HEAD_DIM), F32),
            pltpu.VMEM((2, GROUP * WA_ROWS, WA_BAND), F32),
            pltpu.VMEM((2, GROUP * WA_ROWS, HEAD_DIM), F32),
        ],
    )
    return pl.pallas_call(
        _window_attn_kernel,
        grid_spec=grid_spec,
        out_shape=jax.ShapeDtypeStruct((batch * seq, N_HEADS_B * HEAD_DIM), BF16),
        compiler_params=_params("parallel", "parallel"),
        name="window_attn",
    )(sink, proj, proj, proj, bias_lines)


OUT_TM = 512


def _out_proj_kernel(oa_ref, ob_ref, w_ref, x_ref, g_ref, h_ref, m_ref):
    ka = oa_ref.shape[1]
    for r in range(OUT_TM // SUB_ROWS):
        rows = pl.ds(r * SUB_ROWS, SUB_ROWS)
        h = x_ref[rows, :]
        h = h + jnp.dot(oa_ref[rows, :], w_ref[:ka, :], preferred_element_type=F32)
        h = h + jnp.dot(ob_ref[rows, :], w_ref[ka:, :], preferred_element_type=F32)
        h_ref[rows, :] = h
        m_ref[rows, :] = _rms(h, g_ref[...]).astype(BF16)


def _out_proj(oa, ob, w, x2, g):
    m, d = x2.shape
    ka, kb = oa.shape[1], ob.shape[1]
    return pl.pallas_call(
        _out_proj_kernel,
        grid=(m // OUT_TM,),
        in_specs=[
            pl.BlockSpec((OUT_TM, ka), lambda i: (i, 0)),
            pl.BlockSpec((OUT_TM, kb), lambda i: (i, 0)),
            pl.BlockSpec((ka + kb, d), lambda i: (0, 0)),
            pl.BlockSpec((OUT_TM, d), lambda i: (i, 0)),
            pl.BlockSpec((1, d), lambda i: (0, 0)),
        ],
        out_specs=[pl.BlockSpec((OUT_TM, d), lambda i: (i, 0)), pl.BlockSpec((OUT_TM, d), lambda i: (i, 0))],
        out_shape=[jax.ShapeDtypeStruct((m, d), F32), jax.ShapeDtypeStruct((m, d), BF16)],
        compiler_params=_params("parallel"),
        name="out_proj",
    )(oa, ob, w, x2, g)


MLP_TM = 1024
MLP_TF = 1024


def _mlp_kernel(m_ref, wu_ref, wd_ref, d_ref):
    @pl.when(pl.program_id(1) == 0)
    def _():
        d_ref[...] = jnp.zeros(d_ref.shape, F32)

    a = jnp.dot(m_ref[...], wu_ref[...], preferred_element_type=F32)
    act = jnp.square(jnp.maximum(a, 0.0)).astype(BF16)
    d_ref[...] += jnp.dot(act, wd_ref[...], preferred_element_type=F32)


def _mlp(mn, wu, wd):
    m, d = mn.shape
    dff = wu.shape[1]
    return pl.pallas_call(
        _mlp_kernel,
        grid=(m // MLP_TM, dff // MLP_TF),
        in_specs=[
            pl.BlockSpec((MLP_TM, d), lambda i, f: (i, 0)),
            pl.BlockSpec((d, MLP_TF), lambda i, f: (0, f)),
            pl.BlockSpec((MLP_TF, d), lambda i, f: (f, 0)),
        ],
        out_specs=pl.BlockSpec((MLP_TM, d), lambda i, f: (i, 0)),
        out_shape=jax.ShapeDtypeStruct((m, d), F32),
        compiler_params=_params("parallel", "arbitrary"),
        name="mlp",
    )(mn, wu, wd)


GATE_TM = 512


def _gate_final_kernel(h_ref, d_ref, wg_ref, p_ref, wp_ref, gg_ref, gp_ref, gf_ref, o_ref):
    for r in range(GATE_TM // SUB_ROWS):
        rows = pl.ds(r * SUB_ROWS, SUB_ROWS)
        h = h_ref[rows, :] + d_ref[rows, :]
        gate = jax.nn.sigmoid(jnp.dot(_rms(h, gg_ref[...]).astype(BF16), wg_ref[...], preferred_element_type=F32))
        e = _rms(jnp.dot(p_ref[rows, :].astype(BF16), wp_ref[...], preferred_element_type=F32), gp_ref[...])
        o_ref[rows, :] = _rms(h + gate * e, gf_ref[...])


def _gate_final(h, delta, wg, p2, wp, gg, gp, gf):
    m, d = h.shape
    dp = p2.shape[1]
    tile = pl.BlockSpec((GATE_TM, d), lambda i: (i, 0))
    gain = pl.BlockSpec((1, d), lambda i: (0, 0))
    return pl.pallas_call(
        _gate_final_kernel,
        grid=(m // GATE_TM,),
        in_specs=[
            tile,
            tile,
            pl.BlockSpec((d, d), lambda i: (0, 0)),
            pl.BlockSpec((GATE_TM, dp), lambda i: (i, 0)),
            pl.BlockSpec((dp, d), lambda i: (0, 0)),
            gain, gain, gain,
        ],
        out_specs=tile,
        out_shape=jax.ShapeDtypeStruct((m, d), F32),
        compiler_params=_params("parallel"),
        name="gate_final",
    )(h, delta, wg, p2, wp, gg, gp, gf)


def _rope_tables(seq):
    rows = seq // GRID_W
    row = np.repeat(np.arange(rows, dtype=np.float32), GRID_W)
    col = np.tile(np.arange(GRID_W, dtype=np.float32), rows)
    half = HEAD_DIM // 2
    inv_freq = np.float32(ROPE_THETA) ** (-np.arange(0, half, 2, dtype=np.float32) / np.float32(half))
    ang_r = row[:, None] * inv_freq
    ang_c = col[:, None] * inv_freq
    cr, sr, cc, sc = np.cos(ang_r), np.sin(ang_r), np.cos(ang_c), np.sin(ang_c)
    cos = np.concatenate([cr, cr, cc, cc], axis=-1).astype(np.float32)
    sin_signed = np.concatenate([-sr, sr, -sc, sc], axis=-1).astype(np.float32)
    return jnp.asarray(cos), jnp.asarray(sin_signed)


def _t5_bucket(rel):
    nb = N_BUCKETS // 2
    ret = jnp.where(rel > 0, nb, 0)
    n = jnp.abs(rel)
    max_exact = nb // 2
    nf = jnp.maximum(n, 1).astype(F32)
    large = max_exact + (jnp.log(nf / max_exact) / math.log(MAX_DISTANCE / max_exact)
                         * (nb - max_exact)).astype(jnp.int32)
    large = jnp.minimum(large, nb - 1)
    return ret + jnp.where(n < max_exact, n, large)


def _window_bias_lines(rel_bias_table):
    rel = jnp.arange(WA_BAND, dtype=jnp.int32) - WINDOW
    line = rel_bias_table[_t5_bucket(rel)].astype(F32).T * LOG2_E
    line = jnp.where((jnp.abs(rel) <= WINDOW)[None, :], line, NEG_INF)
    return jnp.broadcast_to(line[:, None, :], (line.shape[0], F32_SUBLANES, WA_BAND))


def kernel(x, p, attn_norm_g, w_in, q_norm_g, k_norm_g, sink_logits, w_out, mlp_norm_g, w_up, w_down, ple_w,
           ple_norm_g, gate_norm_g, w_gate, rel_bias_table, final_norm_g):
    batch, seq, d = x.shape
    assert w_in.shape[0] == 1, "gate_final fuses the final RMSNorm, which is only valid for a single layer"
    row = lambda v: v.reshape(1, -1).astype(F32)
    cos, sin_signed = _rope_tables(seq)
    h = x.reshape(batch * seq, d)
    proj = _in_proj(h, row(attn_norm_g), w_in[0].astype(BF16), cos, sin_signed, row(q_norm_g), row(k_norm_g), seq)
    oa, (wu, wd, wo, wg) = _global_attn(proj, (w_up, w_down, w_out, w_gate), batch, seq)
    ob = _window_attn(proj, _window_bias_lines(rel_bias_table), sink_logits.reshape(-1).astype(F32), batch, seq)
    h1, mn = _out_proj(oa, ob, wo, h, row(mlp_norm_g))
    delta = _mlp(mn, wu, wd)
    out = _gate_final(h1, delta, wg, p.reshape(batch * seq, -1), ple_w[0].astype(BF16),
                      row(gate_norm_g), row(ple_norm_g), row(final_norm_g))
    return out.reshape(batch, seq, d)
```

```python
import functools
import math

import jax
import jax.numpy as jnp
import numpy as np
from jax import lax
from jax.experimental import pallas as pl
from jax.experimental.pallas import tpu as pltpu

HEAD_DIM = 128
N_HEADS_A = 8
N_KV_A = 2
N_HEADS_B = 8
N_KV_B = 2
GROUP = 4
GRID_W = 64
WINDOW = 128
N_BUCKETS = 32
MAX_DISTANCE = 128
ROPE_THETA = 10000.0
EPS = 1e-6
NEG_INF = -1e30
LOG2_E = math.log2(math.e)
Q_SCALE = HEAD_DIM ** -0.5 * LOG2_E

V7X_VMEM_BYTES = 64 * 1024 * 1024
VMEM_RESERVE_BYTES = 8 * 1024 * 1024
VMEM_LIMIT_BYTES = V7X_VMEM_BYTES - VMEM_RESERVE_BYTES
F32_SUBLANES = 8
BF16_SUBLANES = 16

SUB_ROWS = 256

BF16 = jnp.bfloat16
F32 = jnp.float32


def _params(*semantics):
    return pltpu.CompilerParams(dimension_semantics=semantics, vmem_limit_bytes=VMEM_LIMIT_BYTES)


def _rms(x, g):
    return x * lax.rsqrt(jnp.mean(x * x, axis=-1, keepdims=True) + EPS) * g


IN_TM = 512
IN_TN = 512
IN_W_ROWS = 128
IN_W_SLOTS = 4

_PLAIN, _Q_A, _K_A, _Q_B = range(4)
_HEAD_KINDS = ([_Q_A] * N_HEADS_A + [_K_A] * N_KV_A + [_PLAIN] * N_KV_A
               + [_Q_B] * N_HEADS_B + [_PLAIN] * (2 * N_KV_B))


def _rope(y, cos, sin_signed):
    lane = lax.broadcasted_iota(jnp.int32, y.shape, 1)
    partner = jnp.where((lane % 64) < 32, pltpu.roll(y, 96, 1), pltpu.roll(y, 32, 1))
    return y * cos + partner * sin_signed


def _in_proj_kernel(x_ref, g_ref, w_hbm, cos_ref, sin_ref, gq_ref, gk_ref, o_ref, w_ref, wbuf_ref, wsem):
    @pl.when(pl.program_id(0) == 0)
    def _():
        def chunk_copy(c):
            src = w_hbm.at[0, pl.ds(c * IN_W_ROWS, IN_W_ROWS)]
            return pltpu.make_async_copy(src, wbuf_ref.at[c % IN_W_SLOTS], wsem.at[c % IN_W_SLOTS])

        n_chunks = w_ref.shape[0] // IN_W_ROWS
        for c in range(IN_W_SLOTS - 1):
            chunk_copy(c).start()
        for c in range(n_chunks):
            chunk_copy(c).wait()
            if c + IN_W_SLOTS - 1 < n_chunks:
                chunk_copy(c + IN_W_SLOTS - 1).start()
            w_ref[c * IN_W_ROWS:(c + 1) * IN_W_ROWS, :] = wbuf_ref[c % IN_W_SLOTS].astype(BF16)

    heads_per_dot = IN_TN // HEAD_DIM
    for r in range(IN_TM // SUB_ROWS):
        rows = pl.ds(r * SUB_ROWS, SUB_ROWS)
        u = _rms(x_ref[rows, :], g_ref[...]).astype(BF16)
        cos, sin_signed = cos_ref[rows, :], sin_ref[rows, :]
        for c in range(w_ref.shape[1] // IN_TN):
            acc = jnp.dot(u, w_ref[:, c * IN_TN:(c + 1) * IN_TN], preferred_element_type=F32)
            for hh in range(heads_per_dot):
                head = c * heads_per_dot + hh
                a = acc[:, hh * HEAD_DIM:(hh + 1) * HEAD_DIM]
                kind = _HEAD_KINDS[head]
                if kind == _Q_A:
                    a = _rope(_rms(a, gq_ref[...]), cos, sin_signed) * Q_SCALE
                elif kind == _K_A:
                    a = _rope(_rms(a, gk_ref[...]), cos, sin_signed)
                elif kind == _Q_B:
                    a = a * Q_SCALE
                o_ref[rows, head * HEAD_DIM:(head + 1) * HEAD_DIM] = a.astype(BF16)


def _in_proj(x2, g, w, cos, sin_signed, gq, gk, seq):
    m, d = x2.shape
    n = w.shape[2]
    assert w.shape[0] == 1 and n == len(_HEAD_KINDS) * HEAD_DIM and d % IN_W_ROWS == 0
    pos_tiles = seq // IN_TM
    return pl.pallas_call(
        _in_proj_kernel,
        grid=(m // IN_TM,),
        in_specs=[
            pl.BlockSpec((IN_TM, d), lambda i: (i, 0)),
            pl.BlockSpec((1, d), lambda i: (0, 0)),
            pl.BlockSpec(memory_space=pl.ANY),
            pl.BlockSpec((IN_TM, HEAD_DIM), lambda i: (i % pos_tiles, 0)),
            pl.BlockSpec((IN_TM, HEAD_DIM), lambda i: (i % pos_tiles, 0)),
            pl.BlockSpec((1, HEAD_DIM), lambda i: (0, 0)),
            pl.BlockSpec((1, HEAD_DIM), lambda i: (0, 0)),
        ],
        out_specs=pl.BlockSpec((IN_TM, n), lambda i: (i, 0)),
        out_shape=jax.ShapeDtypeStruct((m, n), BF16),
        scratch_shapes=[
            pltpu.VMEM((d, n), BF16),
            pltpu.VMEM((IN_W_SLOTS, IN_W_ROWS, n), F32),
            pltpu.SemaphoreType.DMA((IN_W_SLOTS,)),
        ],
        compiler_params=_params("arbitrary"),
        name="in_proj",
    )(x2, g, w, cos, sin_signed, gq, gk)


GA_ROWS = 256
GA_KEYS = 512
MAX_UNSHIFTED_LOGIT = 64.0


def _global_attn_kernel(*refs, n_cast):
    bound_ref, q_ref, k_ref, v_ref = refs[:4]
    w_f32 = refs[4:4 + n_cast]
    o_ref = refs[4 + n_cast]
    w_bf16 = refs[5 + n_cast:5 + 2 * n_cast]
    v1_ref, kt_ref, s_ref, m_ref = refs[5 + 2 * n_cast:9 + 2 * n_cast]
    in_bufs = refs[9 + 2 * n_cast:9 + 3 * n_cast]
    out_bufs = refs[9 + 3 * n_cast:9 + 4 * n_cast]
    in_sem, out_sem = refs[9 + 4 * n_cast:]

    seq = q_ref.shape[0]
    n_blocks = seq // GA_ROWS
    first_chunk = (pl.program_id(0) * pl.num_programs(1) + pl.program_id(1)) * n_blocks

    def in_copy(w, chunk, slot):
        rows = in_bufs[w].shape[1]
        src = w_f32[w].at[0, pl.ds(pl.multiple_of(chunk * rows, rows), rows)]
        return pltpu.make_async_copy(src, in_bufs[w].at[slot], in_sem.at[w, slot])

    def out_copy(w, chunk):
        rows = out_bufs[w].shape[0]
        dst = w_bf16[w].at[pl.ds(pl.multiple_of(chunk * rows, rows), rows)]
        return pltpu.make_async_copy(out_bufs[w], dst, out_sem.at[w])

    def cast_chunk(r, first, last):
        chunk = first_chunk + r
        slot = r % 2
        for w in range(n_cast):
            in_copy(w, chunk, slot).wait()
            if not last:
                in_copy(w, chunk + 1, 1 - slot).start()
            if not first:
                out_copy(w, chunk - 1).wait()
            out_bufs[w][...] = in_bufs[w][slot].astype(BF16)
            out_copy(w, chunk).start()

    for w in range(n_cast):
        in_copy(w, first_chunk, 0).start()

    v1_ref[:, :HEAD_DIM] = v_ref[...]
    v1_ref[:, HEAD_DIM:] = jnp.ones(v_ref.shape, BF16)
    kt_ref[...] = k_ref[...].T

    def rows_of(r):
        return pl.ds(pl.multiple_of(r * GA_ROWS, GA_ROWS), GA_ROWS)

    def scores(r, g, slot):
        q = q_ref[rows_of(r), g * HEAD_DIM:(g + 1) * HEAD_DIM]
        part = None
        for c in range(seq // GA_KEYS):
            keys = slice(c * GA_KEYS, (c + 1) * GA_KEYS)
            s = jnp.dot(q, kt_ref[:, keys], preferred_element_type=F32)
            s_ref[slot, :, keys] = s
            for l in range(GA_KEYS // HEAD_DIM):
                piece = s[:, l * HEAD_DIM:(l + 1) * HEAD_DIM]
                part = piece if part is None else jnp.maximum(part, piece)
        m_ref[slot] = jnp.broadcast_to(jnp.max(part, axis=-1, keepdims=True), (GA_ROWS, HEAD_DIM))

    def apply(r, g, slot):
        m = m_ref[slot]
        o = jnp.zeros((GA_ROWS, 2 * HEAD_DIM), F32)
        for c in range(seq // GA_KEYS):
            keys = slice(c * GA_KEYS, (c + 1) * GA_KEYS)
            p = jnp.concatenate([jnp.exp2(s_ref[slot, :, c * GA_KEYS + l * HEAD_DIM:c * GA_KEYS + (l + 1) * HEAD_DIM] - m)
                                 for l in range(GA_KEYS // HEAD_DIM)], axis=1).astype(BF16)
            o = o + jnp.dot(p, v1_ref[keys, :], preferred_element_type=F32)
        o_ref[rows_of(r), g * HEAD_DIM:(g + 1) * HEAD_DIM] = (o[:, :HEAD_DIM] / o[:, HEAD_DIM:]).astype(BF16)

    def shifted_row_block(r, first, last):
        for g in range(GROUP):
            slot = g % 2
            if g + 1 < GROUP:
                scores(r, g + 1, 1 - slot)
            elif not last:
                scores(r + 1, 0, 1 - slot)
            apply(r, g, slot)
        cast_chunk(r, first, last)

    def unshifted_row_block(r, first, last):
        for g in range(GROUP):
            q = q_ref[rows_of(r), g * HEAD_DIM:(g + 1) * HEAD_DIM]
            o = jnp.zeros((GA_ROWS, 2 * HEAD_DIM), F32)
            for c in range(seq // GA_KEYS):
                keys = slice(c * GA_KEYS, (c + 1) * GA_KEYS)
                p = jnp.exp2(jnp.dot(q, kt_ref[:, keys], preferred_element_type=F32)).astype(BF16)
                o = o + jnp.dot(p, v1_ref[keys, :], preferred_element_type=F32)
            o_ref[rows_of(r), g * HEAD_DIM:(g + 1) * HEAD_DIM] = (o[:, :HEAD_DIM] / o[:, HEAD_DIM:]).astype(BF16)
        cast_chunk(r, first, last)

    def all_row_blocks(row_block):
        row_block(0, first=True, last=False)

        def body(r, carry):
            row_block(r, first=False, last=False)
            return carry

        lax.fori_loop(1, n_blocks - 1, body, 0)
        row_block(n_blocks - 1, first=False, last=True)

    @pl.when(bound_ref[0] <= MAX_UNSHIFTED_LOGIT)
    def _():
        all_row_blocks(unshifted_row_block)

    @pl.when(bound_ref[0] > MAX_UNSHIFTED_LOGIT)
    def _():
        scores(0, 0, 0)
        all_row_blocks(shifted_row_block)

    for w in range(n_cast):
        out_copy(w, first_chunk + n_blocks - 1).wait()


def _global_attn(proj, logit_bound, weights, batch, seq):
    assert GROUP % 2 == 0, "score slots alternate per head and must line up across row blocks"
    n_blocks = seq // GA_ROWS
    assert n_blocks >= 3 and n_blocks % 2 == 0
    n_chunks = batch * N_KV_A * n_blocks
    chunk_rows = []
    for w in weights:
        assert w.shape[0] == 1 and w.shape[1] % (BF16_SUBLANES * n_chunks) == 0, w.shape
        chunk_rows.append(w.shape[1] // n_chunks)
    n_cast = len(weights)
    k_col = N_HEADS_A
    v_col = N_HEADS_A + N_KV_A
    any_spec = pl.BlockSpec(memory_space=pl.ANY)
    attn_spec = pl.BlockSpec((seq, GROUP * HEAD_DIM), lambda b, k, bound: (b, k))
    grid_spec = pltpu.PrefetchScalarGridSpec(
        num_scalar_prefetch=1,
        grid=(batch, N_KV_A),
        in_specs=[
            attn_spec,
            pl.BlockSpec((seq, HEAD_DIM), lambda b, k, bound: (b, k_col + k)),
            pl.BlockSpec((seq, HEAD_DIM), lambda b, k, bound: (b, v_col + k)),
        ] + [any_spec] * n_cast,
        out_specs=[attn_spec] + [any_spec] * n_cast,
        scratch_shapes=[
            pltpu.VMEM((seq, 2 * HEAD_DIM), BF16),
            pltpu.VMEM((HEAD_DIM, seq), BF16),
            pltpu.VMEM((2, GA_ROWS, seq), F32),
            pltpu.VMEM((2, GA_ROWS, HEAD_DIM), F32),
        ] + [pltpu.VMEM((2, rows, w.shape[2]), F32) for rows, w in zip(chunk_rows, weights)]
        + [pltpu.VMEM((rows, w.shape[2]), BF16) for rows, w in zip(chunk_rows, weights)]
        + [pltpu.SemaphoreType.DMA((n_cast, 2)), pltpu.SemaphoreType.DMA((n_cast,))],
    )
    outs = pl.pallas_call(
        functools.partial(_global_attn_kernel, n_cast=n_cast),
        grid_spec=grid_spec,
        out_shape=[jax.ShapeDtypeStruct((batch * seq, N_HEADS_A * HEAD_DIM), BF16)]
        + [jax.ShapeDtypeStruct(w.shape[1:], BF16) for w in weights],
        compiler_params=_params("arbitrary", "arbitrary"),
        name="global_attn",
    )(logit_bound, proj, proj, proj, *weights)
    return outs[0], outs[1:]


WA_ROWS = 256
WA_BAND = WA_ROWS + 2 * WINDOW


def _window_attn_kernel(sink_ref, q_ref, k_ref, v_ref, line_ref, o_ref,
                        kpad_ref, v1pad_ref, bias_ref, sink_ref_b, s_ref, m_ref):
    kvh = pl.program_id(1)
    seq = q_ref.shape[0]
    n_stages = seq // WA_ROWS

    zeros = jnp.zeros((WINDOW, 2 * HEAD_DIM), BF16)
    kpad_ref[:WINDOW, :] = zeros[:, :HEAD_DIM]
    kpad_ref[WINDOW + seq:, :] = zeros[:, :HEAD_DIM]
    kpad_ref[WINDOW:WINDOW + seq, :] = k_ref[...]
    v1pad_ref[:WINDOW, :] = zeros
    v1pad_ref[WINDOW + seq:, :] = zeros
    v1pad_ref[WINDOW:WINDOW + seq, :HEAD_DIM] = v_ref[...]
    v1pad_ref[WINDOW:WINDOW + seq, HEAD_DIM:] = jnp.ones(v_ref.shape, BF16)

    for g in range(GROUP):
        head = kvh * GROUP + g
        tile = jnp.broadcast_to(line_ref[g, :1, :], (WA_ROWS, WA_BAND))
        bias_ref[g * WA_ROWS:(g + 1) * WA_ROWS, :] = pltpu.roll(tile, 0, 1, stride=1, stride_axis=0)
        sink_ref_b[g * WA_ROWS:(g + 1) * WA_ROWS, :] = jnp.full((WA_ROWS, HEAD_DIM), sink_ref[head] * LOG2_E, F32)

    col = lax.broadcasted_iota(jnp.int32, (GROUP * WA_ROWS, WA_BAND), 1)

    def rows_of(n):
        return pl.ds(pl.multiple_of(n * WA_ROWS, WA_ROWS), WA_ROWS)

    def band_of(n):
        return pl.ds(pl.multiple_of(n * WA_ROWS, WA_ROWS), WA_BAND)

    def scores(n, slot):
        q = jnp.concatenate([q_ref[rows_of(n), g * HEAD_DIM:(g + 1) * HEAD_DIM] for g in range(GROUP)], axis=0)
        s = lax.dot_general(q, kpad_ref[band_of(n), :], (((1,), (1,)), ((), ())), preferred_element_type=F32)
        in_range = ((col >= WINDOW) | (n > 0)) & ((col < WA_BAND - WINDOW) | (n < n_stages - 1))
        s = jnp.where(in_range, s + bias_ref[...], NEG_INF)
        s_ref[slot] = s
        row_max = jnp.broadcast_to(jnp.max(s, axis=-1, keepdims=True), sink_ref_b.shape)
        m_ref[slot] = jnp.maximum(row_max, sink_ref_b[...])

    def apply(n, slot):
        m = m_ref[slot]
        p = jnp.concatenate([jnp.exp2(s_ref[slot, :, c * HEAD_DIM:(c + 1) * HEAD_DIM] - m)
                             for c in range(WA_BAND // HEAD_DIM)], axis=1).astype(BF16)
        o = jnp.dot(p, v1pad_ref[band_of(n), :], preferred_element_type=F32)
        o = o[:, :HEAD_DIM] / (o[:, HEAD_DIM:] + jnp.exp2(sink_ref_b[...] - m))
        for g in range(GROUP):
            o_ref[rows_of(n), g * HEAD_DIM:(g + 1) * HEAD_DIM] = o[g * WA_ROWS:(g + 1) * WA_ROWS].astype(BF16)

    scores(0, 0)

    def body(i, carry):
        for slot in range(2):
            n = 2 * i + slot
            scores(n + 1, 1 - slot)
            apply(n, slot)
        return carry

    lax.fori_loop(0, n_stages // 2 - 1, body, 0)
    scores(n_stages - 1, 1)
    apply(n_stages - 2, 0)
    apply(n_stages - 1, 1)


def _window_attn(proj, bias_lines, sink, batch, seq):
    assert (seq // WA_ROWS) % 2 == 0, "the stage loop is unrolled by two so score slots are static"
    q_col = (N_HEADS_A + 2 * N_KV_A) // GROUP
    k_col = N_HEADS_A + 2 * N_KV_A + N_HEADS_B
    v_col = k_col + N_KV_B
    grid_spec = pltpu.PrefetchScalarGridSpec(
        num_scalar_prefetch=1,
        grid=(batch, N_KV_B),
        in_specs=[
            pl.BlockSpec((seq, GROUP * HEAD_DIM), lambda b, k, s: (b, q_col + k)),
            pl.BlockSpec((seq, HEAD_DIM), lambda b, k, s: (b, k_col + k)),
            pl.BlockSpec((seq, HEAD_DIM), lambda b, k, s: (b, v_col + k)),
            pl.BlockSpec((GROUP, F32_SUBLANES, WA_BAND), lambda b, k, s: (k, 0, 0)),
        ],
        out_specs=pl.BlockSpec((seq, GROUP * HEAD_DIM), lambda b, k, s: (b, k)),
        scratch_shapes=[
            pltpu.VMEM((seq + 2 * WINDOW, HEAD_DIM), BF16),
            pltpu.VMEM((seq + 2 * WINDOW, 2 * HEAD_DIM), BF16),
            pltpu.VMEM((GROUP * WA_ROWS, WA_BAND), F32),
            pltpu.VMEM((GROUP * WA_ROWS, HEAD_DIM), F32),
            pltpu.VMEM((2, GROUP * WA_ROWS, WA_BAND), F32),
            pltpu.VMEM((2, GROUP * WA_ROWS, HEAD_DIM), F32),
        ],
    )
    return pl.pallas_call(
        _window_attn_kernel,
        grid_spec=grid_spec,
        out_shape=jax.ShapeDtypeStruct((batch * seq, N_HEADS_B * HEAD_DIM), BF16),
        compiler_params=_params("parallel", "parallel"),
        name="window_attn",
    )(sink, proj, proj, proj, bias_lines)


OUT_TM = 512


def _out_proj_kernel(oa_ref, ob_ref, w_ref, x_ref, g_ref, h_ref, m_ref):
    ka = oa_ref.shape[1]
    for r in range(OUT_TM // SUB_ROWS):
        rows = pl.ds(r * SUB_ROWS, SUB_ROWS)
        h = x_ref[rows, :]
        h = h + jnp.dot(oa_ref[rows, :], w_ref[:ka, :], preferred_element_type=F32)
        h = h + jnp.dot(ob_ref[rows, :], w_ref[ka:, :], preferred_element_type=F32)
        h_ref[rows, :] = h
        m_ref[rows, :] = _rms(h, g_ref[...]).astype(BF16)


def _out_proj(oa, ob, w, x2, g):
    m, d = x2.shape
    ka, kb = oa.shape[1], ob.shape[1]
    return pl.pallas_call(
        _out_proj_kernel,
        grid=(m // OUT_TM,),
        in_specs=[
            pl.BlockSpec((OUT_TM, ka), lambda i: (i, 0)),
            pl.BlockSpec((OUT_TM, kb), lambda i: (i, 0)),
            pl.BlockSpec((ka + kb, d), lambda i: (0, 0)),
            pl.BlockSpec((OUT_TM, d), lambda i: (i, 0)),
            pl.BlockSpec((1, d), lambda i: (0, 0)),
        ],
        out_specs=[pl.BlockSpec((OUT_TM, d), lambda i: (i, 0)), pl.BlockSpec((OUT_TM, d), lambda i: (i, 0))],
        out_shape=[jax.ShapeDtypeStruct((m, d), F32), jax.ShapeDtypeStruct((m, d), BF16)],
        compiler_params=_params("parallel"),
        name="out_proj",
    )(oa, ob, w, x2, g)


MLP_TM = 1024
MLP_TF = 1024


def _mlp_kernel(m_ref, wu_ref, wd_ref, d_ref):
    @pl.when(pl.program_id(1) == 0)
    def _():
        d_ref[...] = jnp.zeros(d_ref.shape, F32)

    a = jnp.dot(m_ref[...], wu_ref[...], preferred_element_type=F32)
    act = jnp.square(jnp.maximum(a, 0.0)).astype(BF16)
    d_ref[...] += jnp.dot(act, wd_ref[...], preferred_element_type=F32)


def _mlp(mn, wu, wd):
    m, d = mn.shape
    dff = wu.shape[1]
    return pl.pallas_call(
        _mlp_kernel,
        grid=(m // MLP_TM, dff // MLP_TF),
        in_specs=[
            pl.BlockSpec((MLP_TM, d), lambda i, f: (i, 0)),
            pl.BlockSpec((d, MLP_TF), lambda i, f: (0, f)),
            pl.BlockSpec((MLP_TF, d), lambda i, f: (f, 0)),
        ],
        out_specs=pl.BlockSpec((MLP_TM, d), lambda i, f: (i, 0)),
        out_shape=jax.ShapeDtypeStruct((m, d), F32),
        compiler_params=_params("parallel", "arbitrary"),
        name="mlp",
    )(mn, wu, wd)


GATE_TM = 512


def _gate_final_kernel(h_ref, d_ref, wg_ref, p_ref, wp_ref, gg_ref, gp_ref, gf_ref, o_ref):
    for r in range(GATE_TM // SUB_ROWS):
        rows = pl.ds(r * SUB_ROWS, SUB_ROWS)
        h = h_ref[rows, :] + d_ref[rows, :]
        gate = jax.nn.sigmoid(jnp.dot(_rms(h, gg_ref[...]).astype(BF16), wg_ref[...], preferred_element_type=F32))
        e = _rms(jnp.dot(p_ref[rows, :].astype(BF16), wp_ref[...], preferred_element_type=F32), gp_ref[...])
        o_ref[rows, :] = _rms(h + gate * e, gf_ref[...])


def _gate_final(h, delta, wg, p2, wp, gg, gp, gf):
    m, d = h.shape
    dp = p2.shape[1]
    tile = pl.BlockSpec((GATE_TM, d), lambda i: (i, 0))
    gain = pl.BlockSpec((1, d), lambda i: (0, 0))
    return pl.pallas_call(
        _gate_final_kernel,
        grid=(m // GATE_TM,),
        in_specs=[
            tile,
            tile,
            pl.BlockSpec((d, d), lambda i: (0, 0)),
            pl.BlockSpec((GATE_TM, dp), lambda i: (i, 0)),
            pl.BlockSpec((dp, d), lambda i: (0, 0)),
            gain, gain, gain,
        ],
        out_specs=tile,
        out_shape=jax.ShapeDtypeStruct((m, d), F32),
        compiler_params=_params("parallel"),
        name="gate_final",
    )(h, delta, wg, p2, wp, gg, gp, gf)


def _rope_tables(seq):
    rows = seq // GRID_W
    row = np.repeat(np.arange(rows, dtype=np.float32), GRID_W)
    col = np.tile(np.arange(GRID_W, dtype=np.float32), rows)
    half = HEAD_DIM // 2
    inv_freq = np.float32(ROPE_THETA) ** (-np.arange(0, half, 2, dtype=np.float32) / np.float32(half))
    ang_r = row[:, None] * inv_freq
    ang_c = col[:, None] * inv_freq
    cr, sr, cc, sc = np.cos(ang_r), np.sin(ang_r), np.cos(ang_c), np.sin(ang_c)
    cos = np.concatenate([cr, cr, cc, cc], axis=-1).astype(np.float32)
    sin_signed = np.concatenate([-sr, sr, -sc, sc], axis=-1).astype(np.float32)
    return jnp.asarray(cos), jnp.asarray(sin_signed)


def _global_logit_bound(q_gain, k_gain):
    bound = 1.01 * Q_SCALE * HEAD_DIM * jnp.max(jnp.abs(q_gain)) * jnp.max(jnp.abs(k_gain))
    return bound.reshape(1).astype(F32)


def _t5_bucket(rel):
    nb = N_BUCKETS // 2
    ret = jnp.where(rel > 0, nb, 0)
    n = jnp.abs(rel)
    max_exact = nb // 2
    nf = jnp.maximum(n, 1).astype(F32)
    large = max_exact + (jnp.log(nf / max_exact) / math.log(MAX_DISTANCE / max_exact)
                         * (nb - max_exact)).astype(jnp.int32)
    large = jnp.minimum(large, nb - 1)
    return ret + jnp.where(n < max_exact, n, large)


def _window_bias_lines(rel_bias_table):
    rel = jnp.arange(WA_BAND, dtype=jnp.int32) - WINDOW
    line = rel_bias_table[_t5_bucket(rel)].astype(F32).T * LOG2_E
    line = jnp.where((jnp.abs(rel) <= WINDOW)[None, :], line, NEG_INF)
    return jnp.broadcast_to(line[:, None, :], (line.shape[0], F32_SUBLANES, WA_BAND))


def kernel(x, p, attn_norm_g, w_in, q_norm_g, k_norm_g, sink_logits, w_out, mlp_norm_g, w_up, w_down, ple_w,
           ple_norm_g, gate_norm_g, w_gate, rel_bias_table, final_norm_g):
    batch, seq, d = x.shape
    assert w_in.shape[0] == 1, "gate_final fuses the final RMSNorm, which is only valid for a single layer"
    row = lambda v: v.reshape(1, -1).astype(F32)
    cos, sin_signed = _rope_tables(seq)
    h = x.reshape(batch * seq, d)
    proj = _in_proj(h, row(attn_norm_g), w_in, cos, sin_signed, row(q_norm_g), row(k_norm_g), seq)
    oa, (wu, wd, wo, wg) = _global_attn(proj, _global_logit_bound(q_norm_g, k_norm_g),
                                        (w_up, w_down, w_out, w_gate), batch, seq)
    ob = _window_attn(proj, _window_bias_lines(rel_bias_table), sink_logits.reshape(-1).astype(F32), batch, seq)
    h1, mn = _out_proj(oa, ob, wo, h, row(mlp_norm_g))
    delta = _mlp(mn, wu, wd)
    out = _gate_final(h1, delta, wg, p.reshape(batch * seq, -1), ple_w[0].astype(BF16),
                      row(gate_norm_g), row(ple_norm_g), row(final_norm_g))
    return out.reshape(batch, seq, d)
```

```python
import functools
import math

import jax
import jax.numpy as jnp
import numpy as np
from jax import lax
from jax.experimental import pallas as pl
from jax.experimental.pallas import tpu as pltpu

HEAD_DIM = 128
N_HEADS_A = 8
N_KV_A = 2
N_HEADS_B = 8
N_KV_B = 2
GROUP = 4
GRID_W = 64
WINDOW = 128
N_BUCKETS = 32
MAX_DISTANCE = 128
ROPE_THETA = 10000.0
EPS = 1e-6
NEG_INF = -1e30
LOG2_E = math.log2(math.e)
Q_SCALE = HEAD_DIM ** -0.5 * LOG2_E

V7X_VMEM_BYTES = 64 * 1024 * 1024
VMEM_RESERVE_BYTES = 8 * 1024 * 1024
VMEM_LIMIT_BYTES = V7X_VMEM_BYTES - VMEM_RESERVE_BYTES
F32_SUBLANES = 8
BF16_SUBLANES = 16

SUB_ROWS = 256

BF16 = jnp.bfloat16
F32 = jnp.float32


def _params(*semantics):
    return pltpu.CompilerParams(dimension_semantics=semantics, vmem_limit_bytes=VMEM_LIMIT_BYTES)


def _rms(x, g):
    return x * lax.rsqrt(jnp.mean(x * x, axis=-1, keepdims=True) + EPS) * g


IN_TM = 512
IN_TN = 512
IN_W_ROWS = 128
IN_W_SLOTS = 4

_PLAIN, _Q_A, _K_A, _Q_B = range(4)
_HEAD_KINDS = ([_Q_A] * N_HEADS_A + [_K_A] * N_KV_A + [_PLAIN] * N_KV_A
               + [_Q_B] * N_HEADS_B + [_PLAIN] * (2 * N_KV_B))


def _rope(y, cos, sin_signed):
    lane = lax.broadcasted_iota(jnp.int32, y.shape, 1)
    partner = jnp.where((lane % 64) < 32, pltpu.roll(y, 96, 1), pltpu.roll(y, 32, 1))
    return y * cos + partner * sin_signed


def _in_proj_kernel(x_ref, g_ref, w_hbm, cos_ref, sin_ref, gq_ref, gk_ref, o_ref, w_ref, wbuf_ref, wsem):
    @pl.when(pl.program_id(0) == 0)
    def _():
        def chunk_copy(c):
            src = w_hbm.at[0, pl.ds(c * IN_W_ROWS, IN_W_ROWS)]
            return pltpu.make_async_copy(src, wbuf_ref.at[c % IN_W_SLOTS], wsem.at[c % IN_W_SLOTS])

        n_chunks = w_ref.shape[0] // IN_W_ROWS
        for c in range(IN_W_SLOTS - 1):
            chunk_copy(c).start()
        for c in range(n_chunks):
            chunk_copy(c).wait()
            if c + IN_W_SLOTS - 1 < n_chunks:
                chunk_copy(c + IN_W_SLOTS - 1).start()
            w_ref[c * IN_W_ROWS:(c + 1) * IN_W_ROWS, :] = wbuf_ref[c % IN_W_SLOTS].astype(BF16)

    heads_per_dot = IN_TN // HEAD_DIM
    for r in range(IN_TM // SUB_ROWS):
        rows = pl.ds(r * SUB_ROWS, SUB_ROWS)
        u = _rms(x_ref[rows, :], g_ref[...]).astype(BF16)
        cos, sin_signed = cos_ref[rows, :], sin_ref[rows, :]
        for c in range(w_ref.shape[1] // IN_TN):
            acc = jnp.dot(u, w_ref[:, c * IN_TN:(c + 1) * IN_TN], preferred_element_type=F32)
            for hh in range(heads_per_dot):
                head = c * heads_per_dot + hh
                a = acc[:, hh * HEAD_DIM:(hh + 1) * HEAD_DIM]
                kind = _HEAD_KINDS[head]
                if kind == _Q_A:
                    a = _rope(_rms(a, gq_ref[...]), cos, sin_signed) * Q_SCALE
                elif kind == _K_A:
                    a = _rope(_rms(a, gk_ref[...]), cos, sin_signed)
                elif kind == _Q_B:
                    a = a * Q_SCALE
                o_ref[rows, head * HEAD_DIM:(head + 1) * HEAD_DIM] = a.astype(BF16)


def _in_proj(x2, g, w, cos, sin_signed, gq, gk, seq):
    m, d = x2.shape
    n = w.shape[2]
    assert w.shape[0] == 1 and n == len(_HEAD_KINDS) * HEAD_DIM and d % IN_W_ROWS == 0
    pos_tiles = seq // IN_TM
    return pl.pallas_call(
        _in_proj_kernel,
        grid=(m // IN_TM,),
        in_specs=[
            pl.BlockSpec((IN_TM, d), lambda i: (i, 0)),
            pl.BlockSpec((1, d), lambda i: (0, 0)),
            pl.BlockSpec(memory_space=pl.ANY),
            pl.BlockSpec((IN_TM, HEAD_DIM), lambda i: (i % pos_tiles, 0)),
            pl.BlockSpec((IN_TM, HEAD_DIM), lambda i: (i % pos_tiles, 0)),
            pl.BlockSpec((1, HEAD_DIM), lambda i: (0, 0)),
            pl.BlockSpec((1, HEAD_DIM), lambda i: (0, 0)),
        ],
        out_specs=pl.BlockSpec((IN_TM, n), lambda i: (i, 0)),
        out_shape=jax.ShapeDtypeStruct((m, n), BF16),
        scratch_shapes=[
            pltpu.VMEM((d, n), BF16),
            pltpu.VMEM((IN_W_SLOTS, IN_W_ROWS, n), F32),
            pltpu.SemaphoreType.DMA((IN_W_SLOTS,)),
        ],
        compiler_params=_params("arbitrary"),
        name="in_proj",
    )(x2, g, w, cos, sin_signed, gq, gk)


GA_ROWS = 128
GA_KEYS = 512


def _global_attn_kernel(*refs, n_cast):
    q_ref, k_ref, v_ref = refs[:3]
    w_f32 = refs[3:3 + n_cast]
    o_ref = refs[3 + n_cast]
    w_bf16 = refs[4 + n_cast:4 + 2 * n_cast]
    v1_ref, kt_ref, s_ref, m_ref = refs[4 + 2 * n_cast:8 + 2 * n_cast]
    in_bufs = refs[8 + 2 * n_cast:8 + 3 * n_cast]
    out_bufs = refs[8 + 3 * n_cast:8 + 4 * n_cast]
    in_sem, out_sem = refs[8 + 4 * n_cast:]

    seq = q_ref.shape[0]
    n_blocks = seq // GA_ROWS
    first_chunk = (pl.program_id(0) * pl.num_programs(1) + pl.program_id(1)) * n_blocks

    def in_copy(w, chunk, slot):
        rows = in_bufs[w].shape[1]
        src = w_f32[w].at[0, pl.ds(pl.multiple_of(chunk * rows, rows), rows)]
        return pltpu.make_async_copy(src, in_bufs[w].at[slot], in_sem.at[w, slot])

    def out_copy(w, chunk):
        rows = out_bufs[w].shape[0]
        dst = w_bf16[w].at[pl.ds(pl.multiple_of(chunk * rows, rows), rows)]
        return pltpu.make_async_copy(out_bufs[w], dst, out_sem.at[w])

    def cast_chunk(r, first, last):
        chunk = first_chunk + r
        slot = r % 2
        for w in range(n_cast):
            in_copy(w, chunk, slot).wait()
            if not last:
                in_copy(w, chunk + 1, 1 - slot).start()
            if not first:
                out_copy(w, chunk - 1).wait()
            out_bufs[w][...] = in_bufs[w][slot].astype(BF16)
            out_copy(w, chunk).start()

    for w in range(n_cast):
        in_copy(w, first_chunk, 0).start()

    v1_ref[:, :HEAD_DIM] = v_ref[...]
    v1_ref[:, HEAD_DIM:] = jnp.ones(v_ref.shape, BF16)
    kt_ref[...] = k_ref[...].T

    def rows_of(r):
        return pl.ds(pl.multiple_of(r * GA_ROWS, GA_ROWS), GA_ROWS)

    def scores(r, g, slot):
        q = q_ref[rows_of(r), g * HEAD_DIM:(g + 1) * HEAD_DIM]
        part = None
        for c in range(seq // GA_KEYS):
            keys = slice(c * GA_KEYS, (c + 1) * GA_KEYS)
            s = jnp.dot(q, kt_ref[:, keys], preferred_element_type=F32)
            s_ref[slot, :, keys] = s
            for l in range(GA_KEYS // HEAD_DIM):
                piece = s[:, l * HEAD_DIM:(l + 1) * HEAD_DIM]
                part = piece if part is None else jnp.maximum(part, piece)
        m_ref[slot] = jnp.broadcast_to(jnp.max(part, axis=-1, keepdims=True), (GA_ROWS, HEAD_DIM))

    def apply(r, g, slot):
        m = m_ref[slot]
        o = jnp.zeros((GA_ROWS, 2 * HEAD_DIM), F32)
        for c in range(seq // GA_KEYS):
            keys = slice(c * GA_KEYS, (c + 1) * GA_KEYS)
            p = jnp.concatenate([jnp.exp2(s_ref[slot, :, c * GA_KEYS + l * HEAD_DIM:c * GA_KEYS + (l + 1) * HEAD_DIM] - m)
                                 for l in range(GA_KEYS // HEAD_DIM)], axis=1).astype(BF16)
            o = o + jnp.dot(p, v1_ref[keys, :], preferred_element_type=F32)
        o_ref[rows_of(r), g * HEAD_DIM:(g + 1) * HEAD_DIM] = (o[:, :HEAD_DIM] / o[:, HEAD_DIM:]).astype(BF16)

    def row_block(r, first, last):
        for g in range(GROUP):
            slot = g % 2
            if g + 1 < GROUP:
                scores(r, g + 1, 1 - slot)
            elif not last:
                scores(r + 1, 0, 1 - slot)
            apply(r, g, slot)
        cast_chunk(r, first, last)

    scores(0, 0, 0)
    row_block(0, first=True, last=False)

    def body(r, carry):
        row_block(r, first=False, last=False)
        return carry

    lax.fori_loop(1, n_blocks - 1, body, 0)
    row_block(n_blocks - 1, first=False, last=True)
    for w in range(n_cast):
        out_copy(w, first_chunk + n_blocks - 1).wait()


def _global_attn(proj, weights, batch, seq):
    assert GROUP % 2 == 0, "score slots alternate per head and must line up across row blocks"
    n_blocks = seq // GA_ROWS
    assert n_blocks >= 3 and n_blocks % 2 == 0
    n_chunks = batch * N_KV_A * n_blocks
    chunk_rows = []
    for w in weights:
        assert w.shape[0] == 1 and w.shape[1] % (BF16_SUBLANES * n_chunks) == 0, w.shape
        chunk_rows.append(w.shape[1] // n_chunks)
    n_cast = len(weights)
    k_col = N_HEADS_A
    v_col = N_HEADS_A + N_KV_A
    any_spec = pl.BlockSpec(memory_space=pl.ANY)
    attn_spec = pl.BlockSpec((seq, GROUP * HEAD_DIM), lambda b, k: (b, k))
    outs = pl.pallas_call(
        functools.partial(_global_attn_kernel, n_cast=n_cast),
        grid=(batch, N_KV_A),
        in_specs=[
            attn_spec,
            pl.BlockSpec((seq, HEAD_DIM), lambda b, k: (b, k_col + k)),
            pl.BlockSpec((seq, HEAD_DIM), lambda b, k: (b, v_col + k)),
        ] + [any_spec] * n_cast,
        out_specs=[attn_spec] + [any_spec] * n_cast,
        out_shape=[jax.ShapeDtypeStruct((batch * seq, N_HEADS_A * HEAD_DIM), BF16)]
        + [jax.ShapeDtypeStruct(w.shape[1:], BF16) for w in weights],
        scratch_shapes=[
            pltpu.VMEM((seq, 2 * HEAD_DIM), BF16),
            pltpu.VMEM((HEAD_DIM, seq), BF16),
            pltpu.VMEM((2, GA_ROWS, seq), F32),
            pltpu.VMEM((2, GA_ROWS, HEAD_DIM), F32),
        ] + [pltpu.VMEM((2, rows, w.shape[2]), F32) for rows, w in zip(chunk_rows, weights)]
        + [pltpu.VMEM((rows, w.shape[2]), BF16) for rows, w in zip(chunk_rows, weights)]
        + [pltpu.SemaphoreType.DMA((n_cast, 2)), pltpu.SemaphoreType.DMA((n_cast,))],
        compiler_params=_params("arbitrary", "arbitrary"),
        name="global_attn",
    )(proj, proj, proj, *weights)
    return outs[0], outs[1:]


WA_ROWS = 256
WA_BAND = WA_ROWS + 2 * WINDOW


def _window_attn_kernel(sink_ref, q_ref, k_ref, v_ref, line_ref, o_ref,
                        kpad_ref, v1pad_ref, bias_ref, sink_ref_b, s_ref, m_ref):
    kvh = pl.program_id(1)
    seq = q_ref.shape[0]
    n_stages = seq // WA_ROWS

    zeros = jnp.zeros((WINDOW, 2 * HEAD_DIM), BF16)
    kpad_ref[:WINDOW, :] = zeros[:, :HEAD_DIM]
    kpad_ref[WINDOW + seq:, :] = zeros[:, :HEAD_DIM]
    kpad_ref[WINDOW:WINDOW + seq, :] = k_ref[...]
    v1pad_ref[:WINDOW, :] = zeros
    v1pad_ref[WINDOW + seq:, :] = zeros
    v1pad_ref[WINDOW:WINDOW + seq, :HEAD_DIM] = v_ref[...]
    v1pad_ref[WINDOW:WINDOW + seq, HEAD_DIM:] = jnp.ones(v_ref.shape, BF16)

    for g in range(GROUP):
        head = kvh * GROUP + g
        tile = jnp.broadcast_to(line_ref[g, :1, :], (WA_ROWS, WA_BAND))
        bias_ref[g * WA_ROWS:(g + 1) * WA_ROWS, :] = pltpu.roll(tile, 0, 1, stride=1, stride_axis=0)
        sink_ref_b[g * WA_ROWS:(g + 1) * WA_ROWS, :] = jnp.full((WA_ROWS, HEAD_DIM), sink_ref[head] * LOG2_E, F32)

    col = lax.broadcasted_iota(jnp.int32, (GROUP * WA_ROWS, WA_BAND), 1)

    def rows_of(n):
        return pl.ds(pl.multiple_of(n * WA_ROWS, WA_ROWS), WA_ROWS)

    def band_of(n):
        return pl.ds(pl.multiple_of(n * WA_ROWS, WA_ROWS), WA_BAND)

    def scores(n, slot):
        q = jnp.concatenate([q_ref[rows_of(n), g * HEAD_DIM:(g + 1) * HEAD_DIM] for g in range(GROUP)], axis=0)
        s = lax.dot_general(q, kpad_ref[band_of(n), :], (((1,), (1,)), ((), ())), preferred_element_type=F32)
        in_range = ((col >= WINDOW) | (n > 0)) & ((col < WA_BAND - WINDOW) | (n < n_stages - 1))
        s = jnp.where(in_range, s + bias_ref[...], NEG_INF)
        s_ref[slot] = s
        row_max = jnp.broadcast_to(jnp.max(s, axis=-1, keepdims=True), sink_ref_b.shape)
        m_ref[slot] = jnp.maximum(row_max, sink_ref_b[...])

    def apply(n, slot):
        m = m_ref[slot]
        p = jnp.concatenate([jnp.exp2(s_ref[slot, :, c * HEAD_DIM:(c + 1) * HEAD_DIM] - m)
                             for c in range(WA_BAND // HEAD_DIM)], axis=1).astype(BF16)
        o = jnp.dot(p, v1pad_ref[band_of(n), :], preferred_element_type=F32)
        o = o[:, :HEAD_DIM] / (o[:, HEAD_DIM:] + jnp.exp2(sink_ref_b[...] - m))
        for g in range(GROUP):
            o_ref[rows_of(n), g * HEAD_DIM:(g + 1) * HEAD_DIM] = o[g * WA_ROWS:(g + 1) * WA_ROWS].astype(BF16)

    scores(0, 0)

    def body(i, carry):
        for slot in range(2):
            n = 2 * i + slot
            scores(n + 1, 1 - slot)
            apply(n, slot)
        return carry

    lax.fori_loop(0, n_stages // 2 - 1, body, 0)
    scores(n_stages - 1, 1)
    apply(n_stages - 2, 0)
    apply(n_stages - 1, 1)


def _window_attn(proj, bias_lines, sink, batch, seq):
    assert (seq // WA_ROWS) % 2 == 0, "the stage loop is unrolled by two so score slots are static"
    q_col = (N_HEADS_A + 2 * N_KV_A) // GROUP
    k_col = N_HEADS_A + 2 * N_KV_A + N_HEADS_B
    v_col = k_col + N_KV_B
    grid_spec = pltpu.PrefetchScalarGridSpec(
        num_scalar_prefetch=1,
        grid=(batch, N_KV_B),
        in_specs=[
            pl.BlockSpec((seq, GROUP * HEAD_DIM), lambda b, k, s: (b, q_col + k)),
            pl.BlockSpec((seq, HEAD_DIM), lambda b, k, s: (b, k_col + k)),
            pl.BlockSpec((seq, HEAD_DIM), lambda b, k, s: (b, v_col + k)),
            pl.BlockSpec((GROUP, F32_SUBLANES, WA_BAND), lambda b, k, s: (k, 0, 0)),
        ],
        out_specs=pl.BlockSpec((seq, GROUP * HEAD_DIM), lambda b, k, s: (b, k)),
        scratch_shapes=[
            pltpu.VMEM((seq + 2 * WINDOW, HEAD_DIM), BF16),
            pltpu.VMEM((seq + 2 * WINDOW, 2 * HEAD_DIM), BF16),
            pltpu.VMEM((GROUP * WA_ROWS, WA_BAND), F32),
            pltpu.VMEM((GROUP * WA_ROWS, HEAD_DIM), F32),
            pltpu.VMEM((2, GROUP * WA_ROWS, WA_BAND), F32),
            pltpu.VMEM((2, GROUP * WA_ROWS, HEAD_DIM), F32),
        ],
    )
    return pl.pallas_call(
        _window_attn_kernel,
        grid_spec=grid_spec,
        out_shape=jax.ShapeDtypeStruct((batch * seq, N_HEADS_B * HEAD_DIM), BF16),
        compiler_params=_params("parallel", "parallel"),
        name="window_attn",
    )(sink, proj, proj, proj, bias_lines)


OUT_TM = 512


def _out_proj_kernel(oa_ref, ob_ref, w_ref, x_ref, g_ref, h_ref, m_ref):
    ka = oa_ref.shape[1]
    for r in range(OUT_TM // SUB_ROWS):
        rows = pl.ds(r * SUB_ROWS, SUB_ROWS)
        h = x_ref[rows, :]
        h = h + jnp.dot(oa_ref[rows, :], w_ref[:ka, :], preferred_element_type=F32)
        h = h + jnp.dot(ob_ref[rows, :], w_ref[ka:, :], preferred_element_type=F32)
        h_ref[rows, :] = h
        m_ref[rows, :] = _rms(h, g_ref[...]).astype(BF16)


def _out_proj(oa, ob, w, x2, g):
    m, d = x2.shape
    ka, kb = oa.shape[1], ob.shape[1]
    return pl.pallas_call(
        _out_proj_kernel,
        grid=(m // OUT_TM,),
        in_specs=[
            pl.BlockSpec((OUT_TM, ka), lambda i: (i, 0)),
            pl.BlockSpec((OUT_TM, kb), lambda i: (i, 0)),
            pl.BlockSpec((ka + kb, d), lambda i: (0, 0)),
            pl.BlockSpec((OUT_TM, d), lambda i: (i, 0)),
            pl.BlockSpec((1, d), lambda i: (0, 0)),
        ],
        out_specs=[pl.BlockSpec((OUT_TM, d), lambda i: (i, 0)), pl.BlockSpec((OUT_TM, d), lambda i: (i, 0))],
        out_shape=[jax.ShapeDtypeStruct((m, d), F32), jax.ShapeDtypeStruct((m, d), BF16)],
        compiler_params=_params("parallel"),
        name="out_proj",
    )(oa, ob, w, x2, g)


MLP_TM = 1024
MLP_TF = 1024


def _mlp_kernel(m_ref, wu_ref, wd_ref, d_ref):
    @pl.when(pl.program_id(1) == 0)
    def _():
        d_ref[...] = jnp.zeros(d_ref.shape, F32)

    a = jnp.dot(m_ref[...], wu_ref[...], preferred_element_type=F32)
    act = jnp.square(jnp.maximum(a, 0.0)).astype(BF16)
    d_ref[...] += jnp.dot(act, wd_ref[...], preferred_element_type=F32)


def _mlp(mn, wu, wd):
    m, d = mn.shape
    dff = wu.shape[1]
    return pl.pallas_call(
        _mlp_kernel,
        grid=(m // MLP_TM, dff // MLP_TF),
        in_specs=[
            pl.BlockSpec((MLP_TM, d), lambda i, f: (i, 0)),
            pl.BlockSpec((d, MLP_TF), lambda i, f: (0, f)),
            pl.BlockSpec((MLP_TF, d), lambda i, f: (f, 0)),
        ],
        out_specs=pl.BlockSpec((MLP_TM, d), lambda i, f: (i, 0)),
        out_shape=jax.ShapeDtypeStruct((m, d), F32),
        compiler_params=_params("parallel", "arbitrary"),
        name="mlp",
    )(mn, wu, wd)


GATE_TM = 512


def _gate_final_kernel(h_ref, d_ref, wg_ref, p_ref, wp_ref, gg_ref, gp_ref, gf_ref, o_ref):
    for r in range(GATE_TM // SUB_ROWS):
        rows = pl.ds(r * SUB_ROWS, SUB_ROWS)
        h = h_ref[rows, :] + d_ref[rows, :]
        gate = jax.nn.sigmoid(jnp.dot(_rms(h, gg_ref[...]).astype(BF16), wg_ref[...], preferred_element_type=F32))
        e = _rms(jnp.dot(p_ref[rows, :].astype(BF16), wp_ref[...], preferred_element_type=F32), gp_ref[...])
        o_ref[rows, :] = _rms(h + gate * e, gf_ref[...])


def _gate_final(h, delta, wg, p2, wp, gg, gp, gf):
    m, d = h.shape
    dp = p2.shape[1]
    tile = pl.BlockSpec((GATE_TM, d), lambda i: (i, 0))
    gain = pl.BlockSpec((1, d), lambda i: (0, 0))
    return pl.pallas_call(
        _gate_final_kernel,
        grid=(m // GATE_TM,),
        in_specs=[
            tile,
            tile,
            pl.BlockSpec((d, d), lambda i: (0, 0)),
            pl.BlockSpec((GATE_TM, dp), lambda i: (i, 0)),
            pl.BlockSpec((dp, d), lambda i: (0, 0)),
            gain, gain, gain,
        ],
        out_specs=tile,
        out_shape=jax.ShapeDtypeStruct((m, d), F32),
        compiler_params=_params("parallel"),
        name="gate_final",
    )(h, delta, wg, p2, wp, gg, gp, gf)


def _rope_tables(seq):
    rows = seq // GRID_W
    row = np.repeat(np.arange(rows, dtype=np.float32), GRID_W)
    col = np.tile(np.arange(GRID_W, dtype=np.float32), rows)
    half = HEAD_DIM // 2
    inv_freq = np.float32(ROPE_THETA) ** (-np.arange(0, half, 2, dtype=np.float32) / np.float32(half))
    ang_r = row[:, None] * inv_freq
    ang_c = col[:, None] * inv_freq
    cr, sr, cc, sc = np.cos(ang_r), np.sin(ang_r), np.cos(ang_c), np.sin(ang_c)
    cos = np.concatenate([cr, cr, cc, cc], axis=-1).astype(np.float32)
    sin_signed = np.concatenate([-sr, sr, -sc, sc], axis=-1).astype(np.float32)
    return jnp.asarray(cos), jnp.asarray(sin_signed)


def _t5_bucket(rel):
    nb = N_BUCKETS // 2
    ret = jnp.where(rel > 0, nb, 0)
    n = jnp.abs(rel)
    max_exact = nb // 2
    nf = jnp.maximum(n, 1).astype(F32)
    large = max_exact + (jnp.log(nf / max_exact) / math.log(MAX_DISTANCE / max_exact)
                         * (nb - max_exact)).astype(jnp.int32)
    large = jnp.minimum(large, nb - 1)
    return ret + jnp.where(n < max_exact, n, large)


def _window_bias_lines(rel_bias_table):
    rel = jnp.arange(WA_BAND, dtype=jnp.int32) - WINDOW
    line = rel_bias_table[_t5_bucket(rel)].astype(F32).T * LOG2_E
    line = jnp.where((jnp.abs(rel) <= WINDOW)[None, :], line, NEG_INF)
    return jnp.broadcast_to(line[:, None, :], (line.shape[0], F32_SUBLANES, WA_BAND))


def kernel(x, p, attn_norm_g, w_in, q_norm_g, k_norm_g, sink_logits, w_out, mlp_norm_g, w_up, w_down, ple_w,
           ple_norm_g, gate_norm_g, w_gate, rel_bias_table, final_norm_g):
    batch, seq, d = x.shape
    assert w_in.shape[0] == 1, "gate_final fuses the final RMSNorm, which is only valid for a single layer"
    row = lambda v: v.reshape(1, -1).astype(F32)
    cos, sin_signed = _rope_tables(seq)
    h = x.reshape(batch * seq, d)
    proj = _in_proj(h, row(attn_norm_g), w_in, cos, sin_signed, row(q_norm_g), row(k_norm_g), seq)
    oa, (wu, wd, wo, wg) = _global_attn(proj, (w_up, w_down, w_out, w_gate), batch, seq)
    ob = _window_attn(proj, _window_bias_lines(rel_bias_table), sink_logits.reshape(-1).astype(F32), batch, seq)
    h1, mn = _out_proj(oa, ob, wo, h, row(mlp_norm_g))
    delta = _mlp(mn, wu, wd)
    out = _gate_final(h1, delta, wg, p.reshape(batch * seq, -1), ple_w[0].astype(BF16),
                      row(gate_norm_g), row(ple_norm_g), row(final_norm_g))
    return out.reshape(batch, seq, d)
```

```python
import functools
import math

import jax
import jax.numpy as jnp
import numpy as np
from jax import lax
from jax.experimental import pallas as pl
from jax.experimental.pallas import tpu as pltpu

HEAD_DIM = 128
N_HEADS_A = 8
N_KV_A = 2
N_HEADS_B = 8
N_KV_B = 2
GROUP = 4
GRID_W = 64
WINDOW = 128
N_BUCKETS = 32
MAX_DISTANCE = 128
ROPE_THETA = 10000.0
EPS = 1e-6
NEG_INF = -1e30
LOG2_E = math.log2(math.e)
Q_SCALE = HEAD_DIM ** -0.5 * LOG2_E

V7X_VMEM_BYTES = 64 * 1024 * 1024
VMEM_RESERVE_BYTES = 8 * 1024 * 1024
VMEM_LIMIT_BYTES = V7X_VMEM_BYTES - VMEM_RESERVE_BYTES
F32_SUBLANES = 8
BF16_SUBLANES = 16

SUB_ROWS = 256

BF16 = jnp.bfloat16
F32 = jnp.float32


def _params(*semantics):
    return pltpu.CompilerParams(dimension_semantics=semantics, vmem_limit_bytes=VMEM_LIMIT_BYTES)


def _rms(x, g):
    return x * lax.rsqrt(jnp.mean(x * x, axis=-1, keepdims=True) + EPS) * g


IN_TM = 1024
IN_TN = 512
IN_W_ROWS = 128
IN_W_SLOTS = 4

_PLAIN, _Q_A, _K_A, _Q_B = range(4)
_HEAD_KINDS = ([_Q_A] * N_HEADS_A + [_K_A] * N_KV_A + [_PLAIN] * N_KV_A
               + [_Q_B] * N_HEADS_B + [_PLAIN] * (2 * N_KV_B))


def _rope(y, cos, sin_signed):
    lane = lax.broadcasted_iota(jnp.int32, y.shape, 1)
    partner = jnp.where((lane % 64) < 32, pltpu.roll(y, 96, 1), pltpu.roll(y, 32, 1))
    return y * cos + partner * sin_signed


def _in_proj_kernel(x_ref, g_ref, w_hbm, cos_ref, sin_ref, gq_ref, gk_ref, o_ref, w_ref, wbuf_ref, wsem):
    @pl.when(pl.program_id(0) == 0)
    def _():
        def chunk_copy(c):
            src = w_hbm.at[0, pl.ds(c * IN_W_ROWS, IN_W_ROWS)]
            return pltpu.make_async_copy(src, wbuf_ref.at[c % IN_W_SLOTS], wsem.at[c % IN_W_SLOTS])

        n_chunks = w_ref.shape[0] // IN_W_ROWS
        for c in range(IN_W_SLOTS - 1):
            chunk_copy(c).start()
        for c in range(n_chunks):
            chunk_copy(c).wait()
            if c + IN_W_SLOTS - 1 < n_chunks:
                chunk_copy(c + IN_W_SLOTS - 1).start()
            w_ref[c * IN_W_ROWS:(c + 1) * IN_W_ROWS, :] = wbuf_ref[c % IN_W_SLOTS].astype(BF16)

    heads_per_dot = IN_TN // HEAD_DIM
    for r in range(IN_TM // SUB_ROWS):
        rows = pl.ds(r * SUB_ROWS, SUB_ROWS)
        u = _rms(x_ref[rows, :], g_ref[...]).astype(BF16)
        cos, sin_signed = cos_ref[rows, :], sin_ref[rows, :]
        for c in range(w_ref.shape[1] // IN_TN):
            acc = jnp.dot(u, w_ref[:, c * IN_TN:(c + 1) * IN_TN], preferred_element_type=F32)
            for hh in range(heads_per_dot):
                head = c * heads_per_dot + hh
                a = acc[:, hh * HEAD_DIM:(hh + 1) * HEAD_DIM]
                kind = _HEAD_KINDS[head]
                if kind == _Q_A:
                    a = _rope(_rms(a, gq_ref[...]), cos, sin_signed) * Q_SCALE
                elif kind == _K_A:
                    a = _rope(_rms(a, gk_ref[...]), cos, sin_signed)
                elif kind == _Q_B:
                    a = a * Q_SCALE
                o_ref[rows, head * HEAD_DIM:(head + 1) * HEAD_DIM] = a.astype(BF16)


def _in_proj(x2, g, w, cos, sin_signed, gq, gk, seq):
    m, d = x2.shape
    n = w.shape[2]
    assert w.shape[0] == 1 and n == len(_HEAD_KINDS) * HEAD_DIM and d % IN_W_ROWS == 0
    pos_tiles = seq // IN_TM
    return pl.pallas_call(
        _in_proj_kernel,
        grid=(m // IN_TM,),
        in_specs=[
            pl.BlockSpec((IN_TM, d), lambda i: (i, 0)),
            pl.BlockSpec((1, d), lambda i: (0, 0)),
            pl.BlockSpec(memory_space=pl.ANY),
            pl.BlockSpec((IN_TM, HEAD_DIM), lambda i: (i % pos_tiles, 0)),
            pl.BlockSpec((IN_TM, HEAD_DIM), lambda i: (i % pos_tiles, 0)),
            pl.BlockSpec((1, HEAD_DIM), lambda i: (0, 0)),
            pl.BlockSpec((1, HEAD_DIM), lambda i: (0, 0)),
        ],
        out_specs=pl.BlockSpec((IN_TM, n), lambda i: (i, 0)),
        out_shape=jax.ShapeDtypeStruct((m, n), BF16),
        scratch_shapes=[
            pltpu.VMEM((d, n), BF16),
            pltpu.VMEM((IN_W_SLOTS, IN_W_ROWS, n), F32),
            pltpu.SemaphoreType.DMA((IN_W_SLOTS,)),
        ],
        compiler_params=_params("arbitrary"),
        name="in_proj",
    )(x2, g, w, cos, sin_signed, gq, gk)


GA_ROWS = 256
GA_KEYS = 512


def _global_attn_kernel(*refs, n_cast):
    q_ref, k_ref, v_ref = refs[:3]
    w_f32 = refs[3:3 + n_cast]
    o_ref = refs[3 + n_cast]
    w_bf16 = refs[4 + n_cast:4 + 2 * n_cast]
    v1_ref, kt_ref, s_ref, m_ref = refs[4 + 2 * n_cast:8 + 2 * n_cast]
    in_bufs = refs[8 + 2 * n_cast:8 + 3 * n_cast]
    out_bufs = refs[8 + 3 * n_cast:8 + 4 * n_cast]
    in_sem, out_sem = refs[8 + 4 * n_cast:]

    seq = q_ref.shape[0]
    n_blocks = seq // GA_ROWS
    first_chunk = (pl.program_id(0) * pl.num_programs(1) + pl.program_id(1)) * n_blocks

    def in_copy(w, chunk, slot):
        rows = in_bufs[w].shape[1]
        src = w_f32[w].at[0, pl.ds(pl.multiple_of(chunk * rows, rows), rows)]
        return pltpu.make_async_copy(src, in_bufs[w].at[slot], in_sem.at[w, slot])

    def out_copy(w, chunk):
        rows = out_bufs[w].shape[0]
        dst = w_bf16[w].at[pl.ds(pl.multiple_of(chunk * rows, rows), rows)]
        return pltpu.make_async_copy(out_bufs[w], dst, out_sem.at[w])

    def cast_chunk(r, first, last):
        chunk = first_chunk + r
        slot = r % 2
        for w in range(n_cast):
            in_copy(w, chunk, slot).wait()
            if not last:
                in_copy(w, chunk + 1, 1 - slot).start()
            if not first:
                out_copy(w, chunk - 1).wait()
            out_bufs[w][...] = in_bufs[w][slot].astype(BF16)
            out_copy(w, chunk).start()

    for w in range(n_cast):
        in_copy(w, first_chunk, 0).start()

    v1_ref[:, :HEAD_DIM] = v_ref[...]
    v1_ref[:, HEAD_DIM:] = jnp.ones(v_ref.shape, BF16)
    kt_ref[...] = k_ref[...].T

    def rows_of(r):
        return pl.ds(pl.multiple_of(r * GA_ROWS, GA_ROWS), GA_ROWS)

    def scores(r, g, slot):
        q = q_ref[rows_of(r), g * HEAD_DIM:(g + 1) * HEAD_DIM]
        part = None
        for c in range(seq // GA_KEYS):
            keys = slice(c * GA_KEYS, (c + 1) * GA_KEYS)
            s = jnp.dot(q, kt_ref[:, keys], preferred_element_type=F32)
            s_ref[slot, :, keys] = s
            for l in range(GA_KEYS // HEAD_DIM):
                piece = s[:, l * HEAD_DIM:(l + 1) * HEAD_DIM]
                part = piece if part is None else jnp.maximum(part, piece)
        m_ref[slot] = jnp.broadcast_to(jnp.max(part, axis=-1, keepdims=True), (GA_ROWS, HEAD_DIM))

    def apply(r, g, slot):
        m = m_ref[slot]
        o = jnp.zeros((GA_ROWS, 2 * HEAD_DIM), F32)
        for c in range(seq // GA_KEYS):
            keys = slice(c * GA_KEYS, (c + 1) * GA_KEYS)
            p = jnp.concatenate([jnp.exp2(s_ref[slot, :, c * GA_KEYS + l * HEAD_DIM:c * GA_KEYS + (l + 1) * HEAD_DIM] - m)
                                 for l in range(GA_KEYS // HEAD_DIM)], axis=1).astype(BF16)
            o = o + jnp.dot(p, v1_ref[keys, :], preferred_element_type=F32)
        o_ref[rows_of(r), g * HEAD_DIM:(g + 1) * HEAD_DIM] = (o[:, :HEAD_DIM] / o[:, HEAD_DIM:]).astype(BF16)

    def row_block(r, first, last):
        for g in range(GROUP):
            slot = g % 2
            if g + 1 < GROUP:
                scores(r, g + 1, 1 - slot)
            elif not last:
                scores(r + 1, 0, 1 - slot)
            apply(r, g, slot)
        cast_chunk(r, first, last)

    scores(0, 0, 0)
    row_block(0, first=True, last=False)

    def body(r, carry):
        row_block(r, first=False, last=False)
        return carry

    lax.fori_loop(1, n_blocks - 1, body, 0)
    row_block(n_blocks - 1, first=False, last=True)
    for w in range(n_cast):
        out_copy(w, first_chunk + n_blocks - 1).wait()


def _global_attn(proj, weights, batch, seq):
    assert GROUP % 2 == 0, "score slots alternate per head and must line up across row blocks"
    n_blocks = seq // GA_ROWS
    assert n_blocks >= 3 and n_blocks % 2 == 0
    n_chunks = batch * N_KV_A * n_blocks
    chunk_rows = []
    for w in weights:
        assert w.shape[0] == 1 and w.shape[1] % (BF16_SUBLANES * n_chunks) == 0, w.shape
        chunk_rows.append(w.shape[1] // n_chunks)
    n_cast = len(weights)
    k_col = N_HEADS_A
    v_col = N_HEADS_A + N_KV_A
    any_spec = pl.BlockSpec(memory_space=pl.ANY)
    attn_spec = pl.BlockSpec((seq, GROUP * HEAD_DIM), lambda b, k: (b, k))
    outs = pl.pallas_call(
        functools.partial(_global_attn_kernel, n_cast=n_cast),
        grid=(batch, N_KV_A),
        in_specs=[
            attn_spec,
            pl.BlockSpec((seq, HEAD_DIM), lambda b, k: (b, k_col + k)),
            pl.BlockSpec((seq, HEAD_DIM), lambda b, k: (b, v_col + k)),
        ] + [any_spec] * n_cast,
        out_specs=[attn_spec] + [any_spec] * n_cast,
        out_shape=[jax.ShapeDtypeStruct((batch * seq, N_HEADS_A * HEAD_DIM), BF16)]
        + [jax.ShapeDtypeStruct(w.shape[1:], BF16) for w in weights],
        scratch_shapes=[
            pltpu.VMEM((seq, 2 * HEAD_DIM), BF16),
            pltpu.VMEM((HEAD_DIM, seq), BF16),
            pltpu.VMEM((2, GA_ROWS, seq), F32),
            pltpu.VMEM((2, GA_ROWS, HEAD_DIM), F32),
        ] + [pltpu.VMEM((2, rows, w.shape[2]), F32) for rows, w in zip(chunk_rows, weights)]
        + [pltpu.VMEM((rows, w.shape[2]), BF16) for rows, w in zip(chunk_rows, weights)]
        + [pltpu.SemaphoreType.DMA((n_cast, 2)), pltpu.SemaphoreType.DMA((n_cast,))],
        compiler_params=_params("arbitrary", "arbitrary"),
        name="global_attn",
    )(proj, proj, proj, *weights)
    return outs[0], outs[1:]


WA_ROWS = 256
WA_BAND = WA_ROWS + 2 * WINDOW


def _window_attn_kernel(sink_ref, q_ref, k_ref, v_ref, line_ref, o_ref,
                        kpad_ref, v1pad_ref, bias_ref, sink_ref_b, s_ref, m_ref):
    kvh = pl.program_id(1)
    seq = q_ref.shape[0]
    n_stages = seq // WA_ROWS

    zeros = jnp.zeros((WINDOW, 2 * HEAD_DIM), BF16)
    kpad_ref[:WINDOW, :] = zeros[:, :HEAD_DIM]
    kpad_ref[WINDOW + seq:, :] = zeros[:, :HEAD_DIM]
    kpad_ref[WINDOW:WINDOW + seq, :] = k_ref[...]
    v1pad_ref[:WINDOW, :] = zeros
    v1pad_ref[WINDOW + seq:, :] = zeros
    v1pad_ref[WINDOW:WINDOW + seq, :HEAD_DIM] = v_ref[...]
    v1pad_ref[WINDOW:WINDOW + seq, HEAD_DIM:] = jnp.ones(v_ref.shape, BF16)

    for g in range(GROUP):
        head = kvh * GROUP + g
        tile = jnp.broadcast_to(line_ref[g, :1, :], (WA_ROWS, WA_BAND))
        bias_ref[g * WA_ROWS:(g + 1) * WA_ROWS, :] = pltpu.roll(tile, 0, 1, stride=1, stride_axis=0)
        sink_ref_b[g * WA_ROWS:(g + 1) * WA_ROWS, :] = jnp.full((WA_ROWS, HEAD_DIM), sink_ref[head] * LOG2_E, F32)

    col = lax.broadcasted_iota(jnp.int32, (GROUP * WA_ROWS, WA_BAND), 1)

    def rows_of(n):
        return pl.ds(pl.multiple_of(n * WA_ROWS, WA_ROWS), WA_ROWS)

    def band_of(n):
        return pl.ds(pl.multiple_of(n * WA_ROWS, WA_ROWS), WA_BAND)

    def scores(n, slot):
        q = jnp.concatenate([q_ref[rows_of(n), g * HEAD_DIM:(g + 1) * HEAD_DIM] for g in range(GROUP)], axis=0)
        s = lax.dot_general(q, kpad_ref[band_of(n), :], (((1,), (1,)), ((), ())), preferred_element_type=F32)
        in_range = ((col >= WINDOW) | (n > 0)) & ((col < WA_BAND - WINDOW) | (n < n_stages - 1))
        s = jnp.where(in_range, s + bias_ref[...], NEG_INF)
        s_ref[slot] = s
        row_max = jnp.broadcast_to(jnp.max(s, axis=-1, keepdims=True), sink_ref_b.shape)
        m_ref[slot] = jnp.maximum(row_max, sink_ref_b[...])

    def apply(n, slot):
        m = m_ref[slot]
        p = jnp.concatenate([jnp.exp2(s_ref[slot, :, c * HEAD_DIM:(c + 1) * HEAD_DIM] - m)
                             for c in range(WA_BAND // HEAD_DIM)], axis=1).astype(BF16)
        o = jnp.dot(p, v1pad_ref[band_of(n), :], preferred_element_type=F32)
        o = o[:, :HEAD_DIM] / (o[:, HEAD_DIM:] + jnp.exp2(sink_ref_b[...] - m))
        for g in range(GROUP):
            o_ref[rows_of(n), g * HEAD_DIM:(g + 1) * HEAD_DIM] = o[g * WA_ROWS:(g + 1) * WA_ROWS].astype(BF16)

    scores(0, 0)

    def body(i, carry):
        for slot in range(2):
            n = 2 * i + slot
            scores(n + 1, 1 - slot)
            apply(n, slot)
        return carry

    lax.fori_loop(0, n_stages // 2 - 1, body, 0)
    scores(n_stages - 1, 1)
    apply(n_stages - 2, 0)
    apply(n_stages - 1, 1)


def _window_attn(proj, bias_lines, sink, batch, seq):
    assert (seq // WA_ROWS) % 2 == 0, "the stage loop is unrolled by two so score slots are static"
    q_col = (N_HEADS_A + 2 * N_KV_A) // GROUP
    k_col = N_HEADS_A + 2 * N_KV_A + N_HEADS_B
    v_col = k_col + N_KV_B
    grid_spec = pltpu.PrefetchScalarGridSpec(
        num_scalar_prefetch=1,
        grid=(batch, N_KV_B),
        in_specs=[
            pl.BlockSpec((seq, GROUP * HEAD_DIM), lambda b, k, s: (b, q_col + k)),
            pl.BlockSpec((seq, HEAD_DIM), lambda b, k, s: (b, k_col + k)),
            pl.BlockSpec((seq, HEAD_DIM), lambda b, k, s: (b, v_col + k)),
            pl.BlockSpec((GROUP, F32_SUBLANES, WA_BAND), lambda b, k, s: (k, 0, 0)),
        ],
        out_specs=pl.BlockSpec((seq, GROUP * HEAD_DIM), lambda b, k, s: (b, k)),
        scratch_shapes=[
            pltpu.VMEM((seq + 2 * WINDOW, HEAD_DIM), BF16),
            pltpu.VMEM((seq + 2 * WINDOW, 2 * HEAD_DIM), BF16),
            pltpu.VMEM((GROUP * WA_ROWS, WA_BAND), F32),
            pltpu.VMEM((GROUP * WA_ROWS, HEAD_DIM), F32),
            pltpu.VMEM((2, GROUP * WA_ROWS, WA_BAND), F32),
            pltpu.VMEM((2, GROUP * WA_ROWS, HEAD_DIM), F32),
        ],
    )
    return pl.pallas_call(
        _window_attn_kernel,
        grid_spec=grid_spec,
        out_shape=jax.ShapeDtypeStruct((batch * seq, N_HEADS_B * HEAD_DIM), BF16),
        compiler_params=_params("parallel", "parallel"),
        name="window_attn",
    )(sink, proj, proj, proj, bias_lines)


OUT_TM = 512


def _out_proj_kernel(oa_ref, ob_ref, w_ref, x_ref, g_ref, h_ref, m_ref):
    ka = oa_ref.shape[1]
    for r in range(OUT_TM // SUB_ROWS):
        rows = pl.ds(r * SUB_ROWS, SUB_ROWS)
        h = x_ref[rows, :]
        h = h + jnp.dot(oa_ref[rows, :], w_ref[:ka, :], preferred_element_type=F32)
        h = h + jnp.dot(ob_ref[rows, :], w_ref[ka:, :], preferred_element_type=F32)
        h_ref[rows, :] = h
        m_ref[rows, :] = _rms(h, g_ref[...]).astype(BF16)


def _out_proj(oa, ob, w, x2, g):
    m, d = x2.shape
    ka, kb = oa.shape[1], ob.shape[1]
    return pl.pallas_call(
        _out_proj_kernel,
        grid=(m // OUT_TM,),
        in_specs=[
            pl.BlockSpec((OUT_TM, ka), lambda i: (i, 0)),
            pl.BlockSpec((OUT_TM, kb), lambda i: (i, 0)),
            pl.BlockSpec((ka + kb, d), lambda i: (0, 0)),
            pl.BlockSpec((OUT_TM, d), lambda i: (i, 0)),
            pl.BlockSpec((1, d), lambda i: (0, 0)),
        ],
        out_specs=[pl.BlockSpec((OUT_TM, d), lambda i: (i, 0)), pl.BlockSpec((OUT_TM, d), lambda i: (i, 0))],
        out_shape=[jax.ShapeDtypeStruct((m, d), F32), jax.ShapeDtypeStruct((m, d), BF16)],
        compiler_params=_params("parallel"),
        name="out_proj",
    )(oa, ob, w, x2, g)


MLP_TM = 1024
MLP_TF = 1024


def _mlp_kernel(m_ref, wu_ref, wd_ref, d_ref):
    @pl.when(pl.program_id(1) == 0)
    def _():
        d_ref[...] = jnp.zeros(d_ref.shape, F32)

    a = jnp.dot(m_ref[...], wu_ref[...], preferred_element_type=F32)
    act = jnp.square(jnp.maximum(a, 0.0)).astype(BF16)
    d_ref[...] += jnp.dot(act, wd_ref[...], preferred_element_type=F32)


def _mlp(mn, wu, wd):
    m, d = mn.shape
    dff = wu.shape[1]
    return pl.pallas_call(
        _mlp_kernel,
        grid=(m // MLP_TM, dff // MLP_TF),
        in_specs=[
            pl.BlockSpec((MLP_TM, d), lambda i, f: (i, 0)),
            pl.BlockSpec((d, MLP_TF), lambda i, f: (0, f)),
            pl.BlockSpec((MLP_TF, d), lambda i, f: (f, 0)),
        ],
        out_specs=pl.BlockSpec((MLP_TM, d), lambda i, f: (i, 0)),
        out_shape=jax.ShapeDtypeStruct((m, d), F32),
        compiler_params=_params("parallel", "arbitrary"),
        name="mlp",
    )(mn, wu, wd)


GATE_TM = 512


def _gate_final_kernel(h_ref, d_ref, wg_ref, p_ref, wp_ref, gg_ref, gp_ref, gf_ref, o_ref):
    for r in range(GATE_TM // SUB_ROWS):
        rows = pl.ds(r * SUB_ROWS, SUB_ROWS)
        h = h_ref[rows, :] + d_ref[rows, :]
        gate = jax.nn.sigmoid(jnp.dot(_rms(h, gg_ref[...]).astype(BF16), wg_ref[...], preferred_element_type=F32))
        e = _rms(jnp.dot(p_ref[rows, :].astype(BF16), wp_ref[...], preferred_element_type=F32), gp_ref[...])
        o_ref[rows, :] = _rms(h + gate * e, gf_ref[...])


def _gate_final(h, delta, wg, p2, wp, gg, gp, gf):
    m, d = h.shape
    dp = p2.shape[1]
    tile = pl.BlockSpec((GATE_TM, d), lambda i: (i, 0))
    gain = pl.BlockSpec((1, d), lambda i: (0, 0))
    return pl.pallas_call(
        _gate_final_kernel,
        grid=(m // GATE_TM,),
        in_specs=[
            tile,
            tile,
            pl.BlockSpec((d, d), lambda i: (0, 0)),
            pl.BlockSpec((GATE_TM, dp), lambda i: (i, 0)),
            pl.BlockSpec((dp, d), lambda i: (0, 0)),
            gain, gain, gain,
        ],
        out_specs=tile,
        out_shape=jax.ShapeDtypeStruct((m, d), F32),
        compiler_params=_params("parallel"),
        name="gate_final",
    )(h, delta, wg, p2, wp, gg, gp, gf)


def _rope_tables(seq):
    rows = seq // GRID_W
    row = np.repeat(np.arange(rows, dtype=np.float32), GRID_W)
    col = np.tile(np.arange(GRID_W, dtype=np.float32), rows)
    half = HEAD_DIM // 2
    inv_freq = np.float32(ROPE_THETA) ** (-np.arange(0, half, 2, dtype=np.float32) / np.float32(half))
    ang_r = row[:, None] * inv_freq
    ang_c = col[:, None] * inv_freq
    cr, sr, cc, sc = np.cos(ang_r), np.sin(ang_r), np.cos(ang_c), np.sin(ang_c)
    cos = np.concatenate([cr, cr, cc, cc], axis=-1).astype(np.float32)
    sin_signed = np.concatenate([-sr, sr, -sc, sc], axis=-1).astype(np.float32)
    return jnp.asarray(cos), jnp.asarray(sin_signed)


def _t5_bucket(rel):
    nb = N_BUCKETS // 2
    ret = jnp.where(rel > 0, nb, 0)
    n = jnp.abs(rel)
    max_exact = nb // 2
    nf = jnp.maximum(n, 1).astype(F32)
    large = max_exact + (jnp.log(nf / max_exact) / math.log(MAX_DISTANCE / max_exact)
                         * (nb - max_exact)).astype(jnp.int32)
    large = jnp.minimum(large, nb - 1)
    return ret + jnp.where(n < max_exact, n, large)


def _window_bias_lines(rel_bias_table):
    rel = jnp.arange(WA_BAND, dtype=jnp.int32) - WINDOW
    line = rel_bias_table[_t5_bucket(rel)].astype(F32).T * LOG2_E
    line = jnp.where((jnp.abs(rel) <= WINDOW)[None, :], line, NEG_INF)
    return jnp.broadcast_to(line[:, None, :], (line.shape[0], F32_SUBLANES, WA_BAND))


def kernel(x, p, attn_norm_g, w_in, q_norm_g, k_norm_g, sink_logits, w_out, mlp_norm_g, w_up, w_down, ple_w,
           ple_norm_g, gate_norm_g, w_gate, rel_bias_table, final_norm_g):
    batch, seq, d = x.shape
    assert w_in.shape[0] == 1, "gate_final fuses the final RMSNorm, which is only valid for a single layer"
    row = lambda v: v.reshape(1, -1).astype(F32)
    cos, sin_signed = _rope_tables(seq)
    h = x.reshape(batch * seq, d)
    proj = _in_proj(h, row(attn_norm_g), w_in, cos, sin_signed, row(q_norm_g), row(k_norm_g), seq)
    oa, (wu, wd, wo, wg) = _global_attn(proj, (w_up, w_down, w_out, w_gate), batch, seq)
    ob = _window_attn(proj, _window_bias_lines(rel_bias_table), sink_logits.reshape(-1).astype(F32), batch, seq)
    h1, mn = _out_proj(oa, ob, wo, h, row(mlp_norm_g))
    delta = _mlp(mn, wu, wd)
    out = _gate_final(h1, delta, wg, p.reshape(batch * seq, -1), ple_w[0].astype(BF16),
                      row(gate_norm_g), row(ple_norm_g), row(final_norm_g))
    return out.reshape(batch, seq, d)
```

```python
import functools
import math

import jax
import jax.numpy as jnp
import numpy as np
from jax import lax
from jax.experimental import pallas as pl
from jax.experimental.pallas import tpu as pltpu

HEAD_DIM = 128
N_HEADS_A = 8
N_KV_A = 2
N_HEADS_B = 8
N_KV_B = 2
GROUP = 4
GRID_W = 64
WINDOW = 128
N_BUCKETS = 32
MAX_DISTANCE = 128
ROPE_THETA = 10000.0
EPS = 1e-6
NEG_INF = -1e30
LOG2_E = math.log2(math.e)
Q_SCALE = HEAD_DIM ** -0.5 * LOG2_E

V7X_VMEM_BYTES = 64 * 1024 * 1024
VMEM_RESERVE_BYTES = 8 * 1024 * 1024
VMEM_LIMIT_BYTES = V7X_VMEM_BYTES - VMEM_RESERVE_BYTES
F32_SUBLANES = 8
BF16_SUBLANES = 16

SUB_ROWS = 256

BF16 = jnp.bfloat16
F32 = jnp.float32


def _params(*semantics):
    return pltpu.CompilerParams(dimension_semantics=semantics, vmem_limit_bytes=VMEM_LIMIT_BYTES)


def _rms(x, g):
    return x * lax.rsqrt(jnp.mean(x * x, axis=-1, keepdims=True) + EPS) * g


IN_TM = 512
IN_TN = 512
IN_W_ROWS = 128
IN_W_SLOTS = 4

_PLAIN, _Q_A, _K_A, _Q_B = range(4)
_HEAD_KINDS = ([_Q_A] * N_HEADS_A + [_K_A] * N_KV_A + [_PLAIN] * N_KV_A
               + [_Q_B] * N_HEADS_B + [_PLAIN] * (2 * N_KV_B))


def _rope(y, cos, sin_signed):
    lane = lax.broadcasted_iota(jnp.int32, y.shape, 1)
    partner = jnp.where((lane % 64) < 32, pltpu.roll(y, 96, 1), pltpu.roll(y, 32, 1))
    return y * cos + partner * sin_signed


def _in_proj_kernel(x_ref, g_ref, w_hbm, cos_ref, sin_ref, gq_ref, gk_ref, o_ref, w_ref, wbuf_ref, wsem):
    @pl.when(pl.program_id(0) == 0)
    def _():
        def chunk_copy(c):
            src = w_hbm.at[0, pl.ds(c * IN_W_ROWS, IN_W_ROWS)]
            return pltpu.make_async_copy(src, wbuf_ref.at[c % IN_W_SLOTS], wsem.at[c % IN_W_SLOTS])

        n_chunks = w_ref.shape[0] // IN_W_ROWS
        for c in range(IN_W_SLOTS - 1):
            chunk_copy(c).start()
        for c in range(n_chunks):
            chunk_copy(c).wait()
            if c + IN_W_SLOTS - 1 < n_chunks:
                chunk_copy(c + IN_W_SLOTS - 1).start()
            w_ref[c * IN_W_ROWS:(c + 1) * IN_W_ROWS, :] = wbuf_ref[c % IN_W_SLOTS].astype(BF16)

    heads_per_dot = IN_TN // HEAD_DIM
    for r in range(IN_TM // SUB_ROWS):
        rows = pl.ds(r * SUB_ROWS, SUB_ROWS)
        u = _rms(x_ref[rows, :], g_ref[...]).astype(BF16)
        cos, sin_signed = cos_ref[rows, :], sin_ref[rows, :]
        for c in range(w_ref.shape[1] // IN_TN):
            acc = jnp.dot(u, w_ref[:, c * IN_TN:(c + 1) * IN_TN], preferred_element_type=F32)
            for hh in range(heads_per_dot):
                head = c * heads_per_dot + hh
                a = acc[:, hh * HEAD_DIM:(hh + 1) * HEAD_DIM]
                kind = _HEAD_KINDS[head]
                if kind == _Q_A:
                    a = _rope(_rms(a, gq_ref[...]), cos, sin_signed) * Q_SCALE
                elif kind == _K_A:
                    a = _rope(_rms(a, gk_ref[...]), cos, sin_signed)
                elif kind == _Q_B:
                    a = a * Q_SCALE
                o_ref[rows, head * HEAD_DIM:(head + 1) * HEAD_DIM] = a.astype(BF16)


def _in_proj(x2, g, w, cos, sin_signed, gq, gk, seq):
    m, d = x2.shape
    n = w.shape[2]
    assert w.shape[0] == 1 and n == len(_HEAD_KINDS) * HEAD_DIM and d % IN_W_ROWS == 0
    pos_tiles = seq // IN_TM
    return pl.pallas_call(
        _in_proj_kernel,
        grid=(m // IN_TM,),
        in_specs=[
            pl.BlockSpec((IN_TM, d), lambda i: (i, 0)),
            pl.BlockSpec((1, d), lambda i: (0, 0)),
            pl.BlockSpec(memory_space=pl.ANY),
            pl.BlockSpec((IN_TM, HEAD_DIM), lambda i: (i % pos_tiles, 0)),
            pl.BlockSpec((IN_TM, HEAD_DIM), lambda i: (i % pos_tiles, 0)),
            pl.BlockSpec((1, HEAD_DIM), lambda i: (0, 0)),
            pl.BlockSpec((1, HEAD_DIM), lambda i: (0, 0)),
        ],
        out_specs=pl.BlockSpec((IN_TM, n), lambda i: (i, 0)),
        out_shape=jax.ShapeDtypeStruct((m, n), BF16),
        scratch_shapes=[
            pltpu.VMEM((d, n), BF16),
            pltpu.VMEM((IN_W_SLOTS, IN_W_ROWS, n), F32),
            pltpu.SemaphoreType.DMA((IN_W_SLOTS,)),
        ],
        compiler_params=_params("arbitrary"),
        name="in_proj",
    )(x2, g, w, cos, sin_signed, gq, gk)


GA_ROWS = 256
GA_KEYS = 512


def _global_attn_kernel(*refs, n_cast):
    q_ref, k_ref, v_ref = refs[:3]
    w_f32 = refs[3:3 + n_cast]
    o_ref = refs[3 + n_cast]
    w_bf16 = refs[4 + n_cast:4 + 2 * n_cast]
    v1_ref, kt_ref, s_ref, m_ref = refs[4 + 2 * n_cast:8 + 2 * n_cast]
    in_bufs = refs[8 + 2 * n_cast:8 + 3 * n_cast]
    out_bufs = refs[8 + 3 * n_cast:8 + 4 * n_cast]
    in_sem, out_sem = refs[8 + 4 * n_cast:]

    seq = q_ref.shape[0]
    n_blocks = seq // GA_ROWS
    first_chunk = (pl.program_id(0) * pl.num_programs(1) + pl.program_id(1)) * n_blocks

    def in_copy(w, chunk, slot):
        rows = in_bufs[w].shape[1]
        src = w_f32[w].at[0, pl.ds(pl.multiple_of(chunk * rows, rows), rows)]
        return pltpu.make_async_copy(src, in_bufs[w].at[slot], in_sem.at[w, slot])

    def out_copy(w, chunk):
        rows = out_bufs[w].shape[0]
        dst = w_bf16[w].at[pl.ds(pl.multiple_of(chunk * rows, rows), rows)]
        return pltpu.make_async_copy(out_bufs[w], dst, out_sem.at[w])

    def cast_chunk(r, first, last):
        chunk = first_chunk + r
        slot = r % 2
        for w in range(n_cast):
            in_copy(w, chunk, slot).wait()
            if not last:
                in_copy(w, chunk + 1, 1 - slot).start()
            if not first:
                out_copy(w, chunk - 1).wait()
            out_bufs[w][...] = in_bufs[w][slot].astype(BF16)
            out_copy(w, chunk).start()

    for w in range(n_cast):
        in_copy(w, first_chunk, 0).start()

    v1_ref[:, :HEAD_DIM] = v_ref[...]
    v1_ref[:, HEAD_DIM:] = jnp.ones(v_ref.shape, BF16)
    kt_ref[...] = k_ref[...].T

    def rows_of(r):
        return pl.ds(pl.multiple_of(r * GA_ROWS, GA_ROWS), GA_ROWS)

    def scores(r, g, slot):
        q = q_ref[rows_of(r), g * HEAD_DIM:(g + 1) * HEAD_DIM]
        part = None
        for c in range(seq // GA_KEYS):
            keys = slice(c * GA_KEYS, (c + 1) * GA_KEYS)
            s = jnp.dot(q, kt_ref[:, keys], preferred_element_type=F32)
            s_ref[slot, :, keys] = s
            for l in range(GA_KEYS // HEAD_DIM):
                piece = s[:, l * HEAD_DIM:(l + 1) * HEAD_DIM]
                part = piece if part is None else jnp.maximum(part, piece)
        m_ref[slot] = jnp.broadcast_to(jnp.max(part, axis=-1, keepdims=True), (GA_ROWS, HEAD_DIM))

    def apply(r, g, slot):
        m = m_ref[slot]
        o = jnp.zeros((GA_ROWS, 2 * HEAD_DIM), F32)
        for c in range(seq // GA_KEYS):
            keys = slice(c * GA_KEYS, (c + 1) * GA_KEYS)
            p = jnp.concatenate([jnp.exp2(s_ref[slot, :, c * GA_KEYS + l * HEAD_DIM:c * GA_KEYS + (l + 1) * HEAD_DIM] - m)
                                 for l in range(GA_KEYS // HEAD_DIM)], axis=1).astype(BF16)
            o = o + jnp.dot(p, v1_ref[keys, :], preferred_element_type=F32)
        o_ref[rows_of(r), g * HEAD_DIM:(g + 1) * HEAD_DIM] = (o[:, :HEAD_DIM] / o[:, HEAD_DIM:]).astype(BF16)

    def row_block(r, first, last):
        for g in range(GROUP):
            slot = g % 2
            if g + 1 < GROUP:
                scores(r, g + 1, 1 - slot)
            elif not last:
                scores(r + 1, 0, 1 - slot)
            apply(r, g, slot)
        cast_chunk(r, first, last)

    scores(0, 0, 0)
    row_block(0, first=True, last=False)

    def body(r, carry):
        row_block(r, first=False, last=False)
        return carry

    lax.fori_loop(1, n_blocks - 1, body, 0)
    row_block(n_blocks - 1, first=False, last=True)
    for w in range(n_cast):
        out_copy(w, first_chunk + n_blocks - 1).wait()


def _global_attn(proj, weights, batch, seq):
    assert GROUP % 2 == 0, "score slots alternate per head and must line up across row blocks"
    n_blocks = seq // GA_ROWS
    assert n_blocks >= 3 and n_blocks % 2 == 0
    n_chunks = batch * N_KV_A * n_blocks
    chunk_rows = []
    for w in weights:
        assert w.shape[0] == 1 and w.shape[1] % (BF16_SUBLANES * n_chunks) == 0, w.shape
        chunk_rows.append(w.shape[1] // n_chunks)
    n_cast = len(weights)
    k_col = N_HEADS_A
    v_col = N_HEADS_A + N_KV_A
    any_spec = pl.BlockSpec(memory_space=pl.ANY)
    attn_spec = pl.BlockSpec((seq, GROUP * HEAD_DIM), lambda b, k: (b, k))
    outs = pl.pallas_call(
        functools.partial(_global_attn_kernel, n_cast=n_cast),
        grid=(batch, N_KV_A),
        in_specs=[
            attn_spec,
            pl.BlockSpec((seq, HEAD_DIM), lambda b, k: (b, k_col + k)),
            pl.BlockSpec((seq, HEAD_DIM), lambda b, k: (b, v_col + k)),
        ] + [any_spec] * n_cast,
        out_specs=[attn_spec] + [any_spec] * n_cast,
        out_shape=[jax.ShapeDtypeStruct((batch * seq, N_HEADS_A * HEAD_DIM), BF16)]
        + [jax.ShapeDtypeStruct(w.shape[1:], BF16) for w in weights],
        scratch_shapes=[
            pltpu.VMEM((seq, 2 * HEAD_DIM), BF16),
            pltpu.VMEM((HEAD_DIM, seq), BF16),
            pltpu.VMEM((2, GA_ROWS, seq), F32),
            pltpu.VMEM((2, GA_ROWS, HEAD_DIM), F32),
        ] + [pltpu.VMEM((2, rows, w.shape[2]), F32) for rows, w in zip(chunk_rows, weights)]
        + [pltpu.VMEM((rows, w.shape[2]), BF16) for rows, w in zip(chunk_rows, weights)]
        + [pltpu.SemaphoreType.DMA((n_cast, 2)), pltpu.SemaphoreType.DMA((n_cast,))],
        compiler_params=_params("arbitrary", "arbitrary"),
        name="global_attn",
    )(proj, proj, proj, *weights)
    return outs[0], outs[1:]


WA_ROWS = 256
WA_BAND = WA_ROWS + 2 * WINDOW


def _window_attn_kernel(sink_ref, q_ref, k_ref, v_ref, line_ref, o_ref,
                        kpad_ref, v1pad_ref, bias_ref, sink_ref_b, s_ref, m_ref):
    kvh = pl.program_id(1)
    seq = q_ref.shape[0]
    n_stages = seq // WA_ROWS

    zeros = jnp.zeros((WINDOW, 2 * HEAD_DIM), BF16)
    kpad_ref[:WINDOW, :] = zeros[:, :HEAD_DIM]
    kpad_ref[WINDOW + seq:, :] = zeros[:, :HEAD_DIM]
    kpad_ref[WINDOW:WINDOW + seq, :] = k_ref[...]
    v1pad_ref[:WINDOW, :] = zeros
    v1pad_ref[WINDOW + seq:, :] = zeros
    v1pad_ref[WINDOW:WINDOW + seq, :HEAD_DIM] = v_ref[...]
    v1pad_ref[WINDOW:WINDOW + seq, HEAD_DIM:] = jnp.ones(v_ref.shape, BF16)

    for g in range(GROUP):
        head = kvh * GROUP + g
        tile = jnp.broadcast_to(line_ref[g, :1, :], (WA_ROWS, WA_BAND))
        bias_ref[g * WA_ROWS:(g + 1) * WA_ROWS, :] = pltpu.roll(tile, 0, 1, stride=1, stride_axis=0)
        sink_ref_b[g * WA_ROWS:(g + 1) * WA_ROWS, :] = jnp.full((WA_ROWS, HEAD_DIM), sink_ref[head] * LOG2_E, F32)

    col = lax.broadcasted_iota(jnp.int32, (GROUP * WA_ROWS, WA_BAND), 1)

    def rows_of(n):
        return pl.ds(pl.multiple_of(n * WA_ROWS, WA_ROWS), WA_ROWS)

    def band_of(n):
        return pl.ds(pl.multiple_of(n * WA_ROWS, WA_ROWS), WA_BAND)

    def scores(n, slot):
        q = jnp.concatenate([q_ref[rows_of(n), g * HEAD_DIM:(g + 1) * HEAD_DIM] for g in range(GROUP)], axis=0)
        s = lax.dot_general(q, kpad_ref[band_of(n), :], (((1,), (1,)), ((), ())), preferred_element_type=F32)
        in_range = ((col >= WINDOW) | (n > 0)) & ((col < WA_BAND - WINDOW) | (n < n_stages - 1))
        s = jnp.where(in_range, s + bias_ref[...], NEG_INF)
        s_ref[slot] = s
        row_max = jnp.broadcast_to(jnp.max(s, axis=-1, keepdims=True), sink_ref_b.shape)
        m_ref[slot] = jnp.maximum(row_max, sink_ref_b[...])

    def apply(n, slot):
        m = m_ref[slot]
        p = jnp.concatenate([jnp.exp2(s_ref[slot, :, c * HEAD_DIM:(c + 1) * HEAD_DIM] - m)
                             for c in range(WA_BAND // HEAD_DIM)], axis=1).astype(BF16)
        o = jnp.dot(p, v1pad_ref[band_of(n), :], preferred_element_type=F32)
        o = o[:, :HEAD_DIM] / (o[:, HEAD_DIM:] + jnp.exp2(sink_ref_b[...] - m))
        for g in range(GROUP):
            o_ref[rows_of(n), g * HEAD_DIM:(g + 1) * HEAD_DIM] = o[g * WA_ROWS:(g + 1) * WA_ROWS].astype(BF16)

    scores(0, 0)

    def body(i, carry):
        for slot in range(2):
            n = 2 * i + slot
            scores(n + 1, 1 - slot)
            apply(n, slot)
        return carry

    lax.fori_loop(0, n_stages // 2 - 1, body, 0)
    scores(n_stages - 1, 1)
    apply(n_stages - 2, 0)
    apply(n_stages - 1, 1)


def _window_attn(proj, bias_lines, sink, batch, seq):
    assert (seq // WA_ROWS) % 2 == 0, "the stage loop is unrolled by two so score slots are static"
    q_col = (N_HEADS_A + 2 * N_KV_A) // GROUP
    k_col = N_HEADS_A + 2 * N_KV_A + N_HEADS_B
    v_col = k_col + N_KV_B
    grid_spec = pltpu.PrefetchScalarGridSpec(
        num_scalar_prefetch=1,
        grid=(batch, N_KV_B),
        in_specs=[
            pl.BlockSpec((seq, GROUP * HEAD_DIM), lambda b, k, s: (b, q_col + k)),
            pl.BlockSpec((seq, HEAD_DIM), lambda b, k, s: (b, k_col + k)),
            pl.BlockSpec((seq, HEAD_DIM), lambda b, k, s: (b, v_col + k)),
            pl.BlockSpec((GROUP, F32_SUBLANES, WA_BAND), lambda b, k, s: (k, 0, 0)),
        ],
        out_specs=pl.BlockSpec((seq, GROUP * HEAD_DIM), lambda b, k, s: (b, k)),
        scratch_shapes=[
            pltpu.VMEM((seq + 2 * WINDOW, HEAD_DIM), BF16),
            pltpu.VMEM((seq + 2 * WINDOW, 2 * HEAD_DIM), BF16),
            pltpu.VMEM((GROUP * WA_ROWS, WA_BAND), F32),
            pltpu.VMEM((GROUP * WA_ROWS, HEAD_DIM), F32),
            pltpu.VMEM((2, GROUP * WA_ROWS, WA_BAND), F32),
            pltpu.VMEM((2, GROUP * WA_ROWS, HEAD_DIM), F32),
        ],
    )
    return pl.pallas_call(
        _window_attn_kernel,
        grid_spec=grid_spec,
        out_shape=jax.ShapeDtypeStruct((batch * seq, N_HEADS_B * HEAD_DIM), BF16),
        compiler_params=_params("parallel", "parallel"),
        name="window_attn",
    )(sink, proj, proj, proj, bias_lines)


OUT_TM = 512


def _out_proj_kernel(oa_ref, ob_ref, w_ref, x_ref, g_ref, h_ref, m_ref):
    ka = oa_ref.shape[1]
    for r in range(OUT_TM // SUB_ROWS):
        rows = pl.ds(r * SUB_ROWS, SUB_ROWS)
        h = x_ref[rows, :]
        h = h + jnp.dot(oa_ref[rows, :], w_ref[:ka, :], preferred_element_type=F32)
        h = h + jnp.dot(ob_ref[rows, :], w_ref[ka:, :], preferred_element_type=F32)
        h_ref[rows, :] = h
        m_ref[rows, :] = _rms(h, g_ref[...]).astype(BF16)


def _out_proj(oa, ob, w, x2, g):
    m, d = x2.shape
    ka, kb = oa.shape[1], ob.shape[1]
    return pl.pallas_call(
        _out_proj_kernel,
        grid=(m // OUT_TM,),
        in_specs=[
            pl.BlockSpec((OUT_TM, ka), lambda i: (i, 0)),
            pl.BlockSpec((OUT_TM, kb), lambda i: (i, 0)),
            pl.BlockSpec((ka + kb, d), lambda i: (0, 0)),
            pl.BlockSpec((OUT_TM, d), lambda i: (i, 0)),
            pl.BlockSpec((1, d), lambda i: (0, 0)),
        ],
        out_specs=[pl.BlockSpec((OUT_TM, d), lambda i: (i, 0)), pl.BlockSpec((OUT_TM, d), lambda i: (i, 0))],
        out_shape=[jax.ShapeDtypeStruct((m, d), F32), jax.ShapeDtypeStruct((m, d), BF16)],
        compiler_params=_params("parallel"),
        name="out_proj",
    )(oa, ob, w, x2, g)


MLP_TM = 1024
MLP_TF = 1024


def _mlp_kernel(m_ref, wu_ref, wd_ref, d_ref):
    @pl.when(pl.program_id(1) == 0)
    def _():
        d_ref[...] = jnp.zeros(d_ref.shape, F32)

    a = jnp.dot(m_ref[...], wu_ref[...], preferred_element_type=F32)
    act = jnp.square(jnp.maximum(a, 0.0)).astype(BF16)
    d_ref[...] += jnp.dot(act, wd_ref[...], preferred_element_type=F32)


def _mlp(mn, wu, wd):
    m, d = mn.shape
    dff = wu.shape[1]
    return pl.pallas_call(
        _mlp_kernel,
        grid=(m // MLP_TM, dff // MLP_TF),
        in_specs=[
            pl.BlockSpec((MLP_TM, d), lambda i, f: (i, 0)),
            pl.BlockSpec((d, MLP_TF), lambda i, f: (0, f)),
            pl.BlockSpec((MLP_TF, d), lambda i, f: (f, 0)),
        ],
        out_specs=pl.BlockSpec((MLP_TM, d), lambda i, f: (i, 0)),
        out_shape=jax.ShapeDtypeStruct((m, d), F32),
        compiler_params=_params("parallel", "arbitrary"),
        name="mlp",
    )(mn, wu, wd)


GATE_TM = 512


def _gate_final_kernel(h_ref, d_ref, wg_ref, p_ref, wp_ref, gg_ref, gp_ref, gf_ref, o_ref):
    for r in range(GATE_TM // SUB_ROWS):
        rows = pl.ds(r * SUB_ROWS, SUB_ROWS)
        h = h_ref[rows, :] + d_ref[rows, :]
        gate = jax.nn.sigmoid(jnp.dot(_rms(h, gg_ref[...]).astype(BF16), wg_ref[...], preferred_element_type=F32))
        e = _rms(jnp.dot(p_ref[rows, :].astype(BF16), wp_ref[...], preferred_element_type=F32), gp_ref[...])
        o_ref[rows, :] = _rms(h + gate * e, gf_ref[...])


def _gate_final(h, delta, wg, p2, wp, gg, gp, gf):
    m, d = h.shape
    dp = p2.shape[1]
    tile = pl.BlockSpec((GATE_TM, d), lambda i: (i, 0))
    gain = pl.BlockSpec((1, d), lambda i: (0, 0))
    return pl.pallas_call(
        _gate_final_kernel,
        grid=(m // GATE_TM,),
        in_specs=[
            tile,
            tile,
            pl.BlockSpec((d, d), lambda i: (0, 0)),
            pl.BlockSpec((GATE_TM, dp), lambda i: (i, 0)),
            pl.BlockSpec((dp, d), lambda i: (0, 0)),
            gain, gain, gain,
        ],
        out_specs=tile,
        out_shape=jax.ShapeDtypeStruct((m, d), F32),
        compiler_params=_params("parallel"),
        name="gate_final",
    )(h, delta, wg, p2, wp, gg, gp, gf)


def _rope_tables(seq):
    rows = seq // GRID_W
    row = np.repeat(np.arange(rows, dtype=np.float32), GRID_W)
    col = np.tile(np.arange(GRID_W, dtype=np.float32), rows)
    half = HEAD_DIM // 2
    inv_freq = np.float32(ROPE_THETA) ** (-np.arange(0, half, 2, dtype=np.float32) / np.float32(half))
    ang_r = row[:, None] * inv_freq
    ang_c = col[:, None] * inv_freq
    cr, sr, cc, sc = np.cos(ang_r), np.sin(ang_r), np.cos(ang_c), np.sin(ang_c)
    cos = np.concatenate([cr, cr, cc, cc], axis=-1).astype(np.float32)
    sin_signed = np.concatenate([-sr, sr, -sc, sc], axis=-1).astype(np.float32)
    return jnp.asarray(cos), jnp.asarray(sin_signed)


def _t5_bucket(rel):
    nb = N_BUCKETS // 2
    ret = jnp.where(rel > 0, nb, 0)
    n = jnp.abs(rel)
    max_exact = nb // 2
    nf = jnp.maximum(n, 1).astype(F32)
    large = max_exact + (jnp.log(nf / max_exact) / math.log(MAX_DISTANCE / max_exact)
                         * (nb - max_exact)).astype(jnp.int32)
    large = jnp.minimum(large, nb - 1)
    return ret + jnp.where(n < max_exact, n, large)


def _window_bias_lines(rel_bias_table):
    rel = jnp.arange(WA_BAND, dtype=jnp.int32) - WINDOW
    line = rel_bias_table[_t5_bucket(rel)].astype(F32).T * LOG2_E
    line = jnp.where((jnp.abs(rel) <= WINDOW)[None, :], line, NEG_INF)
    return jnp.broadcast_to(line[:, None, :], (line.shape[0], F32_SUBLANES, WA_BAND))


def kernel(x, p, attn_norm_g, w_in, q_norm_g, k_norm_g, sink_logits, w_out, mlp_norm_g, w_up, w_down, ple_w,
           ple_norm_g, gate_norm_g, w_gate, rel_bias_table, final_norm_g):
    batch, seq, d = x.shape
    assert w_in.shape[0] == 1, "gate_final fuses the final RMSNorm, which is only valid for a single layer"
    row = lambda v: v.reshape(1, -1).astype(F32)
    cos, sin_signed = _rope_tables(seq)
    h = x.reshape(batch * seq, d)
    proj = _in_proj(h, row(attn_norm_g), w_in, cos, sin_signed, row(q_norm_g), row(k_norm_g), seq)
    oa, (wu, wd, wo, wg) = _global_attn(proj, (w_up, w_down, w_out, w_gate), batch, seq)
    ob = _window_attn(proj, _window_bias_lines(rel_bias_table), sink_logits.reshape(-1).astype(F32), batch, seq)
    h1, mn = _out_proj(oa, ob, wo, h, row(mlp_norm_g))
    delta = _mlp(mn, wu, wd)
    out = _gate_final(h1, delta, wg, p.reshape(batch * seq, -1), ple_w[0].astype(BF16),
                      row(gate_norm_g), row(ple_norm_g), row(final_norm_g))
    return out.reshape(batch, seq, d)
```

```python
import functools
import math

import jax
import jax.numpy as jnp
import numpy as np
from jax import lax
from jax.experimental import pallas as pl
from jax.experimental.pallas import tpu as pltpu

HEAD_DIM = 128
N_HEADS_A = 8
N_KV_A = 2
N_HEADS_B = 8
N_KV_B = 2
GROUP = 4
GRID_W = 64
WINDOW = 128
N_BUCKETS = 32
MAX_DISTANCE = 128
ROPE_THETA = 10000.0
EPS = 1e-6
NEG_INF = -1e30
LOG2_E = math.log2(math.e)
Q_SCALE = HEAD_DIM ** -0.5 * LOG2_E

V7X_VMEM_BYTES = 64 * 1024 * 1024
VMEM_RESERVE_BYTES = 8 * 1024 * 1024
VMEM_LIMIT_BYTES = V7X_VMEM_BYTES - VMEM_RESERVE_BYTES
F32_SUBLANES = 8
BF16_SUBLANES = 16

SUB_ROWS = 256

BF16 = jnp.bfloat16
F32 = jnp.float32


def _params(*semantics):
    return pltpu.CompilerParams(dimension_semantics=semantics, vmem_limit_bytes=VMEM_LIMIT_BYTES)


def _rms(x, g):
    return x * lax.rsqrt(jnp.mean(x * x, axis=-1, keepdims=True) + EPS) * g


IN_TM = 512
IN_TN = 512
IN_W_SLOTS = 3

_PLAIN, _Q_A, _K_A, _Q_B = range(4)
_HEAD_KINDS = ([_Q_A] * N_HEADS_A + [_K_A] * N_KV_A + [_PLAIN] * N_KV_A
               + [_Q_B] * N_HEADS_B + [_PLAIN] * (2 * N_KV_B))


def _rope(y, cos, sin_signed):
    lane = lax.broadcasted_iota(jnp.int32, y.shape, 1)
    partner = jnp.where((lane % 64) < 32, pltpu.roll(y, 96, 1), pltpu.roll(y, 32, 1))
    return y * cos + partner * sin_signed


def _in_proj_kernel(x_ref, g_ref, w_hbm, cos_ref, sin_ref, gq_ref, gk_ref, o_ref, w_ref, wbuf_ref, wsem):
    heads_per_dot = IN_TN // HEAD_DIM
    n_col_tiles = w_ref.shape[1] // IN_TN

    def normed_rows(r):
        rows = pl.ds(r * SUB_ROWS, SUB_ROWS)
        return rows, _rms(x_ref[rows, :], g_ref[...]).astype(BF16), cos_ref[rows, :], sin_ref[rows, :]

    def column_tile(c, sub_tile):
        rows, u, cos, sin_signed = sub_tile
        acc = jnp.dot(u, w_ref[:, c * IN_TN:(c + 1) * IN_TN], preferred_element_type=F32)
        for hh in range(heads_per_dot):
            head = c * heads_per_dot + hh
            a = acc[:, hh * HEAD_DIM:(hh + 1) * HEAD_DIM]
            kind = _HEAD_KINDS[head]
            if kind == _Q_A:
                a = _rope(_rms(a, gq_ref[...]), cos, sin_signed) * Q_SCALE
            elif kind == _K_A:
                a = _rope(_rms(a, gk_ref[...]), cos, sin_signed)
            elif kind == _Q_B:
                a = a * Q_SCALE
            o_ref[rows, head * HEAD_DIM:(head + 1) * HEAD_DIM] = a.astype(BF16)

    @pl.when(pl.program_id(0) == 0)
    def _():
        def tile_copy(c):
            src = w_hbm.at[0, :, pl.ds(c * IN_TN, IN_TN)]
            return pltpu.make_async_copy(src, wbuf_ref.at[c % IN_W_SLOTS], wsem.at[c % IN_W_SLOTS])

        for c in range(IN_W_SLOTS - 1):
            tile_copy(c).start()
        sub_tiles = [normed_rows(r) for r in range(IN_TM // SUB_ROWS)]
        for c in range(n_col_tiles):
            tile_copy(c).wait()
            if c + IN_W_SLOTS - 1 < n_col_tiles:
                tile_copy(c + IN_W_SLOTS - 1).start()
            w_ref[:, c * IN_TN:(c + 1) * IN_TN] = wbuf_ref[c % IN_W_SLOTS].astype(BF16)
            for sub_tile in sub_tiles:
                column_tile(c, sub_tile)

    @pl.when(pl.program_id(0) > 0)
    def _():
        for r in range(IN_TM // SUB_ROWS):
            sub_tile = normed_rows(r)
            for c in range(n_col_tiles):
                column_tile(c, sub_tile)


def _in_proj(x2, g, w, cos, sin_signed, gq, gk, seq):
    m, d = x2.shape
    n = w.shape[2]
    assert w.shape[0] == 1 and n == len(_HEAD_KINDS) * HEAD_DIM
    pos_tiles = seq // IN_TM
    return pl.pallas_call(
        _in_proj_kernel,
        grid=(m // IN_TM,),
        in_specs=[
            pl.BlockSpec((IN_TM, d), lambda i: (i, 0)),
            pl.BlockSpec((1, d), lambda i: (0, 0)),
            pl.BlockSpec(memory_space=pl.ANY),
            pl.BlockSpec((IN_TM, HEAD_DIM), lambda i: (i % pos_tiles, 0)),
            pl.BlockSpec((IN_TM, HEAD_DIM), lambda i: (i % pos_tiles, 0)),
            pl.BlockSpec((1, HEAD_DIM), lambda i: (0, 0)),
            pl.BlockSpec((1, HEAD_DIM), lambda i: (0, 0)),
        ],
        out_specs=pl.BlockSpec((IN_TM, n), lambda i: (i, 0)),
        out_shape=jax.ShapeDtypeStruct((m, n), BF16),
        scratch_shapes=[
            pltpu.VMEM((d, n), BF16),
            pltpu.VMEM((IN_W_SLOTS, d, IN_TN), F32),
            pltpu.SemaphoreType.DMA((IN_W_SLOTS,)),
        ],
        compiler_params=_params("arbitrary"),
        name="in_proj",
    )(x2, g, w, cos, sin_signed, gq, gk)


GA_ROWS = 256
GA_KEYS = 512


def _global_attn_kernel(*refs, n_cast):
    q_ref, k_ref, v_ref = refs[:3]
    w_f32 = refs[3:3 + n_cast]
    o_ref = refs[3 + n_cast]
    w_bf16 = refs[4 + n_cast:4 + 2 * n_cast]
    v1_ref, kt_ref, s_ref, m_ref = refs[4 + 2 * n_cast:8 + 2 * n_cast]
    in_bufs = refs[8 + 2 * n_cast:8 + 3 * n_cast]
    out_bufs = refs[8 + 3 * n_cast:8 + 4 * n_cast]
    in_sem, out_sem = refs[8 + 4 * n_cast:]

    seq = q_ref.shape[0]
    n_blocks = seq // GA_ROWS
    first_chunk = (pl.program_id(0) * pl.num_programs(1) + pl.program_id(1)) * n_blocks

    def in_copy(w, chunk, slot):
        rows = in_bufs[w].shape[1]
        src = w_f32[w].at[0, pl.ds(pl.multiple_of(chunk * rows, rows), rows)]
        return pltpu.make_async_copy(src, in_bufs[w].at[slot], in_sem.at[w, slot])

    def out_copy(w, chunk):
        rows = out_bufs[w].shape[0]
        dst = w_bf16[w].at[pl.ds(pl.multiple_of(chunk * rows, rows), rows)]
        return pltpu.make_async_copy(out_bufs[w], dst, out_sem.at[w])

    def cast_chunk(r, first, last):
        chunk = first_chunk + r
        slot = r % 2
        for w in range(n_cast):
            in_copy(w, chunk, slot).wait()
            if not last:
                in_copy(w, chunk + 1, 1 - slot).start()
            if not first:
                out_copy(w, chunk - 1).wait()
            out_bufs[w][...] = in_bufs[w][slot].astype(BF16)
            out_copy(w, chunk).start()

    for w in range(n_cast):
        in_copy(w, first_chunk, 0).start()

    v1_ref[:, :HEAD_DIM] = v_ref[...]
    v1_ref[:, HEAD_DIM:] = jnp.ones(v_ref.shape, BF16)
    kt_ref[...] = k_ref[...].T

    def rows_of(r):
        return pl.ds(pl.multiple_of(r * GA_ROWS, GA_ROWS), GA_ROWS)

    def scores(r, g, slot):
        q = q_ref[rows_of(r), g * HEAD_DIM:(g + 1) * HEAD_DIM]
        part = None
        for c in range(seq // GA_KEYS):
            keys = slice(c * GA_KEYS, (c + 1) * GA_KEYS)
            s = jnp.dot(q, kt_ref[:, keys], preferred_element_type=F32)
            s_ref[slot, :, keys] = s
            for l in range(GA_KEYS // HEAD_DIM):
                piece = s[:, l * HEAD_DIM:(l + 1) * HEAD_DIM]
                part = piece if part is None else jnp.maximum(part, piece)
        m_ref[slot] = jnp.broadcast_to(jnp.max(part, axis=-1, keepdims=True), (GA_ROWS, HEAD_DIM))

    def apply(r, g, slot):
        m = m_ref[slot]
        o = jnp.zeros((GA_ROWS, 2 * HEAD_DIM), F32)
        for c in range(seq // GA_KEYS):
            keys = slice(c * GA_KEYS, (c + 1) * GA_KEYS)
            p = jnp.concatenate([jnp.exp2(s_ref[slot, :, c * GA_KEYS + l * HEAD_DIM:c * GA_KEYS + (l + 1) * HEAD_DIM] - m)
                                 for l in range(GA_KEYS // HEAD_DIM)], axis=1).astype(BF16)
            o = o + jnp.dot(p, v1_ref[keys, :], preferred_element_type=F32)
        o_ref[rows_of(r), g * HEAD_DIM:(g + 1) * HEAD_DIM] = (o[:, :HEAD_DIM] / o[:, HEAD_DIM:]).astype(BF16)

    def row_block(r, first, last):
        for g in range(GROUP):
            slot = g % 2
            if g + 1 < GROUP:
                scores(r, g + 1, 1 - slot)
            elif not last:
                scores(r + 1, 0, 1 - slot)
            apply(r, g, slot)
        cast_chunk(r, first, last)

    scores(0, 0, 0)
    row_block(0, first=True, last=False)

    def body(r, carry):
        row_block(r, first=False, last=False)
        return carry

    lax.fori_loop(1, n_blocks - 1, body, 0)
    row_block(n_blocks - 1, first=False, last=True)
    for w in range(n_cast):
        out_copy(w, first_chunk + n_blocks - 1).wait()


def _global_attn(proj, weights, batch, seq):
    assert GROUP % 2 == 0, "score slots alternate per head and must line up across row blocks"
    n_blocks = seq // GA_ROWS
    assert n_blocks >= 3 and n_blocks % 2 == 0
    n_chunks = batch * N_KV_A * n_blocks
    chunk_rows = []
    for w in weights:
        assert w.shape[0] == 1 and w.shape[1] % (BF16_SUBLANES * n_chunks) == 0, w.shape
        chunk_rows.append(w.shape[1] // n_chunks)
    n_cast = len(weights)
    k_col = N_HEADS_A
    v_col = N_HEADS_A + N_KV_A
    any_spec = pl.BlockSpec(memory_space=pl.ANY)
    attn_spec = pl.BlockSpec((seq, GROUP * HEAD_DIM), lambda b, k: (b, k))
    outs = pl.pallas_call(
        functools.partial(_global_attn_kernel, n_cast=n_cast),
        grid=(batch, N_KV_A),
        in_specs=[
            attn_spec,
            pl.BlockSpec((seq, HEAD_DIM), lambda b, k: (b, k_col + k)),
            pl.BlockSpec((seq, HEAD_DIM), lambda b, k: (b, v_col + k)),
        ] + [any_spec] * n_cast,
        out_specs=[attn_spec] + [any_spec] * n_cast,
        out_shape=[jax.ShapeDtypeStruct((batch * seq, N_HEADS_A * HEAD_DIM), BF16)]
        + [jax.ShapeDtypeStruct(w.shape[1:], BF16) for w in weights],
        scratch_shapes=[
            pltpu.VMEM((seq, 2 * HEAD_DIM), BF16),
            pltpu.VMEM((HEAD_DIM, seq), BF16),
            pltpu.VMEM((2, GA_ROWS, seq), F32),
            pltpu.VMEM((2, GA_ROWS, HEAD_DIM), F32),
        ] + [pltpu.VMEM((2, rows, w.shape[2]), F32) for rows, w in zip(chunk_rows, weights)]
        + [pltpu.VMEM((rows, w.shape[2]), BF16) for rows, w in zip(chunk_rows, weights)]
        + [pltpu.SemaphoreType.DMA((n_cast, 2)), pltpu.SemaphoreType.DMA((n_cast,))],
        compiler_params=_params("arbitrary", "arbitrary"),
        name="global_attn",
    )(proj, proj, proj, *weights)
    return outs[0], outs[1:]


WA_ROWS = 256
WA_BAND = WA_ROWS + 2 * WINDOW


def _window_attn_kernel(sink_ref, q_ref, k_ref, v_ref, line_ref, o_ref,
                        kpad_ref, v1pad_ref, bias_ref, sink_ref_b, s_ref, m_ref):
    kvh = pl.program_id(1)
    seq = q_ref.shape[0]
    n_stages = seq // WA_ROWS

    zeros = jnp.zeros((WINDOW, 2 * HEAD_DIM), BF16)
    kpad_ref[:WINDOW, :] = zeros[:, :HEAD_DIM]
    kpad_ref[WINDOW + seq:, :] = zeros[:, :HEAD_DIM]
    kpad_ref[WINDOW:WINDOW + seq, :] = k_ref[...]
    v1pad_ref[:WINDOW, :] = zeros
    v1pad_ref[WINDOW + seq:, :] = zeros
    v1pad_ref[WINDOW:WINDOW + seq, :HEAD_DIM] = v_ref[...]
    v1pad_ref[WINDOW:WINDOW + seq, HEAD_DIM:] = jnp.ones(v_ref.shape, BF16)

    for g in range(GROUP):
        head = kvh * GROUP + g
        tile = jnp.broadcast_to(line_ref[g, :1, :], (WA_ROWS, WA_BAND))
        bias_ref[g * WA_ROWS:(g + 1) * WA_ROWS, :] = pltpu.roll(tile, 0, 1, stride=1, stride_axis=0)
        sink_ref_b[g * WA_ROWS:(g + 1) * WA_ROWS, :] = jnp.full((WA_ROWS, HEAD_DIM), sink_ref[head] * LOG2_E, F32)

    col = lax.broadcasted_iota(jnp.int32, (GROUP * WA_ROWS, WA_BAND), 1)

    def rows_of(n):
        return pl.ds(pl.multiple_of(n * WA_ROWS, WA_ROWS), WA_ROWS)

    def band_of(n):
        return pl.ds(pl.multiple_of(n * WA_ROWS, WA_ROWS), WA_BAND)

    def scores(n, slot):
        q = jnp.concatenate([q_ref[rows_of(n), g * HEAD_DIM:(g + 1) * HEAD_DIM] for g in range(GROUP)], axis=0)
        s = lax.dot_general(q, kpad_ref[band_of(n), :], (((1,), (1,)), ((), ())), preferred_element_type=F32)
        in_range = ((col >= WINDOW) | (n > 0)) & ((col < WA_BAND - WINDOW) | (n < n_stages - 1))
        s = jnp.where(in_range, s + bias_ref[...], NEG_INF)
        s_ref[slot] = s
        row_max = jnp.broadcast_to(jnp.max(s, axis=-1, keepdims=True), sink_ref_b.shape)
        m_ref[slot] = jnp.maximum(row_max, sink_ref_b[...])

    def apply(n, slot):
        m = m_ref[slot]
        p = jnp.concatenate([jnp.exp2(s_ref[slot, :, c * HEAD_DIM:(c + 1) * HEAD_DIM] - m)
                             for c in range(WA_BAND // HEAD_DIM)], axis=1).astype(BF16)
        o = jnp.dot(p, v1pad_ref[band_of(n), :], preferred_element_type=F32)
        o = o[:, :HEAD_DIM] / (o[:, HEAD_DIM:] + jnp.exp2(sink_ref_b[...] - m))
        for g in range(GROUP):
            o_ref[rows_of(n), g * HEAD_DIM:(g + 1) * HEAD_DIM] = o[g * WA_ROWS:(g + 1) * WA_ROWS].astype(BF16)

    scores(0, 0)

    def body(i, carry):
        for slot in range(2):
            n = 2 * i + slot
            scores(n + 1, 1 - slot)
            apply(n, slot)
        return carry

    lax.fori_loop(0, n_stages // 2 - 1, body, 0)
    scores(n_stages - 1, 1)
    apply(n_stages - 2, 0)
    apply(n_stages - 1, 1)


def _window_attn(proj, bias_lines, sink, batch, seq):
    assert (seq // WA_ROWS) % 2 == 0, "the stage loop is unrolled by two so score slots are static"
    q_col = (N_HEADS_A + 2 * N_KV_A) // GROUP
    k_col = N_HEADS_A + 2 * N_KV_A + N_HEADS_B
    v_col = k_col + N_KV_B
    grid_spec = pltpu.PrefetchScalarGridSpec(
        num_scalar_prefetch=1,
        grid=(batch, N_KV_B),
        in_specs=[
            pl.BlockSpec((seq, GROUP * HEAD_DIM), lambda b, k, s: (b, q_col + k)),
            pl.BlockSpec((seq, HEAD_DIM), lambda b, k, s: (b, k_col + k)),
            pl.BlockSpec((seq, HEAD_DIM), lambda b, k, s: (b, v_col + k)),
            pl.BlockSpec((GROUP, F32_SUBLANES, WA_BAND), lambda b, k, s: (k, 0, 0)),
        ],
        out_specs=pl.BlockSpec((seq, GROUP * HEAD_DIM), lambda b, k, s: (b, k)),
        scratch_shapes=[
            pltpu.VMEM((seq + 2 * WINDOW, HEAD_DIM), BF16),
            pltpu.VMEM((seq + 2 * WINDOW, 2 * HEAD_DIM), BF16),
            pltpu.VMEM((GROUP * WA_ROWS, WA_BAND), F32),
            pltpu.VMEM((GROUP * WA_ROWS, HEAD_DIM), F32),
            pltpu.VMEM((2, GROUP * WA_ROWS, WA_BAND), F32),
            pltpu.VMEM((2, GROUP * WA_ROWS, HEAD_DIM), F32),
        ],
    )
    return pl.pallas_call(
        _window_attn_kernel,
        grid_spec=grid_spec,
        out_shape=jax.ShapeDtypeStruct((batch * seq, N_HEADS_B * HEAD_DIM), BF16),
        compiler_params=_params("parallel", "parallel"),
        name="window_attn",
    )(sink, proj, proj, proj, bias_lines)


OUT_TM = 512


def _out_proj_kernel(oa_ref, ob_ref, w_ref, x_ref, g_ref, h_ref, m_ref):
    ka = oa_ref.shape[1]
    for r in range(OUT_TM // SUB_ROWS):
        rows = pl.ds(r * SUB_ROWS, SUB_ROWS)
        h = x_ref[rows, :]
        h = h + jnp.dot(oa_ref[rows, :], w_ref[:ka, :], preferred_element_type=F32)
        h = h + jnp.dot(ob_ref[rows, :], w_ref[ka:, :], preferred_element_type=F32)
        h_ref[rows, :] = h
        m_ref[rows, :] = _rms(h, g_ref[...]).astype(BF16)


def _out_proj(oa, ob, w, x2, g):
    m, d = x2.shape
    ka, kb = oa.shape[1], ob.shape[1]
    return pl.pallas_call(
        _out_proj_kernel,
        grid=(m // OUT_TM,),
        in_specs=[
            pl.BlockSpec((OUT_TM, ka), lambda i: (i, 0)),
            pl.BlockSpec((OUT_TM, kb), lambda i: (i, 0)),
            pl.BlockSpec((ka + kb, d), lambda i: (0, 0)),
            pl.BlockSpec((OUT_TM, d), lambda i: (i, 0)),
            pl.BlockSpec((1, d), lambda i: (0, 0)),
        ],
        out_specs=[pl.BlockSpec((OUT_TM, d), lambda i: (i, 0)), pl.BlockSpec((OUT_TM, d), lambda i: (i, 0))],
        out_shape=[jax.ShapeDtypeStruct((m, d), F32), jax.ShapeDtypeStruct((m, d), BF16)],
        compiler_params=_params("parallel"),
        name="out_proj",
    )(oa, ob, w, x2, g)


MLP_TM = 1024
MLP_TF = 1024


def _mlp_kernel(m_ref, wu_ref, wd_ref, d_ref):
    @pl.when(pl.program_id(1) == 0)
    def _():
        d_ref[...] = jnp.zeros(d_ref.shape, F32)

    a = jnp.dot(m_ref[...], wu_ref[...], preferred_element_type=F32)
    act = jnp.square(jnp.maximum(a, 0.0)).astype(BF16)
    d_ref[...] += jnp.dot(act, wd_ref[...], preferred_element_type=F32)


def _mlp(mn, wu, wd):
    m, d = mn.shape
    dff = wu.shape[1]
    return pl.pallas_call(
        _mlp_kernel,
        grid=(m // MLP_TM, dff // MLP_TF),
        in_specs=[
            pl.BlockSpec((MLP_TM, d), lambda i, f: (i, 0)),
            pl.BlockSpec((d, MLP_TF), lambda i, f: (0, f)),
            pl.BlockSpec((MLP_TF, d), lambda i, f: (f, 0)),
        ],
        out_specs=pl.BlockSpec((MLP_TM, d), lambda i, f: (i, 0)),
        out_shape=jax.ShapeDtypeStruct((m, d), F32),
        compiler_params=_params("parallel", "arbitrary"),
        name="mlp",
    )(mn, wu, wd)


GATE_TM = 512


def _gate_final_kernel(h_ref, d_ref, wg_ref, p_ref, wp_ref, gg_ref, gp_ref, gf_ref, o_ref):
    for r in range(GATE_TM // SUB_ROWS):
        rows = pl.ds(r * SUB_ROWS, SUB_ROWS)
        h = h_ref[rows, :] + d_ref[rows, :]
        gate = jax.nn.sigmoid(jnp.dot(_rms(h, gg_ref[...]).astype(BF16), wg_ref[...], preferred_element_type=F32))
        e = _rms(jnp.dot(p_ref[rows, :].astype(BF16), wp_ref[...], preferred_element_type=F32), gp_ref[...])
        o_ref[rows, :] = _rms(h + gate * e, gf_ref[...])


def _gate_final(h, delta, wg, p2, wp, gg, gp, gf):
    m, d = h.shape
    dp = p2.shape[1]
    tile = pl.BlockSpec((GATE_TM, d), lambda i: (i, 0))
    gain = pl.BlockSpec((1, d), lambda i: (0, 0))
    return pl.pallas_call(
        _gate_final_kernel,
        grid=(m // GATE_TM,),
        in_specs=[
            tile,
            tile,
            pl.BlockSpec((d, d), lambda i: (0, 0)),
            pl.BlockSpec((GATE_TM, dp), lambda i: (i, 0)),
            pl.BlockSpec((dp, d), lambda i: (0, 0)),
            gain, gain, gain,
        ],
        out_specs=tile,
        out_shape=jax.ShapeDtypeStruct((m, d), F32),
        compiler_params=_params("parallel"),
        name="gate_final",
    )(h, delta, wg, p2, wp, gg, gp, gf)


def _rope_tables(seq):
    rows = seq // GRID_W
    row = np.repeat(np.arange(rows, dtype=np.float32), GRID_W)
    col = np.tile(np.arange(GRID_W, dtype=np.float32), rows)
    half = HEAD_DIM // 2
    inv_freq = np.float32(ROPE_THETA) ** (-np.arange(0, half, 2, dtype=np.float32) / np.float32(half))
    ang_r = row[:, None] * inv_freq
    ang_c = col[:, None] * inv_freq
    cr, sr, cc, sc = np.cos(ang_r), np.sin(ang_r), np.cos(ang_c), np.sin(ang_c)
    cos = np.concatenate([cr, cr, cc, cc], axis=-1).astype(np.float32)
    sin_signed = np.concatenate([-sr, sr, -sc, sc], axis=-1).astype(np.float32)
    return jnp.asarray(cos), jnp.asarray(sin_signed)


def _t5_bucket(rel):
    nb = N_BUCKETS // 2
    ret = jnp.where(rel > 0, nb, 0)
    n = jnp.abs(rel)
    max_exact = nb // 2
    nf = jnp.maximum(n, 1).astype(F32)
    large = max_exact + (jnp.log(nf / max_exact) / math.log(MAX_DISTANCE / max_exact)
                         * (nb - max_exact)).astype(jnp.int32)
    large = jnp.minimum(large, nb - 1)
    return ret + jnp.where(n < max_exact, n, large)


def _window_bias_lines(rel_bias_table):
    rel = jnp.arange(WA_BAND, dtype=jnp.int32) - WINDOW
    line = rel_bias_table[_t5_bucket(rel)].astype(F32).T * LOG2_E
    line = jnp.where((jnp.abs(rel) <= WINDOW)[None, :], line, NEG_INF)
    return jnp.broadcast_to(line[:, None, :], (line.shape[0], F32_SUBLANES, WA_BAND))


def kernel(x, p, attn_norm_g, w_in, q_norm_g, k_norm_g, sink_logits, w_out, mlp_norm_g, w_up, w_down, ple_w,
           ple_norm_g, gate_norm_g, w_gate, rel_bias_table, final_norm_g):
    batch, seq, d = x.shape
    assert w_in.shape[0] == 1, "gate_final fuses the final RMSNorm, which is only valid for a single layer"
    row = lambda v: v.reshape(1, -1).astype(F32)
    cos, sin_signed = _rope_tables(seq)
    h = x.reshape(batch * seq, d)
    proj = _in_proj(h, row(attn_norm_g), w_in, cos, sin_signed, row(q_norm_g), row(k_norm_g), seq)
    oa, (wu, wd, wo, wg) = _global_attn(proj, (w_up, w_down, w_out, w_gate), batch, seq)
    ob = _window_attn(proj, _window_bias_lines(rel_bias_table), sink_logits.reshape(-1).astype(F32), batch, seq)
    h1, mn = _out_proj(oa, ob, wo, h, row(mlp_norm_g))
    delta = _mlp(mn, wu, wd)
    out = _gate_final(h1, delta, wg, p.reshape(batch * seq, -1), ple_w[0].astype(BF16),
                      row(gate_norm_g), row(ple_norm_g), row(final_norm_g))
    return out.reshape(batch, seq, d)
```

```python
import functools
import math

import jax
import jax.numpy as jnp
import numpy as np
from jax import lax
from jax.experimental import pallas as pl
from jax.experimental.pallas import tpu as pltpu

HEAD_DIM = 128
N_HEADS_A = 8
N_KV_A = 2
N_HEADS_B = 8
N_KV_B = 2
GROUP = 4
GRID_W = 64
WINDOW = 128
N_BUCKETS = 32
MAX_DISTANCE = 128
ROPE_THETA = 10000.0
EPS = 1e-6
NEG_INF = -1e30
LOG2_E = math.log2(math.e)
Q_SCALE = HEAD_DIM ** -0.5 * LOG2_E

V7X_VMEM_BYTES = 64 * 1024 * 1024
VMEM_RESERVE_BYTES = 8 * 1024 * 1024
VMEM_LIMIT_BYTES = V7X_VMEM_BYTES - VMEM_RESERVE_BYTES
F32_SUBLANES = 8
BF16_SUBLANES = 16

SUB_ROWS = 256

BF16 = jnp.bfloat16
F32 = jnp.float32


def _params(*semantics):
    return pltpu.CompilerParams(dimension_semantics=semantics, vmem_limit_bytes=VMEM_LIMIT_BYTES)


def _rms(x, g):
    return x * lax.rsqrt(jnp.mean(x * x, axis=-1, keepdims=True) + EPS) * g


IN_TM = 512
IN_TN = 512
IN_W_SLOTS = 3

_PLAIN, _Q_A, _K_A, _V_A, _Q_B = range(5)
_HEAD_KINDS = ([_Q_A] * N_HEADS_A + [_K_A] * N_KV_A + [_V_A] * N_KV_A
               + [_Q_B] * N_HEADS_B + [_PLAIN] * (2 * N_KV_B))


def _rope(y, cos, sin_signed):
    lane = lax.broadcasted_iota(jnp.int32, y.shape, 1)
    partner = jnp.where((lane % 64) < 32, pltpu.roll(y, 96, 1), pltpu.roll(y, 32, 1))
    return y * cos + partner * sin_signed


def _in_proj_kernel(x_ref, g_ref, w_hbm, cos_ref, sin_ref, gq_ref, gk_ref, o_ref, kt_ref, v1_ref,
                    w_ref, wbuf_ref, wsem):
    heads_per_dot = IN_TN // HEAD_DIM
    n_col_tiles = w_ref.shape[1] // IN_TN

    def normed_rows(r):
        rows = pl.ds(r * SUB_ROWS, SUB_ROWS)
        return rows, _rms(x_ref[rows, :], g_ref[...]).astype(BF16), cos_ref[rows, :], sin_ref[rows, :]

    def column_tile(c, sub_tile):
        rows, u, cos, sin_signed = sub_tile
        acc = jnp.dot(u, w_ref[:, c * IN_TN:(c + 1) * IN_TN], preferred_element_type=F32)
        for hh in range(heads_per_dot):
            head = c * heads_per_dot + hh
            a = acc[:, hh * HEAD_DIM:(hh + 1) * HEAD_DIM]
            kind = _HEAD_KINDS[head]
            if kind == _Q_A:
                a = _rope(_rms(a, gq_ref[...]), cos, sin_signed) * Q_SCALE
            elif kind == _K_A:
                a = _rope(_rms(a, gk_ref[...]), cos, sin_signed)
                kv = head - N_HEADS_A
                kt_ref[kv * HEAD_DIM:(kv + 1) * HEAD_DIM, rows] = a.T.astype(BF16)
            elif kind == _V_A:
                kv = head - N_HEADS_A - N_KV_A
                v1_ref[rows, 2 * kv * HEAD_DIM:(2 * kv + 1) * HEAD_DIM] = a.astype(BF16)
                v1_ref[rows, (2 * kv + 1) * HEAD_DIM:(2 * kv + 2) * HEAD_DIM] = jnp.ones(a.shape, BF16)
            elif kind == _Q_B:
                a = a * Q_SCALE
            o_ref[rows, head * HEAD_DIM:(head + 1) * HEAD_DIM] = a.astype(BF16)

    @pl.when(pl.program_id(0) == 0)
    def _():
        def tile_copy(c):
            src = w_hbm.at[0, :, pl.ds(c * IN_TN, IN_TN)]
            return pltpu.make_async_copy(src, wbuf_ref.at[c % IN_W_SLOTS], wsem.at[c % IN_W_SLOTS])

        for c in range(IN_W_SLOTS - 1):
            tile_copy(c).start()
        sub_tiles = [normed_rows(r) for r in range(IN_TM // SUB_ROWS)]
        for c in range(n_col_tiles):
            tile_copy(c).wait()
            if c + IN_W_SLOTS - 1 < n_col_tiles:
                tile_copy(c + IN_W_SLOTS - 1).start()
            w_ref[:, c * IN_TN:(c + 1) * IN_TN] = wbuf_ref[c % IN_W_SLOTS].astype(BF16)
            for sub_tile in sub_tiles:
                column_tile(c, sub_tile)

    @pl.when(pl.program_id(0) > 0)
    def _():
        for r in range(IN_TM // SUB_ROWS):
            sub_tile = normed_rows(r)
            for c in range(n_col_tiles):
                column_tile(c, sub_tile)


def _in_proj(x2, g, w, cos, sin_signed, gq, gk, seq):
    m, d = x2.shape
    n = w.shape[2]
    assert w.shape[0] == 1 and n == len(_HEAD_KINDS) * HEAD_DIM
    pos_tiles = seq // IN_TM
    return pl.pallas_call(
        _in_proj_kernel,
        grid=(m // IN_TM,),
        in_specs=[
            pl.BlockSpec((IN_TM, d), lambda i: (i, 0)),
            pl.BlockSpec((1, d), lambda i: (0, 0)),
            pl.BlockSpec(memory_space=pl.ANY),
            pl.BlockSpec((IN_TM, HEAD_DIM), lambda i: (i % pos_tiles, 0)),
            pl.BlockSpec((IN_TM, HEAD_DIM), lambda i: (i % pos_tiles, 0)),
            pl.BlockSpec((1, HEAD_DIM), lambda i: (0, 0)),
            pl.BlockSpec((1, HEAD_DIM), lambda i: (0, 0)),
        ],
        out_specs=[
            pl.BlockSpec((IN_TM, n), lambda i: (i, 0)),
            pl.BlockSpec((N_KV_A * HEAD_DIM, IN_TM), lambda i: (0, i)),
            pl.BlockSpec((IN_TM, 2 * N_KV_A * HEAD_DIM), lambda i: (i, 0)),
        ],
        out_shape=[
            jax.ShapeDtypeStruct((m, n), BF16),
            jax.ShapeDtypeStruct((N_KV_A * HEAD_DIM, m), BF16),
            jax.ShapeDtypeStruct((m, 2 * N_KV_A * HEAD_DIM), BF16),
        ],
        scratch_shapes=[
            pltpu.VMEM((d, n), BF16),
            pltpu.VMEM((IN_W_SLOTS, d, IN_TN), F32),
            pltpu.SemaphoreType.DMA((IN_W_SLOTS,)),
        ],
        compiler_params=_params("arbitrary"),
        name="in_proj",
    )(x2, g, w, cos, sin_signed, gq, gk)


GA_ROWS = 256
GA_KEYS = 512


def _global_attn_kernel(*refs, n_cast):
    q_ref, kt_ref, v1_ref = refs[:3]
    w_f32 = refs[3:3 + n_cast]
    o_ref = refs[3 + n_cast]
    w_bf16 = refs[4 + n_cast:4 + 2 * n_cast]
    s_ref, m_ref = refs[4 + 2 * n_cast:6 + 2 * n_cast]
    in_bufs = refs[6 + 2 * n_cast:6 + 3 * n_cast]
    out_bufs = refs[6 + 3 * n_cast:6 + 4 * n_cast]
    in_sem, out_sem = refs[6 + 4 * n_cast:]

    seq = q_ref.shape[0]
    n_blocks = seq // GA_ROWS
    first_chunk = (pl.program_id(0) * pl.num_programs(1) + pl.program_id(1)) * n_blocks

    def in_copy(w, chunk, slot):
        rows = in_bufs[w].shape[1]
        src = w_f32[w].at[0, pl.ds(pl.multiple_of(chunk * rows, rows), rows)]
        return pltpu.make_async_copy(src, in_bufs[w].at[slot], in_sem.at[w, slot])

    def out_copy(w, chunk):
        rows = out_bufs[w].shape[0]
        dst = w_bf16[w].at[pl.ds(pl.multiple_of(chunk * rows, rows), rows)]
        return pltpu.make_async_copy(out_bufs[w], dst, out_sem.at[w])

    def cast_chunk(r, first, last):
        chunk = first_chunk + r
        slot = r % 2
        for w in range(n_cast):
            in_copy(w, chunk, slot).wait()
            if not last:
                in_copy(w, chunk + 1, 1 - slot).start()
            if not first:
                out_copy(w, chunk - 1).wait()
            out_bufs[w][...] = in_bufs[w][slot].astype(BF16)
            out_copy(w, chunk).start()

    for w in range(n_cast):
        in_copy(w, first_chunk, 0).start()

    def rows_of(r):
        return pl.ds(pl.multiple_of(r * GA_ROWS, GA_ROWS), GA_ROWS)

    def scores(r, g, slot):
        q = q_ref[rows_of(r), g * HEAD_DIM:(g + 1) * HEAD_DIM]
        part = None
        for c in range(seq // GA_KEYS):
            keys = slice(c * GA_KEYS, (c + 1) * GA_KEYS)
            s = jnp.dot(q, kt_ref[:, keys], preferred_element_type=F32)
            s_ref[slot, :, keys] = s
            for l in range(GA_KEYS // HEAD_DIM):
                piece = s[:, l * HEAD_DIM:(l + 1) * HEAD_DIM]
                part = piece if part is None else jnp.maximum(part, piece)
        m_ref[slot] = jnp.broadcast_to(jnp.max(part, axis=-1, keepdims=True), (GA_ROWS, HEAD_DIM))

    def apply(r, g, slot):
        m = m_ref[slot]
        o = jnp.zeros((GA_ROWS, 2 * HEAD_DIM), F32)
        for c in range(seq // GA_KEYS):
            keys = slice(c * GA_KEYS, (c + 1) * GA_KEYS)
            p = jnp.concatenate([jnp.exp2(s_ref[slot, :, c * GA_KEYS + l * HEAD_DIM:c * GA_KEYS + (l + 1) * HEAD_DIM] - m)
                                 for l in range(GA_KEYS // HEAD_DIM)], axis=1).astype(BF16)
            o = o + jnp.dot(p, v1_ref[keys, :], preferred_element_type=F32)
        o_ref[rows_of(r), g * HEAD_DIM:(g + 1) * HEAD_DIM] = (o[:, :HEAD_DIM] / o[:, HEAD_DIM:]).astype(BF16)

    def row_block(r, first, last):
        for g in range(GROUP):
            slot = g % 2
            if g + 1 < GROUP:
                scores(r, g + 1, 1 - slot)
            elif not last:
                scores(r + 1, 0, 1 - slot)
            apply(r, g, slot)
        cast_chunk(r, first, last)

    scores(0, 0, 0)
    row_block(0, first=True, last=False)

    def body(r, carry):
        row_block(r, first=False, last=False)
        return carry

    lax.fori_loop(1, n_blocks - 1, body, 0)
    row_block(n_blocks - 1, first=False, last=True)
    for w in range(n_cast):
        out_copy(w, first_chunk + n_blocks - 1).wait()


def _global_attn(proj, kt, v1, weights, batch, seq):
    assert GROUP % 2 == 0, "score slots alternate per head and must line up across row blocks"
    n_blocks = seq // GA_ROWS
    assert n_blocks >= 3 and n_blocks % 2 == 0
    n_chunks = batch * N_KV_A * n_blocks
    chunk_rows = []
    for w in weights:
        assert w.shape[0] == 1 and w.shape[1] % (BF16_SUBLANES * n_chunks) == 0, w.shape
        chunk_rows.append(w.shape[1] // n_chunks)
    n_cast = len(weights)
    any_spec = pl.BlockSpec(memory_space=pl.ANY)
    attn_spec = pl.BlockSpec((seq, GROUP * HEAD_DIM), lambda b, k: (b, k))
    outs = pl.pallas_call(
        functools.partial(_global_attn_kernel, n_cast=n_cast),
        grid=(batch, N_KV_A),
        in_specs=[
            attn_spec,
            pl.BlockSpec((HEAD_DIM, seq), lambda b, k: (k, b)),
            pl.BlockSpec((seq, 2 * HEAD_DIM), lambda b, k: (b, k)),
        ] + [any_spec] * n_cast,
        out_specs=[attn_spec] + [any_spec] * n_cast,
        out_shape=[jax.ShapeDtypeStruct((batch * seq, N_HEADS_A * HEAD_DIM), BF16)]
        + [jax.ShapeDtypeStruct(w.shape[1:], BF16) for w in weights],
        scratch_shapes=[
            pltpu.VMEM((2, GA_ROWS, seq), F32),
            pltpu.VMEM((2, GA_ROWS, HEAD_DIM), F32),
        ] + [pltpu.VMEM((2, rows, w.shape[2]), F32) for rows, w in zip(chunk_rows, weights)]
        + [pltpu.VMEM((rows, w.shape[2]), BF16) for rows, w in zip(chunk_rows, weights)]
        + [pltpu.SemaphoreType.DMA((n_cast, 2)), pltpu.SemaphoreType.DMA((n_cast,))],
        compiler_params=_params("arbitrary", "arbitrary"),
        name="global_attn",
    )(proj, kt, v1, *weights)
    return outs[0], outs[1:]


WA_ROWS = 256
WA_BAND = WA_ROWS + 2 * WINDOW


def _window_attn_kernel(sink_ref, q_ref, k_ref, v_ref, line_ref, o_ref,
                        kpad_ref, v1pad_ref, bias_ref, sink_ref_b, s_ref, m_ref):
    kvh = pl.program_id(1)
    seq = q_ref.shape[0]
    n_stages = seq // WA_ROWS

    zeros = jnp.zeros((WINDOW, 2 * HEAD_DIM), BF16)
    kpad_ref[:WINDOW, :] = zeros[:, :HEAD_DIM]
    kpad_ref[WINDOW + seq:, :] = zeros[:, :HEAD_DIM]
    kpad_ref[WINDOW:WINDOW + seq, :] = k_ref[...]
    v1pad_ref[:WINDOW, :] = zeros
    v1pad_ref[WINDOW + seq:, :] = zeros
    v1pad_ref[WINDOW:WINDOW + seq, :HEAD_DIM] = v_ref[...]
    v1pad_ref[WINDOW:WINDOW + seq, HEAD_DIM:] = jnp.ones(v_ref.shape, BF16)

    for g in range(GROUP):
        head = kvh * GROUP + g
        tile = jnp.broadcast_to(line_ref[g, :1, :], (WA_ROWS, WA_BAND))
        bias_ref[g * WA_ROWS:(g + 1) * WA_ROWS, :] = pltpu.roll(tile, 0, 1, stride=1, stride_axis=0)
        sink_ref_b[g * WA_ROWS:(g + 1) * WA_ROWS, :] = jnp.full((WA_ROWS, HEAD_DIM), sink_ref[head] * LOG2_E, F32)

    col = lax.broadcasted_iota(jnp.int32, (GROUP * WA_ROWS, WA_BAND), 1)

    def rows_of(n):
        return pl.ds(pl.multiple_of(n * WA_ROWS, WA_ROWS), WA_ROWS)

    def band_of(n):
        return pl.ds(pl.multiple_of(n * WA_ROWS, WA_ROWS), WA_BAND)

    def scores(n, slot):
        q = jnp.concatenate([q_ref[rows_of(n), g * HEAD_DIM:(g + 1) * HEAD_DIM] for g in range(GROUP)], axis=0)
        s = lax.dot_general(q, kpad_ref[band_of(n), :], (((1,), (1,)), ((), ())), preferred_element_type=F32)
        in_range = ((col >= WINDOW) | (n > 0)) & ((col < WA_BAND - WINDOW) | (n < n_stages - 1))
        s = jnp.where(in_range, s + bias_ref[...], NEG_INF)
        s_ref[slot] = s
        row_max = jnp.broadcast_to(jnp.max(s, axis=-1, keepdims=True), sink_ref_b.shape)
        m_ref[slot] = jnp.maximum(row_max, sink_ref_b[...])

    def apply(n, slot):
        m = m_ref[slot]
        p = jnp.concatenate([jnp.exp2(s_ref[slot, :, c * HEAD_DIM:(c + 1) * HEAD_DIM] - m)
                             for c in range(WA_BAND // HEAD_DIM)], axis=1).astype(BF16)
        o = jnp.dot(p, v1pad_ref[band_of(n), :], preferred_element_type=F32)
        o = o[:, :HEAD_DIM] / (o[:, HEAD_DIM:] + jnp.exp2(sink_ref_b[...] - m))
        for g in range(GROUP):
            o_ref[rows_of(n), g * HEAD_DIM:(g + 1) * HEAD_DIM] = o[g * WA_ROWS:(g + 1) * WA_ROWS].astype(BF16)

    scores(0, 0)

    def body(i, carry):
        for slot in range(2):
            n = 2 * i + slot
            scores(n + 1, 1 - slot)
            apply(n, slot)
        return carry

    lax.fori_loop(0, n_stages // 2 - 1, body, 0)
    scores(n_stages - 1, 1)
    apply(n_stages - 2, 0)
    apply(n_stages - 1, 1)


def _window_attn(proj, bias_lines, sink, batch, seq):
    assert (seq // WA_ROWS) % 2 == 0, "the stage loop is unrolled by two so score slots are static"
    q_col = (N_HEADS_A + 2 * N_KV_A) // GROUP
    k_col = N_HEADS_A + 2 * N_KV_A + N_HEADS_B
    v_col = k_col + N_KV_B
    grid_spec = pltpu.PrefetchScalarGridSpec(
        num_scalar_prefetch=1,
        grid=(batch, N_KV_B),
        in_specs=[
            pl.BlockSpec((seq, GROUP * HEAD_DIM), lambda b, k, s: (b, q_col + k)),
            pl.BlockSpec((seq, HEAD_DIM), lambda b, k, s: (b, k_col + k)),
            pl.BlockSpec((seq, HEAD_DIM), lambda b, k, s: (b, v_col + k)),
            pl.BlockSpec((GROUP, F32_SUBLANES, WA_BAND), lambda b, k, s: (k, 0, 0)),
        ],
        out_specs=pl.BlockSpec((seq, GROUP * HEAD_DIM), lambda b, k, s: (b, k)),
        scratch_shapes=[
            pltpu.VMEM((seq + 2 * WINDOW, HEAD_DIM), BF16),
            pltpu.VMEM((seq + 2 * WINDOW, 2 * HEAD_DIM), BF16),
            pltpu.VMEM((GROUP * WA_ROWS, WA_BAND), F32),
            pltpu.VMEM((GROUP * WA_ROWS, HEAD_DIM), F32),
            pltpu.VMEM((2, GROUP * WA_ROWS, WA_BAND), F32),
            pltpu.VMEM((2, GROUP * WA_ROWS, HEAD_DIM), F32),
        ],
    )
    return pl.pallas_call(
        _window_attn_kernel,
        grid_spec=grid_spec,
        out_shape=jax.ShapeDtypeStruct((batch * seq, N_HEADS_B * HEAD_DIM), BF16),
        compiler_params=_params("parallel", "parallel"),
        name="window_attn",
    )(sink, proj, proj, proj, bias_lines)


OUT_TM = 512


def _out_proj_kernel(oa_ref, ob_ref, w_ref, x_ref, g_ref, h_ref, m_ref):
    ka = oa_ref.shape[1]
    for r in range(OUT_TM // SUB_ROWS):
        rows = pl.ds(r * SUB_ROWS, SUB_ROWS)
        h = x_ref[rows, :]
        h = h + jnp.dot(oa_ref[rows, :], w_ref[:ka, :], preferred_element_type=F32)
        h = h + jnp.dot(ob_ref[rows, :], w_ref[ka:, :], preferred_element_type=F32)
        h_ref[rows, :] = h
        m_ref[rows, :] = _rms(h, g_ref[...]).astype(BF16)


def _out_proj(oa, ob, w, x2, g):
    m, d = x2.shape
    ka, kb = oa.shape[1], ob.shape[1]
    return pl.pallas_call(
        _out_proj_kernel,
        grid=(m // OUT_TM,),
        in_specs=[
            pl.BlockSpec((OUT_TM, ka), lambda i: (i, 0)),
            pl.BlockSpec((OUT_TM, kb), lambda i: (i, 0)),
            pl.BlockSpec((ka + kb, d), lambda i: (0, 0)),
            pl.BlockSpec((OUT_TM, d), lambda i: (i, 0)),
            pl.BlockSpec((1, d), lambda i: (0, 0)),
        ],
        out_specs=[pl.BlockSpec((OUT_TM, d), lambda i: (i, 0)), pl.BlockSpec((OUT_TM, d), lambda i: (i, 0))],
        out_shape=[jax.ShapeDtypeStruct((m, d), F32), jax.ShapeDtypeStruct((m, d), BF16)],
        compiler_params=_params("parallel"),
        name="out_proj",
    )(oa, ob, w, x2, g)


MLP_TM = 1024
MLP_TF = 1024


def _mlp_kernel(m_ref, wu_ref, wd_ref, d_ref):
    @pl.when(pl.program_id(1) == 0)
    def _():
        d_ref[...] = jnp.zeros(d_ref.shape, F32)

    a = jnp.dot(m_ref[...], wu_ref[...], preferred_element_type=F32)
    act = jnp.square(jnp.maximum(a, 0.0)).astype(BF16)
    d_ref[...] += jnp.dot(act, wd_ref[...], preferred_element_type=F32)


def _mlp(mn, wu, wd):
    m, d = mn.shape
    dff = wu.shape[1]
    return pl.pallas_call(
        _mlp_kernel,
        grid=(m // MLP_TM, dff // MLP_TF),
        in_specs=[
            pl.BlockSpec((MLP_TM, d), lambda i, f: (i, 0)),
            pl.BlockSpec((d, MLP_TF), lambda i, f: (0, f)),
            pl.BlockSpec((MLP_TF, d), lambda i, f: (f, 0)),
        ],
        out_specs=pl.BlockSpec((MLP_TM, d), lambda i, f: (i, 0)),
        out_shape=jax.ShapeDtypeStruct((m, d), F32),
        compiler_params=_params("parallel", "arbitrary"),
        name="mlp",
    )(mn, wu, wd)


GATE_TM = 512


def _gate_final_kernel(h_ref, d_ref, wg_ref, p_ref, wp_ref, gg_ref, gp_ref, gf_ref, o_ref):
    for r in range(GATE_TM // SUB_ROWS):
        rows = pl.ds(r * SUB_ROWS, SUB_ROWS)
        h = h_ref[rows, :] + d_ref[rows, :]
        gate = jax.nn.sigmoid(jnp.dot(_rms(h, gg_ref[...]).astype(BF16), wg_ref[...], preferred_element_type=F32))
        e = _rms(jnp.dot(p_ref[rows, :].astype(BF16), wp_ref[...], preferred_element_type=F32), gp_ref[...])
        o_ref[rows, :] = _rms(h + gate * e, gf_ref[...])


def _gate_final(h, delta, wg, p2, wp, gg, gp, gf):
    m, d = h.shape
    dp = p2.shape[1]
    tile = pl.BlockSpec((GATE_TM, d), lambda i: (i, 0))
    gain = pl.BlockSpec((1, d), lambda i: (0, 0))
    return pl.pallas_call(
        _gate_final_kernel,
        grid=(m // GATE_TM,),
        in_specs=[
            tile,
            tile,
            pl.BlockSpec((d, d), lambda i: (0, 0)),
            pl.BlockSpec((GATE_TM, dp), lambda i: (i, 0)),
            pl.BlockSpec((dp, d), lambda i: (0, 0)),
            gain, gain, gain,
        ],
        out_specs=tile,
        out_shape=jax.ShapeDtypeStruct((m, d), F32),
        compiler_params=_params("parallel"),
        name="gate_final",
    )(h, delta, wg, p2, wp, gg, gp, gf)


def _rope_tables(seq):
    rows = seq // GRID_W
    row = np.repeat(np.arange(rows, dtype=np.float32), GRID_W)
    col = np.tile(np.arange(GRID_W, dtype=np.float32), rows)
    half = HEAD_DIM // 2
    inv_freq = np.float32(ROPE_THETA) ** (-np.arange(0, half, 2, dtype=np.float32) / np.float32(half))
    ang_r = row[:, None] * inv_freq
    ang_c = col[:, None] * inv_freq
    cr, sr, cc, sc = np.cos(ang_r), np.sin(ang_r), np.cos(ang_c), np.sin(ang_c)
    cos = np.concatenate([cr, cr, cc, cc], axis=-1).astype(np.float32)
    sin_signed = np.concatenate([-sr, sr, -sc, sc], axis=-1).astype(np.float32)
    return jnp.asarray(cos), jnp.asarray(sin_signed)


def _t5_bucket(rel):
    nb = N_BUCKETS // 2
    ret = jnp.where(rel > 0, nb, 0)
    n = jnp.abs(rel)
    max_exact = nb // 2
    nf = jnp.maximum(n, 1).astype(F32)
    large = max_exact + (jnp.log(nf / max_exact) / math.log(MAX_DISTANCE / max_exact)
                         * (nb - max_exact)).astype(jnp.int32)
    large = jnp.minimum(large, nb - 1)
    return ret + jnp.where(n < max_exact, n, large)


def _window_bias_lines(rel_bias_table):
    rel = jnp.arange(WA_BAND, dtype=jnp.int32) - WINDOW
    line = rel_bias_table[_t5_bucket(rel)].astype(F32).T * LOG2_E
    line = jnp.where((jnp.abs(rel) <= WINDOW)[None, :], line, NEG_INF)
    return jnp.broadcast_to(line[:, None, :], (line.shape[0], F32_SUBLANES, WA_BAND))


def kernel(x, p, attn_norm_g, w_in, q_norm_g, k_norm_g, sink_logits, w_out, mlp_norm_g, w_up, w_down, ple_w,
           ple_norm_g, gate_norm_g, w_gate, rel_bias_table, final_norm_g):
    batch, seq, d = x.shape
    assert w_in.shape[0] == 1, "gate_final fuses the final RMSNorm, which is only valid for a single layer"
    row = lambda v: v.reshape(1, -1).astype(F32)
    cos, sin_signed = _rope_tables(seq)
    h = x.reshape(batch * seq, d)
    proj, kt, v1 = _in_proj(h, row(attn_norm_g), w_in, cos, sin_signed, row(q_norm_g), row(k_norm_g), seq)
    oa, (wu, wd, wo, wg) = _global_attn(proj, kt, v1, (w_up, w_down, w_out, w_gate), batch, seq)
    ob = _window_attn(proj, _window_bias_lines(rel_bias_table), sink_logits.reshape(-1).astype(F32), batch, seq)
    h1, mn = _out_proj(oa, ob, wo, h, row(mlp_norm_g))
    delta = _mlp(mn, wu, wd)
    out = _gate_final(h1, delta, wg, p.reshape(batch * seq, -1), ple_w[0].astype(BF16),
                      row(gate_norm_g), row(ple_norm_g), row(final_norm_g))
    return out.reshape(batch, seq, d)
```

```python
import functools
import math

import jax
import jax.numpy as jnp
import numpy as np
from jax import lax
from jax.experimental import pallas as pl
from jax.experimental.pallas import tpu as pltpu

HEAD_DIM = 128
N_HEADS_A = 8
N_KV_A = 2
N_HEADS_B = 8
N_KV_B = 2
GROUP = 4
GRID_W = 64
WINDOW = 128
N_BUCKETS = 32
MAX_DISTANCE = 128
ROPE_THETA = 10000.0
EPS = 1e-6
NEG_INF = -1e30
LOG2_E = math.log2(math.e)
Q_SCALE = HEAD_DIM ** -0.5 * LOG2_E

V7X_VMEM_BYTES = 64 * 1024 * 1024
VMEM_RESERVE_BYTES = 8 * 1024 * 1024
VMEM_LIMIT_BYTES = V7X_VMEM_BYTES - VMEM_RESERVE_BYTES
F32_SUBLANES = 8
BF16_SUBLANES = 16

SUB_ROWS = 256

BF16 = jnp.bfloat16
F32 = jnp.float32


def _params(*semantics):
    return pltpu.CompilerParams(dimension_semantics=semantics, vmem_limit_bytes=VMEM_LIMIT_BYTES)


def _rms(x, g):
    return x * lax.rsqrt(jnp.mean(x * x, axis=-1, keepdims=True) + EPS) * g


IN_TM = 512
IN_TN = 512
IN_W_SLOTS = 3

_PLAIN, _Q_A, _K_A, _Q_B = range(4)
_HEAD_KINDS = ([_Q_A] * N_HEADS_A + [_K_A] * N_KV_A + [_PLAIN] * N_KV_A
               + [_Q_B] * N_HEADS_B + [_PLAIN] * (2 * N_KV_B))


def _rope(y, cos, sin_signed):
    lane = lax.broadcasted_iota(jnp.int32, y.shape, 1)
    partner = jnp.where((lane % 64) < 32, pltpu.roll(y, 96, 1), pltpu.roll(y, 32, 1))
    return y * cos + partner * sin_signed


def _in_proj_kernel(x_ref, g_ref, w_hbm, cos_ref, sin_ref, gq_ref, gk_ref, o_ref, w_ref, wbuf_ref, wsem):
    heads_per_dot = IN_TN // HEAD_DIM
    n_col_tiles = w_ref.shape[1] // IN_TN

    def normed_rows(r):
        rows = pl.ds(r * SUB_ROWS, SUB_ROWS)
        return rows, _rms(x_ref[rows, :], g_ref[...]).astype(BF16), cos_ref[rows, :], sin_ref[rows, :]

    def column_tile(c, sub_tile):
        rows, u, cos, sin_signed = sub_tile
        acc = jnp.dot(u, w_ref[:, c * IN_TN:(c + 1) * IN_TN], preferred_element_type=F32)
        for hh in range(heads_per_dot):
            head = c * heads_per_dot + hh
            a = acc[:, hh * HEAD_DIM:(hh + 1) * HEAD_DIM]
            kind = _HEAD_KINDS[head]
            if kind == _Q_A:
                a = _rope(_rms(a, gq_ref[...]), cos, sin_signed) * Q_SCALE
            elif kind == _K_A:
                a = _rope(_rms(a, gk_ref[...]), cos, sin_signed)
            elif kind == _Q_B:
                a = a * Q_SCALE
            o_ref[rows, head * HEAD_DIM:(head + 1) * HEAD_DIM] = a.astype(BF16)

    @pl.when(pl.program_id(0) == 0)
    def _():
        def tile_copy(c):
            src = w_hbm.at[0, :, pl.ds(c * IN_TN, IN_TN)]
            return pltpu.make_async_copy(src, wbuf_ref.at[c % IN_W_SLOTS], wsem.at[c % IN_W_SLOTS])

        for c in range(IN_W_SLOTS - 1):
            tile_copy(c).start()
        sub_tiles = [normed_rows(r) for r in range(IN_TM // SUB_ROWS)]
        for c in range(n_col_tiles):
            tile_copy(c).wait()
            if c + IN_W_SLOTS - 1 < n_col_tiles:
                tile_copy(c + IN_W_SLOTS - 1).start()
            w_ref[:, c * IN_TN:(c + 1) * IN_TN] = wbuf_ref[c % IN_W_SLOTS].astype(BF16)
            for sub_tile in sub_tiles:
                column_tile(c, sub_tile)

    @pl.when(pl.program_id(0) > 0)
    def _():
        for r in range(IN_TM // SUB_ROWS):
            sub_tile = normed_rows(r)
            for c in range(n_col_tiles):
                column_tile(c, sub_tile)


def _in_proj(x2, g, w, cos, sin_signed, gq, gk, seq):
    m, d = x2.shape
    n = w.shape[2]
    assert w.shape[0] == 1 and n == len(_HEAD_KINDS) * HEAD_DIM
    pos_tiles = seq // IN_TM
    return pl.pallas_call(
        _in_proj_kernel,
        grid=(m // IN_TM,),
        in_specs=[
            pl.BlockSpec((IN_TM, d), lambda i: (i, 0)),
            pl.BlockSpec((1, d), lambda i: (0, 0)),
            pl.BlockSpec(memory_space=pl.ANY),
            pl.BlockSpec((IN_TM, HEAD_DIM), lambda i: (i % pos_tiles, 0)),
            pl.BlockSpec((IN_TM, HEAD_DIM), lambda i: (i % pos_tiles, 0)),
            pl.BlockSpec((1, HEAD_DIM), lambda i: (0, 0)),
            pl.BlockSpec((1, HEAD_DIM), lambda i: (0, 0)),
        ],
        out_specs=pl.BlockSpec((IN_TM, n), lambda i: (i, 0)),
        out_shape=jax.ShapeDtypeStruct((m, n), BF16),
        scratch_shapes=[
            pltpu.VMEM((d, n), BF16),
            pltpu.VMEM((IN_W_SLOTS, d, IN_TN), F32),
            pltpu.SemaphoreType.DMA((IN_W_SLOTS,)),
        ],
        compiler_params=_params("arbitrary"),
        name="in_proj",
    )(x2, g, w, cos, sin_signed, gq, gk)


GA_ROWS = 256
GA_KEYS = 512


def _global_attn_kernel(*refs, n_cast):
    q_ref, k_ref, v_ref = refs[:3]
    w_f32 = refs[3:3 + n_cast]
    o_ref = refs[3 + n_cast]
    w_bf16 = refs[4 + n_cast:4 + 2 * n_cast]
    v1_ref, kt_ref, s_ref, m_ref = refs[4 + 2 * n_cast:8 + 2 * n_cast]
    in_bufs = refs[8 + 2 * n_cast:8 + 3 * n_cast]
    out_bufs = refs[8 + 3 * n_cast:8 + 4 * n_cast]
    in_sem, out_sem = refs[8 + 4 * n_cast:]

    seq = q_ref.shape[0]
    n_blocks = seq // GA_ROWS
    first_chunk = (pl.program_id(0) * pl.num_programs(1) + pl.program_id(1)) * n_blocks

    def in_copy(w, chunk, slot):
        rows = in_bufs[w].shape[1]
        src = w_f32[w].at[0, pl.ds(pl.multiple_of(chunk * rows, rows), rows)]
        return pltpu.make_async_copy(src, in_bufs[w].at[slot], in_sem.at[w, slot])

    def out_copy(w, chunk):
        rows = out_bufs[w].shape[0]
        dst = w_bf16[w].at[pl.ds(pl.multiple_of(chunk * rows, rows), rows)]
        return pltpu.make_async_copy(out_bufs[w], dst, out_sem.at[w])

    def cast_chunk(r, first, last):
        chunk = first_chunk + r
        slot = r % 2
        for w in range(n_cast):
            in_copy(w, chunk, slot).wait()
            if not last:
                in_copy(w, chunk + 1, 1 - slot).start()
            if not first:
                out_copy(w, chunk - 1).wait()
            out_bufs[w][...] = in_bufs[w][slot].astype(BF16)
            out_copy(w, chunk).start()

    for w in range(n_cast):
        in_copy(w, first_chunk, 0).start()

    v1_ref[:, :HEAD_DIM] = v_ref[...]
    v1_ref[:, HEAD_DIM:] = jnp.ones(v_ref.shape, BF16)
    kt_ref[...] = k_ref[...].T

    def rows_of(r):
        return pl.ds(pl.multiple_of(r * GA_ROWS, GA_ROWS), GA_ROWS)

    def scores(r, g, slot):
        q = q_ref[rows_of(r), g * HEAD_DIM:(g + 1) * HEAD_DIM]
        part = None
        for c in range(seq // GA_KEYS):
            keys = slice(c * GA_KEYS, (c + 1) * GA_KEYS)
            s = jnp.dot(q, kt_ref[:, keys], preferred_element_type=F32)
            s_ref[slot, :, keys] = s
            for l in range(GA_KEYS // HEAD_DIM):
                piece = s[:, l * HEAD_DIM:(l + 1) * HEAD_DIM]
                part = piece if part is None else jnp.maximum(part, piece)
        m_ref[slot] = jnp.broadcast_to(jnp.max(part, axis=-1, keepdims=True), (GA_ROWS, HEAD_DIM))

    def apply(r, g, slot):
        m = m_ref[slot]
        o = jnp.zeros((GA_ROWS, 2 * HEAD_DIM), F32)
        for c in range(seq // GA_KEYS):
            keys = slice(c * GA_KEYS, (c + 1) * GA_KEYS)
            p = jnp.concatenate([jnp.exp2(s_ref[slot, :, c * GA_KEYS + l * HEAD_DIM:c * GA_KEYS + (l + 1) * HEAD_DIM] - m)
                                 for l in range(GA_KEYS // HEAD_DIM)], axis=1).astype(BF16)
            o = o + jnp.dot(p, v1_ref[keys, :], preferred_element_type=F32)
        o_ref[rows_of(r), g * HEAD_DIM:(g + 1) * HEAD_DIM] = (o[:, :HEAD_DIM] / o[:, HEAD_DIM:]).astype(BF16)

    def row_block(r, first, last):
        for g in range(GROUP):
            slot = g % 2
            if g + 1 < GROUP:
                scores(r, g + 1, 1 - slot)
            elif not last:
                scores(r + 1, 0, 1 - slot)
            apply(r, g, slot)
        cast_chunk(r, first, last)

    scores(0, 0, 0)
    row_block(0, first=True, last=False)

    def body(r, carry):
        row_block(r, first=False, last=False)
        return carry

    lax.fori_loop(1, n_blocks - 1, body, 0)
    row_block(n_blocks - 1, first=False, last=True)
    for w in range(n_cast):
        out_copy(w, first_chunk + n_blocks - 1).wait()


def _global_attn(proj, weights, batch, seq):
    assert GROUP % 2 == 0, "score slots alternate per head and must line up across row blocks"
    n_blocks = seq // GA_ROWS
    assert n_blocks >= 3 and n_blocks % 2 == 0
    n_chunks = batch * N_KV_A * n_blocks
    chunk_rows = []
    for w in weights:
        assert w.shape[0] == 1 and w.shape[1] % (BF16_SUBLANES * n_chunks) == 0, w.shape
        chunk_rows.append(w.shape[1] // n_chunks)
    n_cast = len(weights)
    k_col = N_HEADS_A
    v_col = N_HEADS_A + N_KV_A
    any_spec = pl.BlockSpec(memory_space=pl.ANY)
    attn_spec = pl.BlockSpec((seq, GROUP * HEAD_DIM), lambda b, k: (b, k))
    outs = pl.pallas_call(
        functools.partial(_global_attn_kernel, n_cast=n_cast),
        grid=(batch, N_KV_A),
        in_specs=[
            attn_spec,
            pl.BlockSpec((seq, HEAD_DIM), lambda b, k: (b, k_col + k)),
            pl.BlockSpec((seq, HEAD_DIM), lambda b, k: (b, v_col + k)),
        ] + [any_spec] * n_cast,
        out_specs=[attn_spec] + [any_spec] * n_cast,
        out_shape=[jax.ShapeDtypeStruct((batch * seq, N_HEADS_A * HEAD_DIM), BF16)]
        + [jax.ShapeDtypeStruct(w.shape[1:], BF16) for w in weights],
        scratch_shapes=[
            pltpu.VMEM((seq, 2 * HEAD_DIM), BF16),
            pltpu.VMEM((HEAD_DIM, seq), BF16),
            pltpu.VMEM((2, GA_ROWS, seq), F32),
            pltpu.VMEM((2, GA_ROWS, HEAD_DIM), F32),
        ] + [pltpu.VMEM((2, rows, w.shape[2]), F32) for rows, w in zip(chunk_rows, weights)]
        + [pltpu.VMEM((rows, w.shape[2]), BF16) for rows, w in zip(chunk_rows, weights)]
        + [pltpu.SemaphoreType.DMA((n_cast, 2)), pltpu.SemaphoreType.DMA((n_cast,))],
        compiler_params=_params("arbitrary", "arbitrary"),
        name="global_attn",
    )(proj, proj, proj, *weights)
    return outs[0], outs[1:]


WA_ROWS = 256
WA_BAND = WA_ROWS + 2 * WINDOW


def _window_attn_kernel(sink_ref, q_ref, k_ref, v_ref, line_ref, o_ref,
                        kpad_ref, v1pad_ref, bias_ref, sink_ref_b, s_ref, m_ref):
    kvh = pl.program_id(1)
    seq = q_ref.shape[0]
    n_stages = seq // WA_ROWS

    zeros = jnp.zeros((WINDOW, 2 * HEAD_DIM), BF16)
    kpad_ref[:WINDOW, :] = zeros[:, :HEAD_DIM]
    kpad_ref[WINDOW + seq:, :] = zeros[:, :HEAD_DIM]
    kpad_ref[WINDOW:WINDOW + seq, :] = k_ref[...]
    v1pad_ref[:WINDOW, :] = zeros
    v1pad_ref[WINDOW + seq:, :] = zeros
    v1pad_ref[WINDOW:WINDOW + seq, :HEAD_DIM] = v_ref[...]
    v1pad_ref[WINDOW:WINDOW + seq, HEAD_DIM:] = jnp.ones(v_ref.shape, BF16)

    for g in range(GROUP):
        head = kvh * GROUP + g
        tile = jnp.broadcast_to(line_ref[g, :1, :], (WA_ROWS, WA_BAND))
        bias_ref[g * WA_ROWS:(g + 1) * WA_ROWS, :] = pltpu.roll(tile, 0, 1, stride=1, stride_axis=0)
        sink_ref_b[g * WA_ROWS:(g + 1) * WA_ROWS, :] = jnp.full((WA_ROWS, HEAD_DIM), sink_ref[head] * LOG2_E, F32)

    col = lax.broadcasted_iota(jnp.int32, (GROUP * WA_ROWS, WA_BAND), 1)

    def rows_of(n):
        return pl.ds(pl.multiple_of(n * WA_ROWS, WA_ROWS), WA_ROWS)

    def band_of(n):
        return pl.ds(pl.multiple_of(n * WA_ROWS, WA_ROWS), WA_BAND)

    def scores(n, slot):
        q = jnp.concatenate([q_ref[rows_of(n), g * HEAD_DIM:(g + 1) * HEAD_DIM] for g in range(GROUP)], axis=0)
        s = lax.dot_general(q, kpad_ref[band_of(n), :], (((1,), (1,)), ((), ())), preferred_element_type=F32)
        in_range = ((col >= WINDOW) | (n > 0)) & ((col < WA_BAND - WINDOW) | (n < n_stages - 1))
        s = jnp.where(in_range, s + bias_ref[...], NEG_INF)
        s_ref[slot] = s
        row_max = jnp.broadcast_to(jnp.max(s, axis=-1, keepdims=True), sink_ref_b.shape)
        m_ref[slot] = jnp.maximum(row_max, sink_ref_b[...])

    def apply(n, slot):
        m = m_ref[slot]
        p = jnp.concatenate([jnp.exp2(s_ref[slot, :, c * HEAD_DIM:(c + 1) * HEAD_DIM] - m)
                             for c in range(WA_BAND // HEAD_DIM)], axis=1).astype(BF16)
        o = jnp.dot(p, v1pad_ref[band_of(n), :], preferred_element_type=F32)
        o = o[:, :HEAD_DIM] / (o[:, HEAD_DIM:] + jnp.exp2(sink_ref_b[...] - m))
        for g in range(GROUP):
            o_ref[rows_of(n), g * HEAD_DIM:(g + 1) * HEAD_DIM] = o[g * WA_ROWS:(g + 1) * WA_ROWS].astype(BF16)

    scores(0, 0)

    def body(i, carry):
        for slot in range(2):
            n = 2 * i + slot
            scores(n + 1, 1 - slot)
            apply(n, slot)
        return carry

    lax.fori_loop(0, n_stages // 2 - 1, body, 0)
    scores(n_stages - 1, 1)
    apply(n_stages - 2, 0)
    apply(n_stages - 1, 1)


def _window_attn(proj, bias_lines, sink, batch, seq):
    assert (seq // WA_ROWS) % 2 == 0, "the stage loop is unrolled by two so score slots are static"
    q_col = (N_HEADS_A + 2 * N_KV_A) // GROUP
    k_col = N_HEADS_A + 2 * N_KV_A + N_HEADS_B
    v_col = k_col + N_KV_B
    grid_spec = pltpu.PrefetchScalarGridSpec(
        num_scalar_prefetch=1,
        grid=(batch, N_KV_B),
        in_specs=[
            pl.BlockSpec((seq, GROUP * HEAD_DIM), lambda b, k, s: (b, q_col + k)),
            pl.BlockSpec((seq, HEAD_DIM), lambda b, k, s: (b, k_col + k)),
            pl.BlockSpec((seq, HEAD_DIM), lambda b, k, s: (b, v_col + k)),
            pl.BlockSpec((GROUP, F32_SUBLANES, WA_BAND), lambda b, k, s: (k, 0, 0)),
        ],
        out_specs=pl.BlockSpec((seq, GROUP * HEAD_DIM), lambda b, k, s: (b, k)),
        scratch_shapes=[
            pltpu.VMEM((seq + 2 * WINDOW, HEAD_DIM), BF16),
            pltpu.VMEM((seq + 2 * WINDOW, 2 * HEAD_DIM), BF16),
            pltpu.VMEM((GROUP * WA_ROWS, WA_BAND), F32),
            pltpu.VMEM((GROUP * WA_ROWS, HEAD_DIM), F32),
            pltpu.VMEM((2, GROUP * WA_ROWS, WA_BAND), F32),
            pltpu.VMEM((2, GROUP * WA_ROWS, HEAD_DIM), F32),
        ],
    )
    return pl.pallas_call(
        _window_attn_kernel,
        grid_spec=grid_spec,
        out_shape=jax.ShapeDtypeStruct((batch * seq, N_HEADS_B * HEAD_DIM), BF16),
        compiler_params=_params("parallel", "parallel"),
        name="window_attn",
    )(sink, proj, proj, proj, bias_lines)


OUT_TM = 512


def _out_proj_kernel(oa_ref, ob_ref, w_ref, x_ref, g_ref, h_ref, m_ref):
    ka = oa_ref.shape[1]
    for r in range(OUT_TM // SUB_ROWS):
        rows = pl.ds(r * SUB_ROWS, SUB_ROWS)
        h = x_ref[rows, :]
        h = h + jnp.dot(oa_ref[rows, :], w_ref[:ka, :], preferred_element_type=F32)
        h = h + jnp.dot(ob_ref[rows, :], w_ref[ka:, :], preferred_element_type=F32)
        h_ref[rows, :] = h
        m_ref[rows, :] = _rms(h, g_ref[...]).astype(BF16)


def _out_proj(oa, ob, w, x2, g):
    m, d = x2.shape
    ka, kb = oa.shape[1], ob.shape[1]
    return pl.pallas_call(
        _out_proj_kernel,
        grid=(m // OUT_TM,),
        in_specs=[
            pl.BlockSpec((OUT_TM, ka), lambda i: (i, 0)),
            pl.BlockSpec((OUT_TM, kb), lambda i: (i, 0)),
            pl.BlockSpec((ka + kb, d), lambda i: (0, 0)),
            pl.BlockSpec((OUT_TM, d), lambda i: (i, 0)),
            pl.BlockSpec((1, d), lambda i: (0, 0)),
        ],
        out_specs=[pl.BlockSpec((OUT_TM, d), lambda i: (i, 0)), pl.BlockSpec((OUT_TM, d), lambda i: (i, 0))],
        out_shape=[jax.ShapeDtypeStruct((m, d), F32), jax.ShapeDtypeStruct((m, d), BF16)],
        compiler_params=_params("parallel"),
        name="out_proj",
    )(oa, ob, w, x2, g)


MLP_TM = 1024
MLP_TF = 1024


def _mlp_kernel(m_ref, wu_ref, wd_ref, d_ref):
    @pl.when(pl.program_id(1) == 0)
    def _():
        d_ref[...] = jnp.zeros(d_ref.shape, F32)

    a = jnp.dot(m_ref[...], wu_ref[...], preferred_element_type=F32)
    act = jnp.square(jnp.maximum(a, 0.0)).astype(BF16)
    d_ref[...] += jnp.dot(act, wd_ref[...], preferred_element_type=F32)


def _mlp(mn, wu, wd):
    m, d = mn.shape
    dff = wu.shape[1]
    return pl.pallas_call(
        _mlp_kernel,
        grid=(m // MLP_TM, dff // MLP_TF),
        in_specs=[
            pl.BlockSpec((MLP_TM, d), lambda i, f: (i, 0)),
            pl.BlockSpec((d, MLP_TF), lambda i, f: (0, f)),
            pl.BlockSpec((MLP_TF, d), lambda i, f: (f, 0)),
        ],
        out_specs=pl.BlockSpec((MLP_TM, d), lambda i, f: (i, 0)),
        out_shape=jax.ShapeDtypeStruct((m, d), F32),
        compiler_params=_params("parallel", "arbitrary"),
        name="mlp",
    )(mn, wu, wd)


GATE_TM = 512
GATE_TN = 512


def _gate_final_kernel(h_ref, d_ref, wg_hbm, p_ref, wp_ref, gg_ref, gp_ref, gf_ref, o_ref, wg_ref, wsem):
    d_model = h_ref.shape[1]
    n_col_tiles = d_model // GATE_TN

    def prologue(r):
        rows = pl.ds(r * SUB_ROWS, SUB_ROWS)
        h = h_ref[rows, :] + d_ref[rows, :]
        e = _rms(jnp.dot(p_ref[rows, :].astype(BF16), wp_ref[...], preferred_element_type=F32), gp_ref[...])
        return rows, h, _rms(h, gg_ref[...]).astype(BF16), e

    @pl.when(pl.program_id(0) == 0)
    def _():
        def tile_copy(c):
            cols = pl.ds(c * GATE_TN, GATE_TN)
            return pltpu.make_async_copy(wg_hbm.at[:, cols], wg_ref.at[:, cols], wsem.at[c])

        for c in range(n_col_tiles):
            tile_copy(c).start()
        sub_tiles = [prologue(r) for r in range(GATE_TM // SUB_ROWS)]
        mixed = [[] for _ in sub_tiles]
        for c in range(n_col_tiles):
            tile_copy(c).wait()
            cols = slice(c * GATE_TN, (c + 1) * GATE_TN)
            for i, (rows, h, gn, e) in enumerate(sub_tiles):
                gate = jax.nn.sigmoid(jnp.dot(gn, wg_ref[:, cols], preferred_element_type=F32))
                mixed[i].append(h[:, cols] + gate * e[:, cols])
        for i, (rows, h, gn, e) in enumerate(sub_tiles):
            o_ref[rows, :] = _rms(jnp.concatenate(mixed[i], axis=1), gf_ref[...])

    @pl.when(pl.program_id(0) > 0)
    def _():
        for r in range(GATE_TM // SUB_ROWS):
            rows, h, gn, e = prologue(r)
            gate = jax.nn.sigmoid(jnp.dot(gn, wg_ref[...], preferred_element_type=F32))
            o_ref[rows, :] = _rms(h + gate * e, gf_ref[...])


def _gate_final(h, delta, wg, p2, wp, gg, gp, gf):
    m, d = h.shape
    dp = p2.shape[1]
    tile = pl.BlockSpec((GATE_TM, d), lambda i: (i, 0))
    gain = pl.BlockSpec((1, d), lambda i: (0, 0))
    return pl.pallas_call(
        _gate_final_kernel,
        grid=(m // GATE_TM,),
        in_specs=[
            tile,
            tile,
            pl.BlockSpec(memory_space=pl.ANY),
            pl.BlockSpec((GATE_TM, dp), lambda i: (i, 0)),
            pl.BlockSpec((dp, d), lambda i: (0, 0)),
            gain, gain, gain,
        ],
        out_specs=tile,
        out_shape=jax.ShapeDtypeStruct((m, d), F32),
        scratch_shapes=[pltpu.VMEM((d, d), BF16), pltpu.SemaphoreType.DMA((d // GATE_TN,))],
        compiler_params=_params("arbitrary"),
        name="gate_final",
    )(h, delta, wg, p2, wp, gg, gp, gf)


def _rope_tables(seq):
    rows = seq // GRID_W
    row = np.repeat(np.arange(rows, dtype=np.float32), GRID_W)
    col = np.tile(np.arange(GRID_W, dtype=np.float32), rows)
    half = HEAD_DIM // 2
    inv_freq = np.float32(ROPE_THETA) ** (-np.arange(0, half, 2, dtype=np.float32) / np.float32(half))
    ang_r = row[:, None] * inv_freq
    ang_c = col[:, None] * inv_freq
    cr, sr, cc, sc = np.cos(ang_r), np.sin(ang_r), np.cos(ang_c), np.sin(ang_c)
    cos = np.concatenate([cr, cr, cc, cc], axis=-1).astype(np.float32)
    sin_signed = np.concatenate([-sr, sr, -sc, sc], axis=-1).astype(np.float32)
    return jnp.asarray(cos), jnp.asarray(sin_signed)


def _t5_bucket(rel):
    nb = N_BUCKETS // 2
    ret = jnp.where(rel > 0, nb, 0)
    n = jnp.abs(rel)
    max_exact = nb // 2
    nf = jnp.maximum(n, 1).astype(F32)
    large = max_exact + (jnp.log(nf / max_exact) / math.log(MAX_DISTANCE / max_exact)
                         * (nb - max_exact)).astype(jnp.int32)
    large = jnp.minimum(large, nb - 1)
    return ret + jnp.where(n < max_exact, n, large)


def _window_bias_lines(rel_bias_table):
    rel = jnp.arange(WA_BAND, dtype=jnp.int32) - WINDOW
    line = rel_bias_table[_t5_bucket(rel)].astype(F32).T * LOG2_E
    line = jnp.where((jnp.abs(rel) <= WINDOW)[None, :], line, NEG_INF)
    return jnp.broadcast_to(line[:, None, :], (line.shape[0], F32_SUBLANES, WA_BAND))


def kernel(x, p, attn_norm_g, w_in, q_norm_g, k_norm_g, sink_logits, w_out, mlp_norm_g, w_up, w_down, ple_w,
           ple_norm_g, gate_norm_g, w_gate, rel_bias_table, final_norm_g):
    batch, seq, d = x.shape
    assert w_in.shape[0] == 1, "gate_final fuses the final RMSNorm, which is only valid for a single layer"
    row = lambda v: v.reshape(1, -1).astype(F32)
    cos, sin_signed = _rope_tables(seq)
    h = x.reshape(batch * seq, d)
    proj = _in_proj(h, row(attn_norm_g), w_in, cos, sin_signed, row(q_norm_g), row(k_norm_g), seq)
    oa, (wu, wd, wo, wg) = _global_attn(proj, (w_up, w_down, w_out, w_gate), batch, seq)
    ob = _window_attn(proj, _window_bias_lines(rel_bias_table), sink_logits.reshape(-1).astype(F32), batch, seq)
    h1, mn = _out_proj(oa, ob, wo, h, row(mlp_norm_g))
    delta = _mlp(mn, wu, wd)
    out = _gate_final(h1, delta, wg, p.reshape(batch * seq, -1), ple_w[0].astype(BF16),
                      row(gate_norm_g), row(ple_norm_g), row(final_norm_g))
    return out.reshape(batch, seq, d)
```

```python
import functools
import math

import jax
import jax.numpy as jnp
import numpy as np
from jax import lax
from jax.experimental import pallas as pl
from jax.experimental.pallas import tpu as pltpu

HEAD_DIM = 128
N_HEADS_A = 8
N_KV_A = 2
N_HEADS_B = 8
N_KV_B = 2
GROUP = 4
GRID_W = 64
WINDOW = 128
N_BUCKETS = 32
MAX_DISTANCE = 128
ROPE_THETA = 10000.0
EPS = 1e-6
NEG_INF = -1e30
LOG2_E = math.log2(math.e)
Q_SCALE = HEAD_DIM ** -0.5 * LOG2_E

V7X_VMEM_BYTES = 64 * 1024 * 1024
VMEM_RESERVE_BYTES = 8 * 1024 * 1024
VMEM_LIMIT_BYTES = V7X_VMEM_BYTES - VMEM_RESERVE_BYTES
F32_SUBLANES = 8
BF16_SUBLANES = 16

SUB_ROWS = 256

BF16 = jnp.bfloat16
F32 = jnp.float32


def _params(*semantics):
    return pltpu.CompilerParams(dimension_semantics=semantics, vmem_limit_bytes=VMEM_LIMIT_BYTES)


def _rms(x, g):
    return x * lax.rsqrt(jnp.mean(x * x, axis=-1, keepdims=True) + EPS) * g


IN_TM = 512
IN_TN = 512
IN_W_SLOTS = 3

_PLAIN, _Q_A, _K_A, _Q_B = range(4)
_HEAD_KINDS = ([_Q_A] * N_HEADS_A + [_K_A] * N_KV_A + [_PLAIN] * N_KV_A
               + [_Q_B] * N_HEADS_B + [_PLAIN] * (2 * N_KV_B))


def _rope(y, cos, sin_signed):
    lane = lax.broadcasted_iota(jnp.int32, y.shape, 1)
    partner = jnp.where((lane % 64) < 32, pltpu.roll(y, 96, 1), pltpu.roll(y, 32, 1))
    return y * cos + partner * sin_signed


def _in_proj_kernel(x_ref, g_ref, w_hbm, cos_ref, sin_ref, gq_ref, gk_ref, o_ref, w_ref, wbuf_ref, wsem):
    heads_per_dot = IN_TN // HEAD_DIM
    n_col_tiles = w_ref.shape[1] // IN_TN

    def normed_rows(r):
        rows = pl.ds(r * SUB_ROWS, SUB_ROWS)
        return rows, _rms(x_ref[rows, :], g_ref[...]).astype(BF16), cos_ref[rows, :], sin_ref[rows, :]

    def column_tile(c, sub_tile):
        rows, u, cos, sin_signed = sub_tile
        acc = jnp.dot(u, w_ref[:, c * IN_TN:(c + 1) * IN_TN], preferred_element_type=F32)
        for hh in range(heads_per_dot):
            head = c * heads_per_dot + hh
            a = acc[:, hh * HEAD_DIM:(hh + 1) * HEAD_DIM]
            kind = _HEAD_KINDS[head]
            if kind == _Q_A:
                a = _rope(_rms(a, gq_ref[...]), cos, sin_signed) * Q_SCALE
            elif kind == _K_A:
                a = _rope(_rms(a, gk_ref[...]), cos, sin_signed)
            elif kind == _Q_B:
                a = a * Q_SCALE
            o_ref[rows, head * HEAD_DIM:(head + 1) * HEAD_DIM] = a.astype(BF16)

    @pl.when(pl.program_id(0) == 0)
    def _():
        def tile_copy(c):
            src = w_hbm.at[0, :, pl.ds(c * IN_TN, IN_TN)]
            return pltpu.make_async_copy(src, wbuf_ref.at[c % IN_W_SLOTS], wsem.at[c % IN_W_SLOTS])

        for c in range(IN_W_SLOTS - 1):
            tile_copy(c).start()
        sub_tiles = [normed_rows(r) for r in range(IN_TM // SUB_ROWS)]
        for c in range(n_col_tiles):
            tile_copy(c).wait()
            if c + IN_W_SLOTS - 1 < n_col_tiles:
                tile_copy(c + IN_W_SLOTS - 1).start()
            w_ref[:, c * IN_TN:(c + 1) * IN_TN] = wbuf_ref[c % IN_W_SLOTS].astype(BF16)
            for sub_tile in sub_tiles:
                column_tile(c, sub_tile)

    @pl.when(pl.program_id(0) > 0)
    def _():
        for r in range(IN_TM // SUB_ROWS):
            sub_tile = normed_rows(r)
            for c in range(n_col_tiles):
                column_tile(c, sub_tile)


def _in_proj(x2, g, w, cos, sin_signed, gq, gk, seq):
    m, d = x2.shape
    n = w.shape[2]
    assert w.shape[0] == 1 and n == len(_HEAD_KINDS) * HEAD_DIM
    pos_tiles = seq // IN_TM
    return pl.pallas_call(
        _in_proj_kernel,
        grid=(m // IN_TM,),
        in_specs=[
            pl.BlockSpec((IN_TM, d), lambda i: (i, 0)),
            pl.BlockSpec((1, d), lambda i: (0, 0)),
            pl.BlockSpec(memory_space=pl.ANY),
            pl.BlockSpec((IN_TM, HEAD_DIM), lambda i: (i % pos_tiles, 0)),
            pl.BlockSpec((IN_TM, HEAD_DIM), lambda i: (i % pos_tiles, 0)),
            pl.BlockSpec((1, HEAD_DIM), lambda i: (0, 0)),
            pl.BlockSpec((1, HEAD_DIM), lambda i: (0, 0)),
        ],
        out_specs=pl.BlockSpec((IN_TM, n), lambda i: (i, 0)),
        out_shape=jax.ShapeDtypeStruct((m, n), BF16),
        scratch_shapes=[
            pltpu.VMEM((d, n), BF16),
            pltpu.VMEM((IN_W_SLOTS, d, IN_TN), F32),
            pltpu.SemaphoreType.DMA((IN_W_SLOTS,)),
        ],
        compiler_params=_params("arbitrary"),
        name="in_proj",
    )(x2, g, w, cos, sin_signed, gq, gk)


GA_ROWS = 256
GA_KEYS = 512


def _global_attn_kernel(*refs, n_cast):
    q_ref, k_ref, v_ref = refs[:3]
    w_f32 = refs[3:3 + n_cast]
    o_ref = refs[3 + n_cast]
    w_bf16 = refs[4 + n_cast:4 + 2 * n_cast]
    v1_ref, kt_ref, s_ref, m_ref = refs[4 + 2 * n_cast:8 + 2 * n_cast]
    in_bufs = refs[8 + 2 * n_cast:8 + 3 * n_cast]
    out_bufs = refs[8 + 3 * n_cast:8 + 4 * n_cast]
    in_sem, out_sem = refs[8 + 4 * n_cast:]

    seq = q_ref.shape[0]
    n_blocks = seq // GA_ROWS
    first_chunk = (pl.program_id(0) * pl.num_programs(1) + pl.program_id(1)) * n_blocks

    def in_copy(w, chunk, slot):
        rows = in_bufs[w].shape[1]
        src = w_f32[w].at[0, pl.ds(pl.multiple_of(chunk * rows, rows), rows)]
        return pltpu.make_async_copy(src, in_bufs[w].at[slot], in_sem.at[w, slot])

    def out_copy(w, chunk):
        rows = out_bufs[w].shape[0]
        dst = w_bf16[w].at[pl.ds(pl.multiple_of(chunk * rows, rows), rows)]
        return pltpu.make_async_copy(out_bufs[w], dst, out_sem.at[w])

    def cast_chunk(r, first, last):
        chunk = first_chunk + r
        slot = r % 2
        for w in range(n_cast):
            in_copy(w, chunk, slot).wait()
            if not last:
                in_copy(w, chunk + 1, 1 - slot).start()
            if not first:
                out_copy(w, chunk - 1).wait()
            out_bufs[w][...] = in_bufs[w][slot].astype(BF16)
            out_copy(w, chunk).start()

    for w in range(n_cast):
        in_copy(w, first_chunk, 0).start()

    v1_ref[:, :HEAD_DIM] = v_ref[...]
    v1_ref[:, HEAD_DIM:] = jnp.ones(v_ref.shape, BF16)
    kt_ref[...] = k_ref[...].T

    def rows_of(r):
        return pl.ds(pl.multiple_of(r * GA_ROWS, GA_ROWS), GA_ROWS)

    def scores(r, g, slot):
        q = q_ref[rows_of(r), g * HEAD_DIM:(g + 1) * HEAD_DIM]
        part = None
        for c in range(seq // GA_KEYS):
            keys = slice(c * GA_KEYS, (c + 1) * GA_KEYS)
            s = jnp.dot(q, kt_ref[:, keys], preferred_element_type=F32)
            s_ref[slot, :, keys] = s
            for l in range(GA_KEYS // HEAD_DIM):
                piece = s[:, l * HEAD_DIM:(l + 1) * HEAD_DIM]
                part = piece if part is None else jnp.maximum(part, piece)
        m_ref[slot] = jnp.broadcast_to(jnp.max(part, axis=-1, keepdims=True), (GA_ROWS, HEAD_DIM))

    def apply(r, g, slot):
        m = m_ref[slot]
        o = jnp.zeros((GA_ROWS, 2 * HEAD_DIM), F32)
        for c in range(seq // GA_KEYS):
            keys = slice(c * GA_KEYS, (c + 1) * GA_KEYS)
            p = jnp.concatenate([jnp.exp2(s_ref[slot, :, c * GA_KEYS + l * HEAD_DIM:c * GA_KEYS + (l + 1) * HEAD_DIM] - m)
                                 for l in range(GA_KEYS // HEAD_DIM)], axis=1).astype(BF16)
            o = o + jnp.dot(p, v1_ref[keys, :], preferred_element_type=F32)
        o_ref[rows_of(r), g * HEAD_DIM:(g + 1) * HEAD_DIM] = (o[:, :HEAD_DIM] / o[:, HEAD_DIM:]).astype(BF16)

    def row_block(r, first, last):
        for g in range(GROUP):
            slot = g % 2
            if g + 1 < GROUP:
                scores(r, g + 1, 1 - slot)
            elif not last:
                scores(r + 1, 0, 1 - slot)
            apply(r, g, slot)
        cast_chunk(r, first, last)

    scores(0, 0, 0)
    row_block(0, first=True, last=False)

    def body(r, carry):
        row_block(r, first=False, last=False)
        return carry

    lax.fori_loop(1, n_blocks - 1, body, 0)
    row_block(n_blocks - 1, first=False, last=True)
    for w in range(n_cast):
        out_copy(w, first_chunk + n_blocks - 1).wait()


def _global_attn(proj, weights, batch, seq):
    assert GROUP % 2 == 0, "score slots alternate per head and must line up across row blocks"
    n_blocks = seq // GA_ROWS
    assert n_blocks >= 3 and n_blocks % 2 == 0
    n_chunks = batch * N_KV_A * n_blocks
    chunk_rows = []
    for w in weights:
        assert w.shape[0] == 1 and w.shape[1] % (BF16_SUBLANES * n_chunks) == 0, w.shape
        chunk_rows.append(w.shape[1] // n_chunks)
    n_cast = len(weights)
    k_col = N_HEADS_A
    v_col = N_HEADS_A + N_KV_A
    any_spec = pl.BlockSpec(memory_space=pl.ANY)
    attn_spec = pl.BlockSpec((seq, GROUP * HEAD_DIM), lambda b, k: (b, k))
    outs = pl.pallas_call(
        functools.partial(_global_attn_kernel, n_cast=n_cast),
        grid=(batch, N_KV_A),
        in_specs=[
            attn_spec,
            pl.BlockSpec((seq, HEAD_DIM), lambda b, k: (b, k_col + k)),
            pl.BlockSpec((seq, HEAD_DIM), lambda b, k: (b, v_col + k)),
        ] + [any_spec] * n_cast,
        out_specs=[attn_spec] + [any_spec] * n_cast,
        out_shape=[jax.ShapeDtypeStruct((batch * seq, N_HEADS_A * HEAD_DIM), BF16)]
        + [jax.ShapeDtypeStruct(w.shape[1:], BF16) for w in weights],
        scratch_shapes=[
            pltpu.VMEM((seq, 2 * HEAD_DIM), BF16),
            pltpu.VMEM((HEAD_DIM, seq), BF16),
            pltpu.VMEM((2, GA_ROWS, seq), F32),
            pltpu.VMEM((2, GA_ROWS, HEAD_DIM), F32),
        ] + [pltpu.VMEM((2, rows, w.shape[2]), F32) for rows, w in zip(chunk_rows, weights)]
        + [pltpu.VMEM((rows, w.shape[2]), BF16) for rows, w in zip(chunk_rows, weights)]
        + [pltpu.SemaphoreType.DMA((n_cast, 2)), pltpu.SemaphoreType.DMA((n_cast,))],
        compiler_params=_params("arbitrary", "arbitrary"),
        name="global_attn",
    )(proj, proj, proj, *weights)
    return outs[0], outs[1:]


WA_ROWS = 256
WA_BAND = WA_ROWS + 2 * WINDOW
WA_UNROLL = 4


def _window_attn_kernel(sink_ref, q_ref, k_ref, v_ref, line_ref, o_ref,
                        kpad_ref, v1pad_ref, bias_ref, sink_ref_b, s_ref, m_ref):
    kvh = pl.program_id(1)
    seq = q_ref.shape[0]
    n_stages = seq // WA_ROWS

    zeros = jnp.zeros((WINDOW, 2 * HEAD_DIM), BF16)
    kpad_ref[:WINDOW, :] = zeros[:, :HEAD_DIM]
    kpad_ref[WINDOW + seq:, :] = zeros[:, :HEAD_DIM]
    kpad_ref[WINDOW:WINDOW + seq, :] = k_ref[...]
    v1pad_ref[:WINDOW, :] = zeros
    v1pad_ref[WINDOW + seq:, :] = zeros
    v1pad_ref[WINDOW:WINDOW + seq, :HEAD_DIM] = v_ref[...]
    v1pad_ref[WINDOW:WINDOW + seq, HEAD_DIM:] = jnp.ones(v_ref.shape, BF16)

    for g in range(GROUP):
        head = kvh * GROUP + g
        tile = jnp.broadcast_to(line_ref[g, :1, :], (WA_ROWS, WA_BAND))
        bias_ref[g * WA_ROWS:(g + 1) * WA_ROWS, :] = pltpu.roll(tile, 0, 1, stride=1, stride_axis=0)
        sink_ref_b[g * WA_ROWS:(g + 1) * WA_ROWS, :] = jnp.full((WA_ROWS, HEAD_DIM), sink_ref[head] * LOG2_E, F32)

    col = lax.broadcasted_iota(jnp.int32, (GROUP * WA_ROWS, WA_BAND), 1)

    def rows_of(n):
        return pl.ds(pl.multiple_of(n * WA_ROWS, WA_ROWS), WA_ROWS)

    def band_of(n):
        return pl.ds(pl.multiple_of(n * WA_ROWS, WA_ROWS), WA_BAND)

    def scores(n, slot):
        q = jnp.concatenate([q_ref[rows_of(n), g * HEAD_DIM:(g + 1) * HEAD_DIM] for g in range(GROUP)], axis=0)
        s = lax.dot_general(q, kpad_ref[band_of(n), :], (((1,), (1,)), ((), ())), preferred_element_type=F32)
        in_range = ((col >= WINDOW) | (n > 0)) & ((col < WA_BAND - WINDOW) | (n < n_stages - 1))
        s = jnp.where(in_range, s + bias_ref[...], NEG_INF)
        s_ref[slot] = s
        row_max = jnp.broadcast_to(jnp.max(s, axis=-1, keepdims=True), sink_ref_b.shape)
        m_ref[slot] = jnp.maximum(row_max, sink_ref_b[...])

    def apply(n, slot):
        m = m_ref[slot]
        p = jnp.concatenate([jnp.exp2(s_ref[slot, :, c * HEAD_DIM:(c + 1) * HEAD_DIM] - m)
                             for c in range(WA_BAND // HEAD_DIM)], axis=1).astype(BF16)
        o = jnp.dot(p, v1pad_ref[band_of(n), :], preferred_element_type=F32)
        o = o[:, :HEAD_DIM] / (o[:, HEAD_DIM:] + jnp.exp2(sink_ref_b[...] - m))
        for g in range(GROUP):
            o_ref[rows_of(n), g * HEAD_DIM:(g + 1) * HEAD_DIM] = o[g * WA_ROWS:(g + 1) * WA_ROWS].astype(BF16)

    scores(0, 0)

    def body(i, carry):
        for k in range(WA_UNROLL):
            n = WA_UNROLL * i + k
            scores(n + 1, (k + 1) % 2)
            apply(n, k % 2)
        return carry

    lax.fori_loop(0, n_stages // WA_UNROLL - 1, body, 0)
    for n in range(n_stages - WA_UNROLL, n_stages):
        if n + 1 < n_stages:
            scores(n + 1, (n + 1) % 2)
        apply(n, n % 2)


def _window_attn(proj, bias_lines, sink, batch, seq):
    assert WA_UNROLL % 2 == 0 and (seq // WA_ROWS) % WA_UNROLL == 0, "score slots must stay static in the stage loop"
    q_col = (N_HEADS_A + 2 * N_KV_A) // GROUP
    k_col = N_HEADS_A + 2 * N_KV_A + N_HEADS_B
    v_col = k_col + N_KV_B
    grid_spec = pltpu.PrefetchScalarGridSpec(
        num_scalar_prefetch=1,
        grid=(batch, N_KV_B),
        in_specs=[
            pl.BlockSpec((seq, GROUP * HEAD_DIM), lambda b, k, s: (b, q_col + k)),
            pl.BlockSpec((seq, HEAD_DIM), lambda b, k, s: (b, k_col + k)),
            pl.BlockSpec((seq, HEAD_DIM), lambda b, k, s: (b, v_col + k)),
            pl.BlockSpec((GROUP, F32_SUBLANES, WA_BAND), lambda b, k, s: (k, 0, 0)),
        ],
        out_specs=pl.BlockSpec((seq, GROUP * HEAD_DIM), lambda b, k, s: (b, k)),
        scratch_shapes=[
            pltpu.VMEM((seq + 2 * WINDOW, HEAD_DIM), BF16),
            pltpu.VMEM((seq + 2 * WINDOW, 2 * HEAD_DIM), BF16),
            pltpu.VMEM((GROUP * WA_ROWS, WA_BAND), F32),
            pltpu.VMEM((GROUP * WA_ROWS, HEAD_DIM), F32),
            pltpu.VMEM((2, GROUP * WA_ROWS, WA_BAND), F32),
            pltpu.VMEM((2, GROUP * WA_ROWS, HEAD_DIM), F32),
        ],
    )
    return pl.pallas_call(
        _window_attn_kernel,
        grid_spec=grid_spec,
        out_shape=jax.ShapeDtypeStruct((batch * seq, N_HEADS_B * HEAD_DIM), BF16),
        compiler_params=_params("parallel", "parallel"),
        name="window_attn",
    )(sink, proj, proj, proj, bias_lines)


OUT_TM = 512


def _out_proj_kernel(oa_ref, ob_ref, w_ref, x_ref, g_ref, h_ref, m_ref):
    ka = oa_ref.shape[1]
    for r in range(OUT_TM // SUB_ROWS):
        rows = pl.ds(r * SUB_ROWS, SUB_ROWS)
        h = x_ref[rows, :]
        h = h + jnp.dot(oa_ref[rows, :], w_ref[:ka, :], preferred_element_type=F32)
        h = h + jnp.dot(ob_ref[rows, :], w_ref[ka:, :], preferred_element_type=F32)
        h_ref[rows, :] = h
        m_ref[rows, :] = _rms(h, g_ref[...]).astype(BF16)


def _out_proj(oa, ob, w, x2, g):
    m, d = x2.shape
    ka, kb = oa.shape[1], ob.shape[1]
    return pl.pallas_call(
        _out_proj_kernel,
        grid=(m // OUT_TM,),
        in_specs=[
            pl.BlockSpec((OUT_TM, ka), lambda i: (i, 0)),
            pl.BlockSpec((OUT_TM, kb), lambda i: (i, 0)),
            pl.BlockSpec((ka + kb, d), lambda i: (0, 0)),
            pl.BlockSpec((OUT_TM, d), lambda i: (i, 0)),
            pl.BlockSpec((1, d), lambda i: (0, 0)),
        ],
        out_specs=[pl.BlockSpec((OUT_TM, d), lambda i: (i, 0)), pl.BlockSpec((OUT_TM, d), lambda i: (i, 0))],
        out_shape=[jax.ShapeDtypeStruct((m, d), F32), jax.ShapeDtypeStruct((m, d), BF16)],
        compiler_params=_params("parallel"),
        name="out_proj",
    )(oa, ob, w, x2, g)


MLP_TM = 1024
MLP_TF = 1024


def _mlp_kernel(m_ref, wu_ref, wd_ref, d_ref):
    @pl.when(pl.program_id(1) == 0)
    def _():
        d_ref[...] = jnp.zeros(d_ref.shape, F32)

    a = jnp.dot(m_ref[...], wu_ref[...], preferred_element_type=F32)
    act = jnp.square(jnp.maximum(a, 0.0)).astype(BF16)
    d_ref[...] += jnp.dot(act, wd_ref[...], preferred_element_type=F32)


def _mlp(mn, wu, wd):
    m, d = mn.shape
    dff = wu.shape[1]
    return pl.pallas_call(
        _mlp_kernel,
        grid=(m // MLP_TM, dff // MLP_TF),
        in_specs=[
            pl.BlockSpec((MLP_TM, d), lambda i, f: (i, 0)),
            pl.BlockSpec((d, MLP_TF), lambda i, f: (0, f)),
            pl.BlockSpec((MLP_TF, d), lambda i, f: (f, 0)),
        ],
        out_specs=pl.BlockSpec((MLP_TM, d), lambda i, f: (i, 0)),
        out_shape=jax.ShapeDtypeStruct((m, d), F32),
        compiler_params=_params("parallel", "arbitrary"),
        name="mlp",
    )(mn, wu, wd)


GATE_TM = 512


def _gate_final_kernel(h_ref, d_ref, wg_ref, p_ref, wp_ref, gg_ref, gp_ref, gf_ref, o_ref):
    for r in range(GATE_TM // SUB_ROWS):
        rows = pl.ds(r * SUB_ROWS, SUB_ROWS)
        h = h_ref[rows, :] + d_ref[rows, :]
        gate = jax.nn.sigmoid(jnp.dot(_rms(h, gg_ref[...]).astype(BF16), wg_ref[...], preferred_element_type=F32))
        e = _rms(jnp.dot(p_ref[rows, :].astype(BF16), wp_ref[...], preferred_element_type=F32), gp_ref[...])
        o_ref[rows, :] = _rms(h + gate * e, gf_ref[...])


def _gate_final(h, delta, wg, p2, wp, gg, gp, gf):
    m, d = h.shape
    dp = p2.shape[1]
    tile = pl.BlockSpec((GATE_TM, d), lambda i: (i, 0))
    gain = pl.BlockSpec((1, d), lambda i: (0, 0))
    return pl.pallas_call(
        _gate_final_kernel,
        grid=(m // GATE_TM,),
        in_specs=[
            tile,
            tile,
            pl.BlockSpec((d, d), lambda i: (0, 0)),
            pl.BlockSpec((GATE_TM, dp), lambda i: (i, 0)),
            pl.BlockSpec((dp, d), lambda i: (0, 0)),
            gain, gain, gain,
        ],
        out_specs=tile,
        out_shape=jax.ShapeDtypeStruct((m, d), F32),
        compiler_params=_params("parallel"),
        name="gate_final",
    )(h, delta, wg, p2, wp, gg, gp, gf)


def _rope_tables(seq):
    rows = seq // GRID_W
    row = np.repeat(np.arange(rows, dtype=np.float32), GRID_W)
    col = np.tile(np.arange(GRID_W, dtype=np.float32), rows)
    half = HEAD_DIM // 2
    inv_freq = np.float32(ROPE_THETA) ** (-np.arange(0, half, 2, dtype=np.float32) / np.float32(half))
    ang_r = row[:, None] * inv_freq
    ang_c = col[:, None] * inv_freq
    cr, sr, cc, sc = np.cos(ang_r), np.sin(ang_r), np.cos(ang_c), np.sin(ang_c)
    cos = np.concatenate([cr, cr, cc, cc], axis=-1).astype(np.float32)
    sin_signed = np.concatenate([-sr, sr, -sc, sc], axis=-1).astype(np.float32)
    return jnp.asarray(cos), jnp.asarray(sin_signed)


def _t5_bucket(rel):
    nb = N_BUCKETS // 2
    ret = jnp.where(rel > 0, nb, 0)
    n = jnp.abs(rel)
    max_exact = nb // 2
    nf = jnp.maximum(n, 1).astype(F32)
    large = max_exact + (jnp.log(nf / max_exact) / math.log(MAX_DISTANCE / max_exact)
                         * (nb - max_exact)).astype(jnp.int32)
    large = jnp.minimum(large, nb - 1)
    return ret + jnp.where(n < max_exact, n, large)


def _window_bias_lines(rel_bias_table):
    rel = jnp.arange(WA_BAND, dtype=jnp.int32) - WINDOW
    line = rel_bias_table[_t5_bucket(rel)].astype(F32).T * LOG2_E
    line = jnp.where((jnp.abs(rel) <= WINDOW)[None, :], line, NEG_INF)
    return jnp.broadcast_to(line[:, None, :], (line.shape[0], F32_SUBLANES, WA_BAND))


def kernel(x, p, attn_norm_g, w_in, q_norm_g, k_norm_g, sink_logits, w_out, mlp_norm_g, w_up, w_down, ple_w,
           ple_norm_g, gate_norm_g, w_gate, rel_bias_table, final_norm_g):
    batch, seq, d = x.shape
    assert w_in.shape[0] == 1, "gate_final fuses the final RMSNorm, which is only valid for a single layer"
    row = lambda v: v.reshape(1, -1).astype(F32)
    cos, sin_signed = _rope_tables(seq)
    h = x.reshape(batch * seq, d)
    proj = _in_proj(h, row(attn_norm_g), w_in, cos, sin_signed, row(q_norm_g), row(k_norm_g), seq)
    oa, (wu, wd, wo, wg) = _global_attn(proj, (w_up, w_down, w_out, w_gate), batch, seq)
    ob = _window_attn(proj, _window_bias_lines(rel_bias_table), sink_logits.reshape(-1).astype(F32), batch, seq)
    h1, mn = _out_proj(oa, ob, wo, h, row(mlp_norm_g))
    delta = _mlp(mn, wu, wd)
    out = _gate_final(h1, delta, wg, p.reshape(batch * seq, -1), ple_w[0].astype(BF16),
                      row(gate_norm_g), row(ple_norm_g), row(final_norm_g))
    return out.reshape(batch, seq, d)
```

```python
import functools
import math

import jax
import jax.numpy as jnp
import numpy as np
from jax import lax
from jax.experimental import pallas as pl
from jax.experimental.pallas import tpu as pltpu

HEAD_DIM = 128
N_HEADS_A = 8
N_KV_A = 2
N_HEADS_B = 8
N_KV_B = 2
GROUP = 4
GRID_W = 64
WINDOW = 128
N_BUCKETS = 32
MAX_DISTANCE = 128
ROPE_THETA = 10000.0
EPS = 1e-6
NEG_INF = -1e30
LOG2_E = math.log2(math.e)
Q_SCALE = HEAD_DIM ** -0.5 * LOG2_E

V7X_VMEM_BYTES = 64 * 1024 * 1024
VMEM_RESERVE_BYTES = 8 * 1024 * 1024
VMEM_LIMIT_BYTES = V7X_VMEM_BYTES - VMEM_RESERVE_BYTES
F32_SUBLANES = 8
BF16_SUBLANES = 16

SUB_ROWS = 256

BF16 = jnp.bfloat16
F32 = jnp.float32


def _params(*semantics):
    return pltpu.CompilerParams(dimension_semantics=semantics, vmem_limit_bytes=VMEM_LIMIT_BYTES)


def _rms(x, g):
    return x * lax.rsqrt(jnp.mean(x * x, axis=-1, keepdims=True) + EPS) * g


IN_TM = 512
IN_TN = 512
IN_W_SLOTS = 3

_PLAIN, _Q_A, _K_A, _Q_B = range(4)
_HEAD_KINDS = ([_Q_A] * N_HEADS_A + [_K_A] * N_KV_A + [_PLAIN] * N_KV_A
               + [_Q_B] * N_HEADS_B + [_PLAIN] * (2 * N_KV_B))


def _rope(y, cos, sin_signed):
    lane = lax.broadcasted_iota(jnp.int32, y.shape, 1)
    partner = jnp.where((lane % 64) < 32, pltpu.roll(y, 96, 1), pltpu.roll(y, 32, 1))
    return y * cos + partner * sin_signed


def _in_proj_kernel(x_ref, g_ref, w_hbm, cos_ref, sin_ref, gq_ref, gk_ref, o_ref, w_ref, wbuf_ref, wsem):
    heads_per_dot = IN_TN // HEAD_DIM
    n_col_tiles = w_ref.shape[1] // IN_TN

    def normed_rows(r):
        rows = pl.ds(r * SUB_ROWS, SUB_ROWS)
        return rows, _rms(x_ref[rows, :], g_ref[...]).astype(BF16), cos_ref[rows, :], sin_ref[rows, :]

    def column_tile(c, sub_tile):
        rows, u, cos, sin_signed = sub_tile
        acc = jnp.dot(u, w_ref[:, c * IN_TN:(c + 1) * IN_TN], preferred_element_type=F32)
        for hh in range(heads_per_dot):
            head = c * heads_per_dot + hh
            a = acc[:, hh * HEAD_DIM:(hh + 1) * HEAD_DIM]
            kind = _HEAD_KINDS[head]
            if kind == _Q_A:
                a = _rope(_rms(a, gq_ref[...]), cos, sin_signed) * Q_SCALE
            elif kind == _K_A:
                a = _rope(_rms(a, gk_ref[...]), cos, sin_signed)
            elif kind == _Q_B:
                a = a * Q_SCALE
            o_ref[rows, head * HEAD_DIM:(head + 1) * HEAD_DIM] = a.astype(BF16)

    @pl.when(pl.program_id(0) == 0)
    def _():
        def tile_copy(c):
            src = w_hbm.at[0, :, pl.ds(c * IN_TN, IN_TN)]
            return pltpu.make_async_copy(src, wbuf_ref.at[c % IN_W_SLOTS], wsem.at[c % IN_W_SLOTS])

        for c in range(IN_W_SLOTS - 1):
            tile_copy(c).start()
        sub_tiles = [normed_rows(r) for r in range(IN_TM // SUB_ROWS)]
        for c in range(n_col_tiles):
            tile_copy(c).wait()
            if c + IN_W_SLOTS - 1 < n_col_tiles:
                tile_copy(c + IN_W_SLOTS - 1).start()
            w_ref[:, c * IN_TN:(c + 1) * IN_TN] = wbuf_ref[c % IN_W_SLOTS].astype(BF16)
            for sub_tile in sub_tiles:
                column_tile(c, sub_tile)

    @pl.when(pl.program_id(0) > 0)
    def _():
        for r in range(IN_TM // SUB_ROWS):
            sub_tile = normed_rows(r)
            for c in range(n_col_tiles):
                column_tile(c, sub_tile)


def _in_proj(x2, g, w, cos, sin_signed, gq, gk, seq):
    m, d = x2.shape
    n = w.shape[2]
    assert w.shape[0] == 1 and n == len(_HEAD_KINDS) * HEAD_DIM
    pos_tiles = seq // IN_TM
    return pl.pallas_call(
        _in_proj_kernel,
        grid=(m // IN_TM,),
        in_specs=[
            pl.BlockSpec((IN_TM, d), lambda i: (i, 0)),
            pl.BlockSpec((1, d), lambda i: (0, 0)),
            pl.BlockSpec(memory_space=pl.ANY),
            pl.BlockSpec((IN_TM, HEAD_DIM), lambda i: (i % pos_tiles, 0)),
            pl.BlockSpec((IN_TM, HEAD_DIM), lambda i: (i % pos_tiles, 0)),
            pl.BlockSpec((1, HEAD_DIM), lambda i: (0, 0)),
            pl.BlockSpec((1, HEAD_DIM), lambda i: (0, 0)),
        ],
        out_specs=pl.BlockSpec((IN_TM, n), lambda i: (i, 0)),
        out_shape=jax.ShapeDtypeStruct((m, n), BF16),
        scratch_shapes=[
            pltpu.VMEM((d, n), BF16),
            pltpu.VMEM((IN_W_SLOTS, d, IN_TN), F32),
            pltpu.SemaphoreType.DMA((IN_W_SLOTS,)),
        ],
        compiler_params=_params("arbitrary"),
        name="in_proj",
    )(x2, g, w, cos, sin_signed, gq, gk)


GA_ROWS = 256
GA_KEYS = 512


def _global_attn_kernel(*refs, n_cast):
    q_ref, k_ref, v_ref = refs[:3]
    w_f32 = refs[3:3 + n_cast]
    o_ref = refs[3 + n_cast]
    w_bf16 = refs[4 + n_cast:4 + 2 * n_cast]
    v1_ref, kt_ref, s_ref, m_ref = refs[4 + 2 * n_cast:8 + 2 * n_cast]
    in_bufs = refs[8 + 2 * n_cast:8 + 3 * n_cast]
    out_bufs = refs[8 + 3 * n_cast:8 + 4 * n_cast]
    in_sem, out_sem = refs[8 + 4 * n_cast:]

    seq = q_ref.shape[0]
    n_blocks = seq // GA_ROWS
    first_chunk = (pl.program_id(0) * pl.num_programs(1) + pl.program_id(1)) * n_blocks

    def in_copy(w, chunk, slot):
        rows = in_bufs[w].shape[1]
        src = w_f32[w].at[0, pl.ds(pl.multiple_of(chunk * rows, rows), rows)]
        return pltpu.make_async_copy(src, in_bufs[w].at[slot], in_sem.at[w, slot])

    def out_copy(w, chunk):
        rows = out_bufs[w].shape[0]
        dst = w_bf16[w].at[pl.ds(pl.multiple_of(chunk * rows, rows), rows)]
        return pltpu.make_async_copy(out_bufs[w], dst, out_sem.at[w])

    def cast_chunk(r, first, last):
        chunk = first_chunk + r
        slot = r % 2
        for w in range(n_cast):
            in_copy(w, chunk, slot).wait()
            if not last:
                in_copy(w, chunk + 1, 1 - slot).start()
            if not first:
                out_copy(w, chunk - 1).wait()
            out_bufs[w][...] = in_bufs[w][slot].astype(BF16)
            out_copy(w, chunk).start()

    for w in range(n_cast):
        in_copy(w, first_chunk, 0).start()

    v1_ref[:, :HEAD_DIM] = v_ref[...]
    v1_ref[:, HEAD_DIM:] = jnp.ones(v_ref.shape, BF16)
    kt_ref[...] = k_ref[...].T

    def rows_of(r):
        return pl.ds(pl.multiple_of(r * GA_ROWS, GA_ROWS), GA_ROWS)

    def scores(r, g, slot):
        q = q_ref[rows_of(r), g * HEAD_DIM:(g + 1) * HEAD_DIM]
        part = None
        for c in range(seq // GA_KEYS):
            keys = slice(c * GA_KEYS, (c + 1) * GA_KEYS)
            s = jnp.dot(q, kt_ref[:, keys], preferred_element_type=F32)
            s_ref[slot, :, keys] = s
            for l in range(GA_KEYS // HEAD_DIM):
                piece = s[:, l * HEAD_DIM:(l + 1) * HEAD_DIM]
                part = piece if part is None else jnp.maximum(part, piece)
        m_ref[slot] = jnp.broadcast_to(jnp.max(part, axis=-1, keepdims=True), (GA_ROWS, HEAD_DIM))

    def apply(r, g, slot):
        m = m_ref[slot]
        o = jnp.zeros((GA_ROWS, 2 * HEAD_DIM), F32)
        for c in range(seq // GA_KEYS):
            keys = slice(c * GA_KEYS, (c + 1) * GA_KEYS)
            p = jnp.concatenate([jnp.exp2(s_ref[slot, :, c * GA_KEYS + l * HEAD_DIM:c * GA_KEYS + (l + 1) * HEAD_DIM] - m)
                                 for l in range(GA_KEYS // HEAD_DIM)], axis=1).astype(BF16)
            o = o + jnp.dot(p, v1_ref[keys, :], preferred_element_type=F32)
        o_ref[rows_of(r), g * HEAD_DIM:(g + 1) * HEAD_DIM] = (o[:, :HEAD_DIM] / o[:, HEAD_DIM:]).astype(BF16)

    def row_block(r, first, last):
        for g in range(GROUP):
            slot = g % 2
            if g + 1 < GROUP:
                scores(r, g + 1, 1 - slot)
            elif not last:
                scores(r + 1, 0, 1 - slot)
            apply(r, g, slot)
        cast_chunk(r, first, last)

    scores(0, 0, 0)
    row_block(0, first=True, last=False)

    def body(r, carry):
        row_block(r, first=False, last=False)
        return carry

    lax.fori_loop(1, n_blocks - 1, body, 0)
    row_block(n_blocks - 1, first=False, last=True)
    for w in range(n_cast):
        out_copy(w, first_chunk + n_blocks - 1).wait()


def _global_attn(proj, weights, batch, seq):
    assert GROUP % 2 == 0, "score slots alternate per head and must line up across row blocks"
    n_blocks = seq // GA_ROWS
    assert n_blocks >= 3 and n_blocks % 2 == 0
    n_chunks = batch * N_KV_A * n_blocks
    chunk_rows = []
    for w in weights:
        assert w.shape[0] == 1 and w.shape[1] % (BF16_SUBLANES * n_chunks) == 0, w.shape
        chunk_rows.append(w.shape[1] // n_chunks)
    n_cast = len(weights)
    k_col = N_HEADS_A
    v_col = N_HEADS_A + N_KV_A
    any_spec = pl.BlockSpec(memory_space=pl.ANY)
    attn_spec = pl.BlockSpec((seq, GROUP * HEAD_DIM), lambda b, k: (b, k))
    outs = pl.pallas_call(
        functools.partial(_global_attn_kernel, n_cast=n_cast),
        grid=(batch, N_KV_A),
        in_specs=[
            attn_spec,
            pl.BlockSpec((seq, HEAD_DIM), lambda b, k: (b, k_col + k)),
            pl.BlockSpec((seq, HEAD_DIM), lambda b, k: (b, v_col + k)),
        ] + [any_spec] * n_cast,
        out_specs=[attn_spec] + [any_spec] * n_cast,
        out_shape=[jax.ShapeDtypeStruct((batch * seq, N_HEADS_A * HEAD_DIM), BF16)]
        + [jax.ShapeDtypeStruct(w.shape[1:], BF16) for w in weights],
        scratch_shapes=[
            pltpu.VMEM((seq, 2 * HEAD_DIM), BF16),
            pltpu.VMEM((HEAD_DIM, seq), BF16),
            pltpu.VMEM((2, GA_ROWS, seq), F32),
            pltpu.VMEM((2, GA_ROWS, HEAD_DIM), F32),
        ] + [pltpu.VMEM((2, rows, w.shape[2]), F32) for rows, w in zip(chunk_rows, weights)]
        + [pltpu.VMEM((rows, w.shape[2]), BF16) for rows, w in zip(chunk_rows, weights)]
        + [pltpu.SemaphoreType.DMA((n_cast, 2)), pltpu.SemaphoreType.DMA((n_cast,))],
        compiler_params=_params("arbitrary", "arbitrary"),
        name="global_attn",
    )(proj, proj, proj, *weights)
    return outs[0], outs[1:]


WA_ROWS = 256
WA_BAND = WA_ROWS + 2 * WINDOW
WA_UNROLL = 4


def _window_attn_kernel(sink_ref, q_ref, k_ref, v_ref, line_ref, o_ref,
                        kpad_ref, v1pad_ref, bias_ref, sink_ref_b, s_ref, m_ref):
    kvh = pl.program_id(1)
    seq = q_ref.shape[0]
    n_stages = seq // WA_ROWS

    zeros = jnp.zeros((WINDOW, 2 * HEAD_DIM), BF16)
    kpad_ref[:WINDOW, :] = zeros[:, :HEAD_DIM]
    kpad_ref[WINDOW + seq:, :] = zeros[:, :HEAD_DIM]
    kpad_ref[WINDOW:WINDOW + seq, :] = k_ref[...]
    v1pad_ref[:WINDOW, :] = zeros
    v1pad_ref[WINDOW + seq:, :] = zeros
    v1pad_ref[WINDOW:WINDOW + seq, :HEAD_DIM] = v_ref[...]
    v1pad_ref[WINDOW:WINDOW + seq, HEAD_DIM:] = jnp.ones(v_ref.shape, BF16)

    for g in range(GROUP):
        head = kvh * GROUP + g
        tile = jnp.broadcast_to(line_ref[g, :1, :], (WA_ROWS, WA_BAND))
        bias_ref[g * WA_ROWS:(g + 1) * WA_ROWS, :] = pltpu.roll(tile, 0, 1, stride=1, stride_axis=0)
        sink_ref_b[g * WA_ROWS:(g + 1) * WA_ROWS, :] = jnp.full((WA_ROWS, HEAD_DIM), sink_ref[head] * LOG2_E, F32)

    col = lax.broadcasted_iota(jnp.int32, (GROUP * WA_ROWS, WA_BAND), 1)

    def rows_of(n):
        return pl.ds(pl.multiple_of(n * WA_ROWS, WA_ROWS), WA_ROWS)

    def band_of(n):
        return pl.ds(pl.multiple_of(n * WA_ROWS, WA_ROWS), WA_BAND)

    def scores(n, slot):
        q = jnp.concatenate([q_ref[rows_of(n), g * HEAD_DIM:(g + 1) * HEAD_DIM] for g in range(GROUP)], axis=0)
        s = lax.dot_general(q, kpad_ref[band_of(n), :], (((1,), (1,)), ((), ())), preferred_element_type=F32)
        in_range = ((col >= WINDOW) | (n > 0)) & ((col < WA_BAND - WINDOW) | (n < n_stages - 1))
        s = jnp.where(in_range, s + bias_ref[...], NEG_INF)
        s_ref[slot] = s
        row_max = jnp.broadcast_to(jnp.max(s, axis=-1, keepdims=True), sink_ref_b.shape)
        m_ref[slot] = jnp.maximum(row_max, sink_ref_b[...])

    def apply(n, slot):
        m = m_ref[slot]
        p = jnp.concatenate([jnp.exp2(s_ref[slot, :, c * HEAD_DIM:(c + 1) * HEAD_DIM] - m)
                             for c in range(WA_BAND // HEAD_DIM)], axis=1).astype(BF16)
        o = jnp.dot(p, v1pad_ref[band_of(n), :], preferred_element_type=F32)
        o = o[:, :HEAD_DIM] / (o[:, HEAD_DIM:] + jnp.exp2(sink_ref_b[...] - m))
        for g in range(GROUP):
            o_ref[rows_of(n), g * HEAD_DIM:(g + 1) * HEAD_DIM] = o[g * WA_ROWS:(g + 1) * WA_ROWS].astype(BF16)

    scores(0, 0)

    def body(i, carry):
        for k in range(WA_UNROLL):
            n = WA_UNROLL * i + k
            scores(n + 1, (k + 1) % 2)
            apply(n, k % 2)
        return carry

    lax.fori_loop(0, n_stages // WA_UNROLL - 1, body, 0)
    for n in range(n_stages - WA_UNROLL, n_stages):
        if n + 1 < n_stages:
            scores(n + 1, (n + 1) % 2)
        apply(n, n % 2)


def _window_attn(proj, bias_lines, sink, batch, seq):
    assert WA_UNROLL % 2 == 0 and (seq // WA_ROWS) % WA_UNROLL == 0, "score slots must stay static in the stage loop"
    q_col = (N_HEADS_A + 2 * N_KV_A) // GROUP
    k_col = N_HEADS_A + 2 * N_KV_A + N_HEADS_B
    v_col = k_col + N_KV_B
    grid_spec = pltpu.PrefetchScalarGridSpec(
        num_scalar_prefetch=1,
        grid=(batch, N_KV_B),
        in_specs=[
            pl.BlockSpec((seq, GROUP * HEAD_DIM), lambda b, k, s: (b, q_col + k)),
            pl.BlockSpec((seq, HEAD_DIM), lambda b, k, s: (b, k_col + k)),
            pl.BlockSpec((seq, HEAD_DIM), lambda b, k, s: (b, v_col + k)),
            pl.BlockSpec((GROUP, F32_SUBLANES, WA_BAND), lambda b, k, s: (k, 0, 0)),
        ],
        out_specs=pl.BlockSpec((seq, GROUP * HEAD_DIM), lambda b, k, s: (b, k)),
        scratch_shapes=[
            pltpu.VMEM((seq + 2 * WINDOW, HEAD_DIM), BF16),
            pltpu.VMEM((seq + 2 * WINDOW, 2 * HEAD_DIM), BF16),
            pltpu.VMEM((GROUP * WA_ROWS, WA_BAND), F32),
            pltpu.VMEM((GROUP * WA_ROWS, HEAD_DIM), F32),
            pltpu.VMEM((2, GROUP * WA_ROWS, WA_BAND), F32),
            pltpu.VMEM((2, GROUP * WA_ROWS, HEAD_DIM), F32),
        ],
    )
    return pl.pallas_call(
        _window_attn_kernel,
        grid_spec=grid_spec,
        out_shape=jax.ShapeDtypeStruct((batch * seq, N_HEADS_B * HEAD_DIM), BF16),
        compiler_params=_params("parallel", "parallel"),
        name="window_attn",
    )(sink, proj, proj, proj, bias_lines)


OUT_TM = 512


def _out_proj_kernel(oa_ref, ob_ref, w_ref, x_ref, g_ref, h_ref, m_ref):
    ka = oa_ref.shape[1]
    for r in range(OUT_TM // SUB_ROWS):
        rows = pl.ds(r * SUB_ROWS, SUB_ROWS)
        h = x_ref[rows, :]
        h = h + jnp.dot(oa_ref[rows, :], w_ref[:ka, :], preferred_element_type=F32)
        h = h + jnp.dot(ob_ref[rows, :], w_ref[ka:, :], preferred_element_type=F32)
        h_ref[rows, :] = h
        m_ref[rows, :] = _rms(h, g_ref[...]).astype(BF16)


def _out_proj(oa, ob, w, x2, g):
    m, d = x2.shape
    ka, kb = oa.shape[1], ob.shape[1]
    return pl.pallas_call(
        _out_proj_kernel,
        grid=(m // OUT_TM,),
        in_specs=[
            pl.BlockSpec((OUT_TM, ka), lambda i: (i, 0)),
            pl.BlockSpec((OUT_TM, kb), lambda i: (i, 0)),
            pl.BlockSpec((ka + kb, d), lambda i: (0, 0)),
            pl.BlockSpec((OUT_TM, d), lambda i: (i, 0)),
            pl.BlockSpec((1, d), lambda i: (0, 0)),
        ],
        out_specs=[pl.BlockSpec((OUT_TM, d), lambda i: (i, 0)), pl.BlockSpec((OUT_TM, d), lambda i: (i, 0))],
        out_shape=[jax.ShapeDtypeStruct((m, d), F32), jax.ShapeDtypeStruct((m, d), BF16)],
        compiler_params=_params("parallel"),
        name="out_proj",
    )(oa, ob, w, x2, g)


MLP_TM = 1024
MLP_TF = 1024


def _mlp_kernel(m_ref, wu_hbm, wd_hbm, d_ref, wu_buf, wd_buf, sem):
    i = pl.program_id(0)
    n_f = wu_hbm.shape[1] // MLP_TF

    def chunk_copies(f, slot):
        cols = pl.ds(pl.multiple_of(f * MLP_TF, MLP_TF), MLP_TF)
        return (pltpu.make_async_copy(wu_hbm.at[:, cols], wu_buf.at[slot], sem.at[0, slot]),
                pltpu.make_async_copy(wd_hbm.at[cols, :], wd_buf.at[slot], sem.at[1, slot]))

    @pl.when(i == 0)
    def _():
        for copy in chunk_copies(0, 0):
            copy.start()

    d_ref[...] = jnp.zeros(d_ref.shape, F32)

    def body(f, carry):
        slot = f % 2
        for copy in chunk_copies(f, slot):
            copy.wait()

        @pl.when((f + 1 < n_f) | (i + 1 < pl.num_programs(0)))
        def _():
            for copy in chunk_copies((f + 1) % n_f, 1 - slot):
                copy.start()

        a = jnp.dot(m_ref[...], wu_buf[slot], preferred_element_type=F32)
        act = jnp.square(jnp.maximum(a, 0.0)).astype(BF16)
        d_ref[...] += jnp.dot(act, wd_buf[slot], preferred_element_type=F32)
        return carry

    lax.fori_loop(0, n_f, body, 0)


def _mlp(mn, wu, wd):
    m, d = mn.shape
    dff = wu.shape[1]
    assert dff % (2 * MLP_TF) == 0, "the weight ring alternates two slots per row tile"
    any_spec = pl.BlockSpec(memory_space=pl.ANY)
    return pl.pallas_call(
        _mlp_kernel,
        grid=(m // MLP_TM,),
        in_specs=[pl.BlockSpec((MLP_TM, d), lambda i: (i, 0)), any_spec, any_spec],
        out_specs=pl.BlockSpec((MLP_TM, d), lambda i: (i, 0)),
        out_shape=jax.ShapeDtypeStruct((m, d), F32),
        scratch_shapes=[
            pltpu.VMEM((2, d, MLP_TF), BF16),
            pltpu.VMEM((2, MLP_TF, d), BF16),
            pltpu.SemaphoreType.DMA((2, 2)),
        ],
        compiler_params=_params("arbitrary"),
        name="mlp",
    )(mn, wu, wd)


GATE_TM = 512


def _gate_final_kernel(h_ref, d_ref, wg_ref, p_ref, wp_ref, gg_ref, gp_ref, gf_ref, o_ref):
    for r in range(GATE_TM // SUB_ROWS):
        rows = pl.ds(r * SUB_ROWS, SUB_ROWS)
        h = h_ref[rows, :] + d_ref[rows, :]
        gate = jax.nn.sigmoid(jnp.dot(_rms(h, gg_ref[...]).astype(BF16), wg_ref[...], preferred_element_type=F32))
        e = _rms(jnp.dot(p_ref[rows, :].astype(BF16), wp_ref[...], preferred_element_type=F32), gp_ref[...])
        o_ref[rows, :] = _rms(h + gate * e, gf_ref[...])


def _gate_final(h, delta, wg, p2, wp, gg, gp, gf):
    m, d = h.shape
    dp = p2.shape[1]
    tile = pl.BlockSpec((GATE_TM, d), lambda i: (i, 0))
    gain = pl.BlockSpec((1, d), lambda i: (0, 0))
    return pl.pallas_call(
        _gate_final_kernel,
        grid=(m // GATE_TM,),
        in_specs=[
            tile,
            tile,
            pl.BlockSpec((d, d), lambda i: (0, 0)),
            pl.BlockSpec((GATE_TM, dp), lambda i: (i, 0)),
            pl.BlockSpec((dp, d), lambda i: (0, 0)),
            gain, gain, gain,
        ],
        out_specs=tile,
        out_shape=jax.ShapeDtypeStruct((m, d), F32),
        compiler_params=_params("parallel"),
        name="gate_final",
    )(h, delta, wg, p2, wp, gg, gp, gf)


def _rope_tables(seq):
    rows = seq // GRID_W
    row = np.repeat(np.arange(rows, dtype=np.float32), GRID_W)
    col = np.tile(np.arange(GRID_W, dtype=np.float32), rows)
    half = HEAD_DIM // 2
    inv_freq = np.float32(ROPE_THETA) ** (-np.arange(0, half, 2, dtype=np.float32) / np.float32(half))
    ang_r = row[:, None] * inv_freq
    ang_c = col[:, None] * inv_freq
    cr, sr, cc, sc = np.cos(ang_r), np.sin(ang_r), np.cos(ang_c), np.sin(ang_c)
    cos = np.concatenate([cr, cr, cc, cc], axis=-1).astype(np.float32)
    sin_signed = np.concatenate([-sr, sr, -sc, sc], axis=-1).astype(np.float32)
    return jnp.asarray(cos), jnp.asarray(sin_signed)


def _t5_bucket(rel):
    nb = N_BUCKETS // 2
    ret = jnp.where(rel > 0, nb, 0)
    n = jnp.abs(rel)
    max_exact = nb // 2
    nf = jnp.maximum(n, 1).astype(F32)
    large = max_exact + (jnp.log(nf / max_exact) / math.log(MAX_DISTANCE / max_exact)
                         * (nb - max_exact)).astype(jnp.int32)
    large = jnp.minimum(large, nb - 1)
    return ret + jnp.where(n < max_exact, n, large)


def _window_bias_lines(rel_bias_table):
    rel = jnp.arange(WA_BAND, dtype=jnp.int32) - WINDOW
    line = rel_bias_table[_t5_bucket(rel)].astype(F32).T * LOG2_E
    line = jnp.where((jnp.abs(rel) <= WINDOW)[None, :], line, NEG_INF)
    return jnp.broadcast_to(line[:, None, :], (line.shape[0], F32_SUBLANES, WA_BAND))


def kernel(x, p, attn_norm_g, w_in, q_norm_g, k_norm_g, sink_logits, w_out, mlp_norm_g, w_up, w_down, ple_w,
           ple_norm_g, gate_norm_g, w_gate, rel_bias_table, final_norm_g):
    batch, seq, d = x.shape
    assert w_in.shape[0] == 1, "gate_final fuses the final RMSNorm, which is only valid for a single layer"
    row = lambda v: v.reshape(1, -1).astype(F32)
    cos, sin_signed = _rope_tables(seq)
    h = x.reshape(batch * seq, d)
    proj = _in_proj(h, row(attn_norm_g), w_in, cos, sin_signed, row(q_norm_g), row(k_norm_g), seq)
    oa, (wu, wd, wo, wg) = _global_attn(proj, (w_up, w_down, w_out, w_gate), batch, seq)
    ob = _window_attn(proj, _window_bias_lines(rel_bias_table), sink_logits.reshape(-1).astype(F32), batch, seq)
    h1, mn = _out_proj(oa, ob, wo, h, row(mlp_norm_g))
    delta = _mlp(mn, wu, wd)
    out = _gate_final(h1, delta, wg, p.reshape(batch * seq, -1), ple_w[0].astype(BF16),
                      row(gate_norm_g), row(ple_norm_g), row(final_norm_g))
    return out.reshape(batch, seq, d)
```

```python
import functools
import math

import jax
import jax.numpy as jnp
import numpy as np
from jax import lax
from jax.experimental import pallas as pl
from jax.experimental.pallas import tpu as pltpu

HEAD_DIM = 128
N_HEADS_A = 8
N_KV_A = 2
N_HEADS_B = 8
N_KV_B = 2
GROUP = 4
GRID_W = 64
WINDOW = 128
N_BUCKETS = 32
MAX_DISTANCE = 128
ROPE_THETA = 10000.0
EPS = 1e-6
NEG_INF = -1e30
LOG2_E = math.log2(math.e)
Q_SCALE = HEAD_DIM ** -0.5 * LOG2_E

V7X_VMEM_BYTES = 64 * 1024 * 1024
VMEM_RESERVE_BYTES = 8 * 1024 * 1024
VMEM_LIMIT_BYTES = V7X_VMEM_BYTES - VMEM_RESERVE_BYTES
F32_SUBLANES = 8
BF16_SUBLANES = 16

SUB_ROWS = 256

BF16 = jnp.bfloat16
F32 = jnp.float32


def _params(*semantics):
    return pltpu.CompilerParams(dimension_semantics=semantics, vmem_limit_bytes=VMEM_LIMIT_BYTES)


def _rms(x, g):
    return x * lax.rsqrt(jnp.mean(x * x, axis=-1, keepdims=True) + EPS) * g


IN_TM = 512
IN_TN = 512
IN_W_SLOTS = 3

_PLAIN, _Q_A, _K_A, _Q_B = range(4)
_HEAD_KINDS = ([_Q_A] * N_HEADS_A + [_K_A] * N_KV_A + [_PLAIN] * N_KV_A
               + [_Q_B] * N_HEADS_B + [_PLAIN] * (2 * N_KV_B))


def _rope(y, cos, sin_signed):
    lane = lax.broadcasted_iota(jnp.int32, y.shape, 1)
    partner = jnp.where((lane % 64) < 32, pltpu.roll(y, 96, 1), pltpu.roll(y, 32, 1))
    return y * cos + partner * sin_signed


def _in_proj_kernel(x_ref, g_ref, w_hbm, cos_ref, sin_ref, gq_ref, gk_ref, o_ref, w_ref, wbuf_ref, wsem):
    heads_per_dot = IN_TN // HEAD_DIM
    n_col_tiles = w_ref.shape[1] // IN_TN

    def normed_rows(r):
        rows = pl.ds(r * SUB_ROWS, SUB_ROWS)
        return rows, _rms(x_ref[rows, :], g_ref[...]).astype(BF16), cos_ref[rows, :], sin_ref[rows, :]

    def column_tile(c, sub_tile):
        rows, u, cos, sin_signed = sub_tile
        acc = jnp.dot(u, w_ref[:, c * IN_TN:(c + 1) * IN_TN], preferred_element_type=F32)
        for hh in range(heads_per_dot):
            head = c * heads_per_dot + hh
            a = acc[:, hh * HEAD_DIM:(hh + 1) * HEAD_DIM]
            kind = _HEAD_KINDS[head]
            if kind == _Q_A:
                a = _rope(_rms(a, gq_ref[...]), cos, sin_signed) * Q_SCALE
            elif kind == _K_A:
                a = _rope(_rms(a, gk_ref[...]), cos, sin_signed)
            elif kind == _Q_B:
                a = a * Q_SCALE
            o_ref[rows, head * HEAD_DIM:(head + 1) * HEAD_DIM] = a.astype(BF16)

    @pl.when(pl.program_id(0) == 0)
    def _():
        def tile_copy(c):
            src = w_hbm.at[0, :, pl.ds(c * IN_TN, IN_TN)]
            return pltpu.make_async_copy(src, wbuf_ref.at[c % IN_W_SLOTS], wsem.at[c % IN_W_SLOTS])

        for c in range(IN_W_SLOTS - 1):
            tile_copy(c).start()
        sub_tiles = [normed_rows(r) for r in range(IN_TM // SUB_ROWS)]
        for c in range(n_col_tiles):
            tile_copy(c).wait()
            if c + IN_W_SLOTS - 1 < n_col_tiles:
                tile_copy(c + IN_W_SLOTS - 1).start()
            w_ref[:, c * IN_TN:(c + 1) * IN_TN] = wbuf_ref[c % IN_W_SLOTS].astype(BF16)
            for sub_tile in sub_tiles:
                column_tile(c, sub_tile)

    @pl.when(pl.program_id(0) > 0)
    def _():
        for r in range(IN_TM // SUB_ROWS):
            sub_tile = normed_rows(r)
            for c in range(n_col_tiles):
                column_tile(c, sub_tile)


def _in_proj(x2, g, w, cos, sin_signed, gq, gk, seq):
    m, d = x2.shape
    n = w.shape[2]
    assert w.shape[0] == 1 and n == len(_HEAD_KINDS) * HEAD_DIM
    pos_tiles = seq // IN_TM
    return pl.pallas_call(
        _in_proj_kernel,
        grid=(m // IN_TM,),
        in_specs=[
            pl.BlockSpec((IN_TM, d), lambda i: (i, 0)),
            pl.BlockSpec((1, d), lambda i: (0, 0)),
            pl.BlockSpec(memory_space=pl.ANY),
            pl.BlockSpec((IN_TM, HEAD_DIM), lambda i: (i % pos_tiles, 0)),
            pl.BlockSpec((IN_TM, HEAD_DIM), lambda i: (i % pos_tiles, 0)),
            pl.BlockSpec((1, HEAD_DIM), lambda i: (0, 0)),
            pl.BlockSpec((1, HEAD_DIM), lambda i: (0, 0)),
        ],
        out_specs=pl.BlockSpec((IN_TM, n), lambda i: (i, 0)),
        out_shape=jax.ShapeDtypeStruct((m, n), BF16),
        scratch_shapes=[
            pltpu.VMEM((d, n), BF16),
            pltpu.VMEM((IN_W_SLOTS, d, IN_TN), F32),
            pltpu.SemaphoreType.DMA((IN_W_SLOTS,)),
        ],
        compiler_params=_params("arbitrary"),
        name="in_proj",
    )(x2, g, w, cos, sin_signed, gq, gk)


GA_ROWS = 256
GA_KEYS = 512


def _global_attn_kernel(*refs, n_cast):
    q_ref, k_ref, v_ref = refs[:3]
    w_f32 = refs[3:3 + n_cast]
    o_ref = refs[3 + n_cast]
    w_bf16 = refs[4 + n_cast:4 + 2 * n_cast]
    v1_ref, kt_ref, s_ref, m_ref = refs[4 + 2 * n_cast:8 + 2 * n_cast]
    in_bufs = refs[8 + 2 * n_cast:8 + 3 * n_cast]
    out_bufs = refs[8 + 3 * n_cast:8 + 4 * n_cast]
    in_sem, out_sem = refs[8 + 4 * n_cast:]

    seq = q_ref.shape[0]
    n_blocks = seq // GA_ROWS
    first_chunk = (pl.program_id(0) * pl.num_programs(1) + pl.program_id(1)) * n_blocks

    def in_copy(w, chunk, slot):
        rows = in_bufs[w].shape[1]
        src = w_f32[w].at[0, pl.ds(pl.multiple_of(chunk * rows, rows), rows)]
        return pltpu.make_async_copy(src, in_bufs[w].at[slot], in_sem.at[w, slot])

    def out_copy(w, chunk):
        rows = out_bufs[w].shape[0]
        dst = w_bf16[w].at[pl.ds(pl.multiple_of(chunk * rows, rows), rows)]
        return pltpu.make_async_copy(out_bufs[w], dst, out_sem.at[w])

    def cast_chunk(r, first, last):
        chunk = first_chunk + r
        slot = r % 2
        for w in range(n_cast):
            in_copy(w, chunk, slot).wait()
            if not last:
                in_copy(w, chunk + 1, 1 - slot).start()
            if not first:
                out_copy(w, chunk - 1).wait()
            out_bufs[w][...] = in_bufs[w][slot].astype(BF16)
            out_copy(w, chunk).start()

    for w in range(n_cast):
        in_copy(w, first_chunk, 0).start()

    v1_ref[:, :HEAD_DIM] = v_ref[...]
    v1_ref[:, HEAD_DIM:] = jnp.ones(v_ref.shape, BF16)
    kt_ref[...] = k_ref[...].T

    def rows_of(r):
        return pl.ds(pl.multiple_of(r * GA_ROWS, GA_ROWS), GA_ROWS)

    def scores(r, g, slot):
        q = q_ref[rows_of(r), g * HEAD_DIM:(g + 1) * HEAD_DIM]
        part = None
        for c in range(seq // GA_KEYS):
            keys = slice(c * GA_KEYS, (c + 1) * GA_KEYS)
            s = jnp.dot(q, kt_ref[:, keys], preferred_element_type=F32)
            s_ref[slot, :, keys] = s
            for l in range(GA_KEYS // HEAD_DIM):
                piece = s[:, l * HEAD_DIM:(l + 1) * HEAD_DIM]
                part = piece if part is None else jnp.maximum(part, piece)
        m_ref[slot] = jnp.broadcast_to(jnp.max(part, axis=-1, keepdims=True), (GA_ROWS, HEAD_DIM))

    def apply(r, g, slot):
        m = m_ref[slot]
        o = jnp.zeros((GA_ROWS, 2 * HEAD_DIM), F32)
        for c in range(seq // GA_KEYS):
            keys = slice(c * GA_KEYS, (c + 1) * GA_KEYS)
            p = jnp.concatenate([jnp.exp2(s_ref[slot, :, c * GA_KEYS + l * HEAD_DIM:c * GA_KEYS + (l + 1) * HEAD_DIM] - m)
                                 for l in range(GA_KEYS // HEAD_DIM)], axis=1).astype(BF16)
            o = o + jnp.dot(p, v1_ref[keys, :], preferred_element_type=F32)
        o_ref[rows_of(r), g * HEAD_DIM:(g + 1) * HEAD_DIM] = (o[:, :HEAD_DIM] / o[:, HEAD_DIM:]).astype(BF16)

    def row_block(r, first, last):
        for g in range(GROUP):
            slot = g % 2
            if g + 1 < GROUP:
                scores(r, g + 1, 1 - slot)
            elif not last:
                scores(r + 1, 0, 1 - slot)
            apply(r, g, slot)
        cast_chunk(r, first, last)

    scores(0, 0, 0)
    row_block(0, first=True, last=False)

    def body(r, carry):
        row_block(r, first=False, last=False)
        return carry

    lax.fori_loop(1, n_blocks - 1, body, 0)
    row_block(n_blocks - 1, first=False, last=True)
    for w in range(n_cast):
        out_copy(w, first_chunk + n_blocks - 1).wait()


def _global_attn(proj, weights, batch, seq):
    assert GROUP % 2 == 0, "score slots alternate per head and must line up across row blocks"
    n_blocks = seq // GA_ROWS
    assert n_blocks >= 3 and n_blocks % 2 == 0
    n_chunks = batch * N_KV_A * n_blocks
    chunk_rows = []
    for w in weights:
        assert w.shape[0] == 1 and w.shape[1] % (BF16_SUBLANES * n_chunks) == 0, w.shape
        chunk_rows.append(w.shape[1] // n_chunks)
    n_cast = len(weights)
    k_col = N_HEADS_A
    v_col = N_HEADS_A + N_KV_A
    any_spec = pl.BlockSpec(memory_space=pl.ANY)
    attn_spec = pl.BlockSpec((seq, GROUP * HEAD_DIM), lambda b, k: (b, k))
    outs = pl.pallas_call(
        functools.partial(_global_attn_kernel, n_cast=n_cast),
        grid=(batch, N_KV_A),
        in_specs=[
            attn_spec,
            pl.BlockSpec((seq, HEAD_DIM), lambda b, k: (b, k_col + k)),
            pl.BlockSpec((seq, HEAD_DIM), lambda b, k: (b, v_col + k)),
        ] + [any_spec] * n_cast,
        out_specs=[attn_spec] + [any_spec] * n_cast,
        out_shape=[jax.ShapeDtypeStruct((batch * seq, N_HEADS_A * HEAD_DIM), BF16)]
        + [jax.ShapeDtypeStruct(w.shape[1:], BF16) for w in weights],
        scratch_shapes=[
            pltpu.VMEM((seq, 2 * HEAD_DIM), BF16),
            pltpu.VMEM((HEAD_DIM, seq), BF16),
            pltpu.VMEM((2, GA_ROWS, seq), F32),
            pltpu.VMEM((2, GA_ROWS, HEAD_DIM), F32),
        ] + [pltpu.VMEM((2, rows, w.shape[2]), F32) for rows, w in zip(chunk_rows, weights)]
        + [pltpu.VMEM((rows, w.shape[2]), BF16) for rows, w in zip(chunk_rows, weights)]
        + [pltpu.SemaphoreType.DMA((n_cast, 2)), pltpu.SemaphoreType.DMA((n_cast,))],
        compiler_params=_params("arbitrary", "arbitrary"),
        name="global_attn",
    )(proj, proj, proj, *weights)
    return outs[0], outs[1:]


WA_ROWS = 256
WA_BAND = WA_ROWS + 2 * WINDOW
WA_UNROLL = 4


def _window_attn_kernel(sink_ref, q_ref, k_ref, v_ref, line_ref, o_ref,
                        kpad_ref, v1pad_ref, bias_ref, sink_ref_b, s_ref, m_ref):
    kvh = pl.program_id(1)
    seq = q_ref.shape[0]
    n_stages = seq // WA_ROWS

    zeros = jnp.zeros((WINDOW, 2 * HEAD_DIM), BF16)
    kpad_ref[:WINDOW, :] = zeros[:, :HEAD_DIM]
    kpad_ref[WINDOW + seq:, :] = zeros[:, :HEAD_DIM]
    kpad_ref[WINDOW:WINDOW + seq, :] = k_ref[...]
    v1pad_ref[:WINDOW, :] = zeros
    v1pad_ref[WINDOW + seq:, :] = zeros
    v1pad_ref[WINDOW:WINDOW + seq, :HEAD_DIM] = v_ref[...]
    v1pad_ref[WINDOW:WINDOW + seq, HEAD_DIM:] = jnp.ones(v_ref.shape, BF16)

    for g in range(GROUP):
        head = kvh * GROUP + g
        tile = jnp.broadcast_to(line_ref[g, :1, :], (WA_ROWS, WA_BAND))
        bias_ref[g * WA_ROWS:(g + 1) * WA_ROWS, :] = pltpu.roll(tile, 0, 1, stride=1, stride_axis=0)
        sink_ref_b[g * WA_ROWS:(g + 1) * WA_ROWS, :] = jnp.full((WA_ROWS, HEAD_DIM), sink_ref[head] * LOG2_E, F32)

    col = lax.broadcasted_iota(jnp.int32, (GROUP * WA_ROWS, WA_BAND), 1)

    def rows_of(n):
        return pl.ds(pl.multiple_of(n * WA_ROWS, WA_ROWS), WA_ROWS)

    def band_of(n):
        return pl.ds(pl.multiple_of(n * WA_ROWS, WA_ROWS), WA_BAND)

    def scores(n, slot):
        q = jnp.concatenate([q_ref[rows_of(n), g * HEAD_DIM:(g + 1) * HEAD_DIM] for g in range(GROUP)], axis=0)
        s = lax.dot_general(q, kpad_ref[band_of(n), :], (((1,), (1,)), ((), ())), preferred_element_type=F32)
        in_range = ((col >= WINDOW) | (n > 0)) & ((col < WA_BAND - WINDOW) | (n < n_stages - 1))
        s = jnp.where(in_range, s + bias_ref[...], NEG_INF)
        s_ref[slot] = s
        row_max = jnp.broadcast_to(jnp.max(s, axis=-1, keepdims=True), sink_ref_b.shape)
        m_ref[slot] = jnp.maximum(row_max, sink_ref_b[...])

    def apply(n, slot):
        m = m_ref[slot]
        p = jnp.concatenate([jnp.exp2(s_ref[slot, :, c * HEAD_DIM:(c + 1) * HEAD_DIM] - m)
                             for c in range(WA_BAND // HEAD_DIM)], axis=1).astype(BF16)
        o = jnp.dot(p, v1pad_ref[band_of(n), :], preferred_element_type=F32)
        o = o[:, :HEAD_DIM] / (o[:, HEAD_DIM:] + jnp.exp2(sink_ref_b[...] - m))
        for g in range(GROUP):
            o_ref[rows_of(n), g * HEAD_DIM:(g + 1) * HEAD_DIM] = o[g * WA_ROWS:(g + 1) * WA_ROWS].astype(BF16)

    scores(0, 0)

    def body(i, carry):
        for k in range(WA_UNROLL):
            n = WA_UNROLL * i + k
            scores(n + 1, (k + 1) % 2)
            apply(n, k % 2)
        return carry

    lax.fori_loop(0, n_stages // WA_UNROLL - 1, body, 0)
    for n in range(n_stages - WA_UNROLL, n_stages):
        if n + 1 < n_stages:
            scores(n + 1, (n + 1) % 2)
        apply(n, n % 2)


def _window_attn(proj, bias_lines, sink, batch, seq):
    assert WA_UNROLL % 2 == 0 and (seq // WA_ROWS) % WA_UNROLL == 0, "score slots must stay static in the stage loop"
    q_col = (N_HEADS_A + 2 * N_KV_A) // GROUP
    k_col = N_HEADS_A + 2 * N_KV_A + N_HEADS_B
    v_col = k_col + N_KV_B
    grid_spec = pltpu.PrefetchScalarGridSpec(
        num_scalar_prefetch=1,
        grid=(batch, N_KV_B),
        in_specs=[
            pl.BlockSpec((seq, GROUP * HEAD_DIM), lambda b, k, s: (b, q_col + k)),
            pl.BlockSpec((seq, HEAD_DIM), lambda b, k, s: (b, k_col + k)),
            pl.BlockSpec((seq, HEAD_DIM), lambda b, k, s: (b, v_col + k)),
            pl.BlockSpec((GROUP, F32_SUBLANES, WA_BAND), lambda b, k, s: (k, 0, 0)),
        ],
        out_specs=pl.BlockSpec((seq, GROUP * HEAD_DIM), lambda b, k, s: (b, k)),
        scratch_shapes=[
            pltpu.VMEM((seq + 2 * WINDOW, HEAD_DIM), BF16),
            pltpu.VMEM((seq + 2 * WINDOW, 2 * HEAD_DIM), BF16),
            pltpu.VMEM((GROUP * WA_ROWS, WA_BAND), F32),
            pltpu.VMEM((GROUP * WA_ROWS, HEAD_DIM), F32),
            pltpu.VMEM((2, GROUP * WA_ROWS, WA_BAND), F32),
            pltpu.VMEM((2, GROUP * WA_ROWS, HEAD_DIM), F32),
        ],
    )
    return pl.pallas_call(
        _window_attn_kernel,
        grid_spec=grid_spec,
        out_shape=jax.ShapeDtypeStruct((batch * seq, N_HEADS_B * HEAD_DIM), BF16),
        compiler_params=_params("parallel", "parallel"),
        name="window_attn",
    )(sink, proj, proj, proj, bias_lines)


OUT_TM = 512


def _out_proj_kernel(oa_ref, ob_ref, w_ref, x_ref, g_ref, h_ref, m_ref):
    ka = oa_ref.shape[1]
    for r in range(OUT_TM // SUB_ROWS):
        rows = pl.ds(r * SUB_ROWS, SUB_ROWS)
        h = x_ref[rows, :]
        h = h + jnp.dot(oa_ref[rows, :], w_ref[:ka, :], preferred_element_type=F32)
        h = h + jnp.dot(ob_ref[rows, :], w_ref[ka:, :], preferred_element_type=F32)
        h_ref[rows, :] = h
        m_ref[rows, :] = _rms(h, g_ref[...]).astype(BF16)


def _out_proj(oa, ob, w, x2, g):
    m, d = x2.shape
    ka, kb = oa.shape[1], ob.shape[1]
    return pl.pallas_call(
        _out_proj_kernel,
        grid=(m // OUT_TM,),
        in_specs=[
            pl.BlockSpec((OUT_TM, ka), lambda i: (i, 0)),
            pl.BlockSpec((OUT_TM, kb), lambda i: (i, 0)),
            pl.BlockSpec((ka + kb, d), lambda i: (0, 0)),
            pl.BlockSpec((OUT_TM, d), lambda i: (i, 0)),
            pl.BlockSpec((1, d), lambda i: (0, 0)),
        ],
        out_specs=[pl.BlockSpec((OUT_TM, d), lambda i: (i, 0)), pl.BlockSpec((OUT_TM, d), lambda i: (i, 0))],
        out_shape=[jax.ShapeDtypeStruct((m, d), F32), jax.ShapeDtypeStruct((m, d), BF16)],
        compiler_params=_params("parallel"),
        name="out_proj",
    )(oa, ob, w, x2, g)


MLP_TM = 1024
MLP_TF = 1024


def _mlp_kernel(m_ref, wu_hbm, wd_hbm, d_ref, wu_buf, wd_buf, sem):
    i = pl.program_id(0)
    n_f = wu_hbm.shape[1] // MLP_TF

    def chunk_copies(f, slot):
        cols = pl.ds(pl.multiple_of(f * MLP_TF, MLP_TF), MLP_TF)
        return (pltpu.make_async_copy(wu_hbm.at[:, cols], wu_buf.at[slot], sem.at[0, slot]),
                pltpu.make_async_copy(wd_hbm.at[cols, :], wd_buf.at[slot], sem.at[1, slot]))

    @pl.when(i == 0)
    def _():
        for copy in chunk_copies(0, 0):
            copy.start()

    d_ref[...] = jnp.zeros(d_ref.shape, F32)

    def chunk(f, slot):
        for copy in chunk_copies(f, slot):
            copy.wait()

        @pl.when((f + 1 < n_f) | (i + 1 < pl.num_programs(0)))
        def _():
            for copy in chunk_copies((f + 1) % n_f, 1 - slot):
                copy.start()

        a = jnp.dot(m_ref[...], wu_buf[slot], preferred_element_type=F32)
        act = jnp.square(jnp.maximum(a, 0.0)).astype(BF16)
        d_ref[...] += jnp.dot(act, wd_buf[slot], preferred_element_type=F32)

    def body(j, carry):
        chunk(2 * j, 0)
        chunk(2 * j + 1, 1)
        return carry

    lax.fori_loop(0, n_f // 2, body, 0)


def _mlp(mn, wu, wd):
    m, d = mn.shape
    dff = wu.shape[1]
    assert dff % (2 * MLP_TF) == 0, "the weight ring alternates two slots per row tile"
    any_spec = pl.BlockSpec(memory_space=pl.ANY)
    return pl.pallas_call(
        _mlp_kernel,
        grid=(m // MLP_TM,),
        in_specs=[pl.BlockSpec((MLP_TM, d), lambda i: (i, 0)), any_spec, any_spec],
        out_specs=pl.BlockSpec((MLP_TM, d), lambda i: (i, 0)),
        out_shape=jax.ShapeDtypeStruct((m, d), F32),
        scratch_shapes=[
            pltpu.VMEM((2, d, MLP_TF), BF16),
            pltpu.VMEM((2, MLP_TF, d), BF16),
            pltpu.SemaphoreType.DMA((2, 2)),
        ],
        compiler_params=_params("arbitrary"),
        name="mlp",
    )(mn, wu, wd)


GATE_TM = 512


def _gate_final_kernel(h_ref, d_ref, wg_ref, p_ref, wp_ref, gg_ref, gp_ref, gf_ref, o_ref):
    for r in range(GATE_TM // SUB_ROWS):
        rows = pl.ds(r * SUB_ROWS, SUB_ROWS)
        h = h_ref[rows, :] + d_ref[rows, :]
        gate = jax.nn.sigmoid(jnp.dot(_rms(h, gg_ref[...]).astype(BF16), wg_ref[...], preferred_element_type=F32))
        e = _rms(jnp.dot(p_ref[rows, :].astype(BF16), wp_ref[...], preferred_element_type=F32), gp_ref[...])
        o_ref[rows, :] = _rms(h + gate * e, gf_ref[...])


def _gate_final(h, delta, wg, p2, wp, gg, gp, gf):
    m, d = h.shape
    dp = p2.shape[1]
    tile = pl.BlockSpec((GATE_TM, d), lambda i: (i, 0))
    gain = pl.BlockSpec((1, d), lambda i: (0, 0))
    return pl.pallas_call(
        _gate_final_kernel,
        grid=(m // GATE_TM,),
        in_specs=[
            tile,
            tile,
            pl.BlockSpec((d, d), lambda i: (0, 0)),
            pl.BlockSpec((GATE_TM, dp), lambda i: (i, 0)),
            pl.BlockSpec((dp, d), lambda i: (0, 0)),
            gain, gain, gain,
        ],
        out_specs=tile,
        out_shape=jax.ShapeDtypeStruct((m, d), F32),
        compiler_params=_params("parallel"),
        name="gate_final",
    )(h, delta, wg, p2, wp, gg, gp, gf)


def _rope_tables(seq):
    rows = seq // GRID_W
    row = np.repeat(np.arange(rows, dtype=np.float32), GRID_W)
    col = np.tile(np.arange(GRID_W, dtype=np.float32), rows)
    half = HEAD_DIM // 2
    inv_freq = np.float32(ROPE_THETA) ** (-np.arange(0, half, 2, dtype=np.float32) / np.float32(half))
    ang_r = row[:, None] * inv_freq
    ang_c = col[:, None] * inv_freq
    cr, sr, cc, sc = np.cos(ang_r), np.sin(ang_r), np.cos(ang_c), np.sin(ang_c)
    cos = np.concatenate([cr, cr, cc, cc], axis=-1).astype(np.float32)
    sin_signed = np.concatenate([-sr, sr, -sc, sc], axis=-1).astype(np.float32)
    return jnp.asarray(cos), jnp.asarray(sin_signed)


def _t5_bucket(rel):
    nb = N_BUCKETS // 2
    ret = jnp.where(rel > 0, nb, 0)
    n = jnp.abs(rel)
    max_exact = nb // 2
    nf = jnp.maximum(n, 1).astype(F32)
    large = max_exact + (jnp.log(nf / max_exact) / math.log(MAX_DISTANCE / max_exact)
                         * (nb - max_exact)).astype(jnp.int32)
    large = jnp.minimum(large, nb - 1)
    return ret + jnp.where(n < max_exact, n, large)


def _window_bias_lines(rel_bias_table):
    rel = jnp.arange(WA_BAND, dtype=jnp.int32) - WINDOW
    line = rel_bias_table[_t5_bucket(rel)].astype(F32).T * LOG2_E
    line = jnp.where((jnp.abs(rel) <= WINDOW)[None, :], line, NEG_INF)
    return jnp.broadcast_to(line[:, None, :], (line.shape[0], F32_SUBLANES, WA_BAND))


def kernel(x, p, attn_norm_g, w_in, q_norm_g, k_norm_g, sink_logits, w_out, mlp_norm_g, w_up, w_down, ple_w,
           ple_norm_g, gate_norm_g, w_gate, rel_bias_table, final_norm_g):
    batch, seq, d = x.shape
    assert w_in.shape[0] == 1, "gate_final fuses the final RMSNorm, which is only valid for a single layer"
    row = lambda v: v.reshape(1, -1).astype(F32)
    cos, sin_signed = _rope_tables(seq)
    h = x.reshape(batch * seq, d)
    proj = _in_proj(h, row(attn_norm_g), w_in, cos, sin_signed, row(q_norm_g), row(k_norm_g), seq)
    oa, (wu, wd, wo, wg) = _global_attn(proj, (w_up, w_down, w_out, w_gate), batch, seq)
    ob = _window_attn(proj, _window_bias_lines(rel_bias_table), sink_logits.reshape(-1).astype(F32), batch, seq)
    h1, mn = _out_proj(oa, ob, wo, h, row(mlp_norm_g))
    delta = _mlp(mn, wu, wd)
    out = _gate_final(h1, delta, wg, p.reshape(batch * seq, -1), ple_w[0].astype(BF16),
                      row(gate_norm_g), row(ple_norm_g), row(final_norm_g))
    return out.reshape(batch, seq, d)
```
